```python
import math
import jax, jax.numpy as jnp
from jax import lax
import numpy as np

D_MODEL = 2048
BATCH = 1
SEQ = 8192
DEPTH = 2

D_CONV = 1024
CONV_K = 31
N_HEADS = 8
HEAD_DIM = 128
D_ATTN = N_HEADS * HEAD_DIM
MOBA_BLOCK = 256
MOBA_TOPK = 3
Q_CHUNK = 128
N_BUCKETS = 32
REL_MAX_DIST = 2048
D_FF = 5632
FFN_CONV_K = 3
EPS = 1e-6
NEG = -1e30

D_IN_TOTAL = 2 * D_CONV + 3 * D_ATTN + 2 * D_MODEL

kernel_name = "hybrid_conformer_moba_gated_block"


def _rmsnorm(x, g):
    xf = x.astype(jnp.float32)
    y = xf * lax.rsqrt(jnp.mean(xf * xf, axis=-1, keepdims=True) + EPS)
    return (y * g.astype(jnp.float32)).astype(x.dtype)


def _layernorm(x, g, b):
    xf = x.astype(jnp.float32)
    mu = jnp.mean(xf, axis=-1, keepdims=True)
    var = jnp.mean(jnp.square(xf - mu), axis=-1, keepdims=True)
    y = (xf - mu) * lax.rsqrt(var + EPS)
    return (y * g.astype(jnp.float32) + b.astype(jnp.float32)).astype(x.dtype)


def _causal_dwconv(x, w, b):
    K, C = w.shape
    y = lax.conv_general_dilated(
        x, w[:, None, :].astype(x.dtype), window_strides=(1,), padding=[(K - 1, 0)],
        dimension_numbers=("NWC", "WIO", "NWC"), feature_group_count=C)
    return y + b


def _t5_bucket(dist):
    n = jnp.maximum(dist, 0)
    max_exact = N_BUCKETS // 2
    nf = jnp.maximum(n, 1).astype(jnp.float32)
    large = max_exact + (jnp.log(nf / max_exact) / math.log(REL_MAX_DIST / max_exact)
                         * (N_BUCKETS - max_exact)).astype(jnp.int32)
    large = jnp.minimum(large, N_BUCKETS - 1)
    return jnp.where(n < max_exact, n, large)


def _gather_blocks(blocks, idx):
    return jax.vmap(jax.vmap(lambda kb, i: kb[i]))(blocks, idx)


def _moba_attention(q, k, v, rel_bias):
    B, S, H, D = q.shape
    S_pad = ((S + MOBA_BLOCK - 1) // MOBA_BLOCK) * MOBA_BLOCK
    pad = S_pad - S
    padw = ((0, 0), (0, pad), (0, 0), (0, 0))
    q = jnp.pad(q, padw).transpose(0, 2, 1, 3)
    k = jnp.pad(k, padw).transpose(0, 2, 1, 3)
    v = jnp.pad(v, padw).transpose(0, 2, 1, 3)
    nb = S_pad // MOBA_BLOCK
    k_sel = min(MOBA_TOPK, nb)
    n_chunks = S_pad // Q_CHUNK
    kb = k.reshape(B, H, nb, MOBA_BLOCK, D)
    vb = v.reshape(B, H, nb, MOBA_BLOCK, D)
    kmean = jnp.mean(kb.astype(jnp.float32), axis=3)
    scale = D ** -0.5
    bias_hT = rel_bias.T.astype(jnp.float32)
    head_ix = jnp.arange(H)[None, :, None, None, None]
    blk_off = jnp.arange(MOBA_BLOCK)

    def one_chunk(c):
        q0 = c * Q_CHUNK
        qc = lax.dynamic_slice_in_dim(q, q0, Q_CHUNK, axis=2)
        qpos = q0 + jnp.arange(Q_CHUNK)
        own = q0 // MOBA_BLOCK
        gate = jnp.einsum("bhqd,bhnd->bhqn", qc.astype(jnp.float32), kmean)
        gate = jnp.where(jnp.arange(nb) < own, gate, NEG)
        _, sel = lax.top_k(gate, k_sel)
        sel_valid = jnp.arange(k_sel) < jnp.minimum(k_sel, own)
        ks = _gather_blocks(kb, sel)
        vs = _gather_blocks(vb, sel)
        s_sel = jnp.einsum("bhqd,bhqkjd->bhqkj", qc, ks).astype(jnp.float32) * scale
        key_pos = sel[..., None] * MOBA_BLOCK + blk_off
        dist = qpos[None, None, :, None, None] - key_pos
        s_sel = s_sel + bias_hT[head_ix, _t5_bucket(dist)]
        s_sel = jnp.where(sel_valid[:, None], s_sel, NEG)
        k_own = lax.dynamic_index_in_dim(kb, own, axis=2, keepdims=False)
        v_own = lax.dynamic_index_in_dim(vb, own, axis=2, keepdims=False)
        s_own = jnp.einsum("bhqd,bhjd->bhqj", qc, k_own).astype(jnp.float32) * scale
        d_own = qpos[:, None] - (own * MOBA_BLOCK + blk_off)[None, :]
        s_own = s_own + bias_hT[:, _t5_bucket(d_own)][None]
        s_own = jnp.where(d_own >= 0, s_own, NEG)
        logits = jnp.concatenate(
            [s_sel.reshape(B, H, Q_CHUNK, k_sel * MOBA_BLOCK), s_own], axis=-1)
        p = jax.nn.softmax(logits, axis=-1).astype(v.dtype)
        p_sel = p[..., :k_sel * MOBA_BLOCK].reshape(B, H, Q_CHUNK, k_sel, MOBA_BLOCK)
        p_own = p[..., k_sel * MOBA_BLOCK:]
        o = (jnp.einsum("bhqkj,bhqkjd->bhqd", p_sel, vs)
             + jnp.einsum("bhqj,bhjd->bhqd", p_own, v_own))
        return o

    out = lax.map(one_chunk, jnp.arange(n_chunks))
    out = out.transpose(1, 0, 3, 2, 4).reshape(B, S_pad, H * D)
    return out[:, :S]


def setup_inputs(seed: int = 0) -> dict:
    key = jax.random.key(seed)
    ks = jax.random.split(key, 20)
    f32 = jnp.float32

    def nrm(k, shape, scale):
        return jax.random.normal(k, shape, f32) * scale

    return {
        "x": nrm(ks[0], (BATCH, SEQ, D_MODEL), 1.0),
        "norm1_g": 1.0 + nrm(ks[1], (DEPTH, D_MODEL), 0.02),
        "w_in": nrm(ks[2], (DEPTH, D_MODEL, D_IN_TOTAL), D_MODEL ** -0.5),
        "conv_dw": nrm(ks[3], (DEPTH, CONV_K, D_CONV), CONV_K ** -0.5),
        "conv_dw_b": nrm(ks[4], (DEPTH, D_CONV), 0.02),
        "conv_ln_g": 1.0 + nrm(ks[5], (DEPTH, D_CONV), 0.02),
        "conv_ln_b": nrm(ks[6], (DEPTH, D_CONV), 0.02),
        "w_conv_out": nrm(ks[7], (DEPTH, D_CONV, D_MODEL), D_CONV ** -0.5),
        "rel_bias": nrm(ks[8], (N_BUCKETS, N_HEADS), 0.5),
        "w_attn_out": nrm(ks[9], (DEPTH, D_ATTN, D_MODEL), D_ATTN ** -0.5),
        "w_out": nrm(ks[10], (DEPTH, D_MODEL, D_MODEL), D_MODEL ** -0.5),
        "norm2_g": 1.0 + nrm(ks[11], (DEPTH, D_MODEL), 0.02),
        "w_up": nrm(ks[12], (DEPTH, D_MODEL, 2 * D_FF), D_MODEL ** -0.5),
        "ffn_dw": nrm(ks[13], (DEPTH, FFN_CONV_K, 2 * D_FF), FFN_CONV_K ** -0.5),
        "ffn_dw_b": nrm(ks[14], (DEPTH, 2 * D_FF), 0.02),
        "w_down": nrm(ks[15], (DEPTH, D_FF, D_MODEL), D_FF ** -0.5),
        "final_g": 1.0 + nrm(ks[16], (D_MODEL,), 0.02),
    }


def reference(x, norm1_g, w_in, conv_dw, conv_dw_b, conv_ln_g, conv_ln_b, w_conv_out,
              rel_bias, w_attn_out, w_out, norm2_g, w_up, ffn_dw, ffn_dw_b, w_down,
              final_g):
    B, S, _ = x.shape
    c1 = D_CONV
    c2 = 2 * D_CONV
    a1 = c2 + D_ATTN
    a2 = a1 + D_ATTN
    a3 = a2 + D_ATTN
    g1 = a3 + D_MODEL
    for l in range(DEPTH):
        h = _rmsnorm(x, norm1_g[l])
        proj = h @ w_in[l]
        ca = proj[..., :c1] * jax.nn.sigmoid(proj[..., c1:c2])
        ca = _causal_dwconv(ca, conv_dw[l], conv_dw_b[l])
        ca = jax.nn.silu(_layernorm(ca, conv_ln_g[l], conv_ln_b[l]))
        y_conv = ca @ w_conv_out[l]
        q = proj[..., c2:a1].reshape(B, S, N_HEADS, HEAD_DIM)
        k = proj[..., a1:a2].reshape(B, S, N_HEADS, HEAD_DIM)
        v = proj[..., a2:a3].reshape(B, S, N_HEADS, HEAD_DIM)
        y_attn = _moba_attention(q, k, v, rel_bias) @ w_attn_out[l]
        merged = (jax.nn.sigmoid(proj[..., a3:g1]) * y_conv
                  + jax.nn.sigmoid(proj[..., g1:]) * y_attn)
        x = x + merged @ w_out[l]
        h = _rmsnorm(x, norm2_g[l])
        u = _causal_dwconv(h @ w_up[l], ffn_dw[l], ffn_dw_b[l])
        x = x + (jax.nn.silu(u[..., D_FF:]) * u[..., :D_FF]) @ w_down[l]
    return _rmsnorm(x, final_g)
```

```python
import functools
import math

import jax
import jax.numpy as jnp
from jax import lax
from jax.experimental import pallas as pl
from jax.experimental.pallas import tpu as pltpu

F32 = jnp.float32
BF16 = jnp.bfloat16

D_MODEL = 2048
D_CONV = 1024
CONV_K = 31
N_HEADS = 8
HEAD_DIM = 128
D_ATTN = N_HEADS * HEAD_DIM
MOBA_BLOCK = 256
MOBA_TOPK = 3
N_BUCKETS = 32
REL_MAX_DIST = 2048
D_FF = 5632
FFN_CONV_K = 3
EPS = 1e-6
NEG = -1e30

N_BIAS_TILES = 8

V7X_VMEM_BYTES = 64 * 1024 * 1024
VMEM_LIMIT = V7X_VMEM_BYTES - 8 * 1024 * 1024

NORM_ROWS = 16
HALO = 8
CONV_HALO = 32


def _params(sem):
    return pltpu.CompilerParams(dimension_semantics=sem, vmem_limit_bytes=VMEM_LIMIT)


def _rmsnorm_rows(x_ref, g_ref, dst_ref, n_rows, dst_off=0, out_dtype=BF16):
    def body(c, carry):
        r = pl.multiple_of(c * NORM_ROWS, NORM_ROWS)
        xv = x_ref[pl.ds(r, NORM_ROWS), :]
        ms = jnp.mean(xv * xv, axis=-1, keepdims=True)
        y = xv * lax.rsqrt(ms + EPS) * g_ref[...]
        dst_ref[pl.ds(dst_off + r, NORM_ROWS), :] = y.astype(out_dtype)
        return carry
    lax.fori_loop(0, n_rows // NORM_ROWS, body, 0)


def _inproj_glu_kernel(x_ref, g_ref, wa_ref, wb_ref, o_ref, h_ref):
    @pl.when(pl.program_id(1) == 0)
    def _():
        _rmsnorm_rows(x_ref, g_ref, h_ref, x_ref.shape[0])
    h = h_ref[...]
    a = jnp.dot(h, wa_ref[...], preferred_element_type=F32)
    b = jnp.dot(h, wb_ref[...], preferred_element_type=F32)
    o_ref[...] = a * jax.nn.sigmoid(b)


def _inproj_qkv_kernel(x_ref, g_ref, w_ref, o_ref, km_ref, h_ref):
    @pl.when(pl.program_id(1) == 0)
    def _():
        _rmsnorm_rows(x_ref, g_ref, h_ref, x_ref.shape[0])
    acc = jnp.dot(h_ref[...], w_ref[...], preferred_element_type=F32)
    o_ref[...] = acc.astype(o_ref.dtype)
    for b in range(acc.shape[0] // MOBA_BLOCK):
        blk = acc[b * MOBA_BLOCK:(b + 1) * MOBA_BLOCK]
        km_ref[0, b:b + 1, :] = jnp.mean(blk, axis=0, keepdims=True)


def _inproj_gate_kernel(x_ref, g_ref, w_ref, o_ref, h_ref):
    @pl.when(pl.program_id(1) == 0)
    def _():
        _rmsnorm_rows(x_ref, g_ref, h_ref, x_ref.shape[0])
    acc = jnp.dot(h_ref[...], w_ref[...], preferred_element_type=F32)
    o_ref[...] = jax.nn.sigmoid(acc).astype(o_ref.dtype)


def _inproj(x, g, w_in, tm=1024, tn=512):
    S, D = x.shape
    nm = S // tm
    x_spec = pl.BlockSpec((tm, D), lambda i, j: (i, 0))
    g_spec = pl.BlockSpec((1, D), lambda i, j: (0, 0))
    scratch = [pltpu.VMEM((tm, D), BF16)]
    sem = ("parallel", "arbitrary")

    def w_spec(col0):
        off = col0 // tn
        return pl.BlockSpec((D, tn), lambda i, j: (0, j + off))

    glu = pl.pallas_call(
        _inproj_glu_kernel,
        grid=(nm, D_CONV // tn),
        in_specs=[x_spec, g_spec, w_spec(0), w_spec(D_CONV)],
        out_specs=pl.BlockSpec((tm, tn), lambda i, j: (i, j)),
        out_shape=jax.ShapeDtypeStruct((S, D_CONV), F32),
        scratch_shapes=scratch,
        compiler_params=_params(sem),
        name="inproj_glu",
    )(x, g, w_in, w_in)

    n_qkv = 3 * D_ATTN
    qkv, km = pl.pallas_call(
        _inproj_qkv_kernel,
        grid=(nm, n_qkv // tn),
        in_specs=[x_spec, g_spec, w_spec(2 * D_CONV)],
        out_specs=[pl.BlockSpec((tm, tn), lambda i, j: (i, j)),
                   pl.BlockSpec((1, tm // MOBA_BLOCK, tn), lambda i, j: (i, 0, j))],
        out_shape=[jax.ShapeDtypeStruct((S, n_qkv), BF16),
                   jax.ShapeDtypeStruct((nm, tm // MOBA_BLOCK, n_qkv), F32)],
        scratch_shapes=scratch,
        compiler_params=_params(sem),
        name="inproj_qkv",
    )(x, g, w_in)

    gates = pl.pallas_call(
        _inproj_gate_kernel,
        grid=(nm, 2 * D_MODEL // tn),
        in_specs=[x_spec, g_spec, w_spec(2 * D_CONV + n_qkv)],
        out_specs=pl.BlockSpec((tm, tn), lambda i, j: (i, j)),
        out_shape=jax.ShapeDtypeStruct((S, 2 * D_MODEL), BF16),
        scratch_shapes=scratch,
        compiler_params=_params(sem),
        name="inproj_gates",
    )(x, g, w_in)
    return glu, qkv, km.reshape(S // MOBA_BLOCK, n_qkv), gates


def _conv_kernel(halo_ref, x_ref, w_ref, b_ref, lg_ref, lb_ref, o_ref, win_ref, y_ref,
                 *, rows, lanes):
    tm, C = x_ref.shape
    i = pl.program_id(0)
    halo = halo_ref[...]
    win_ref[0:CONV_HALO, :] = jnp.where(i > 0, halo, jnp.zeros_like(halo))
    win_ref[CONV_HALO:, :] = x_ref[...]
    first = CONV_HALO - (CONV_K - 1)

    for c0 in range(0, C, lanes):
        def body(rc, carry):
            r = pl.multiple_of(rc * rows, rows)
            acc = jnp.zeros((rows, lanes), F32) + b_ref[:, c0:c0 + lanes]
            win = win_ref[pl.ds(r, rows + CONV_HALO), c0:c0 + lanes]
            for k in range(CONV_K):
                acc = acc + win[first + k:first + k + rows] * w_ref[k:k + 1, c0:c0 + lanes]
            y_ref[pl.ds(r, rows), c0:c0 + lanes] = acc
            return carry
        lax.fori_loop(0, tm // rows, body, 0)

    def ln_body(rc, carry):
        r = pl.multiple_of(rc * NORM_ROWS, NORM_ROWS)
        v = y_ref[pl.ds(r, NORM_ROWS), :]
        mu = jnp.mean(v, axis=-1, keepdims=True)
        var = jnp.mean(jnp.square(v - mu), axis=-1, keepdims=True)
        z = (v - mu) * lax.rsqrt(var + EPS) * lg_ref[...] + lb_ref[...]
        o_ref[pl.ds(r, NORM_ROWS), :] = jax.nn.silu(z).astype(o_ref.dtype)
        return carry
    lax.fori_loop(0, tm // NORM_ROWS, ln_body, 0)


def _conv_branch(glu, w, b, lg, lb, tm=256):
    S, C = glu.shape
    per = tm // CONV_HALO
    return pl.pallas_call(
        functools.partial(_conv_kernel, rows=32, lanes=256),
        grid=(S // tm,),
        in_specs=[pl.BlockSpec((CONV_HALO, C), lambda i: (jnp.maximum(i * per - 1, 0), 0)),
                  pl.BlockSpec((tm, C), lambda i: (i, 0)),
                  pl.BlockSpec((CONV_K, C), lambda i: (0, 0)),
                  pl.BlockSpec((1, C), lambda i: (0, 0)),
                  pl.BlockSpec((1, C), lambda i: (0, 0)),
                  pl.BlockSpec((1, C), lambda i: (0, 0))],
        out_specs=pl.BlockSpec((tm, C), lambda i: (i, 0)),
        out_shape=jax.ShapeDtypeStruct((S, C), BF16),
        scratch_shapes=[pltpu.VMEM((tm + CONV_HALO, C), F32), pltpu.VMEM((tm, C), F32)],
        compiler_params=_params(("parallel",)),
        name="conv_branch",
    )(glu, glu, w, b, lg, lb)


def _bias_kernel(rb_ref, o_ref):
    h = pl.program_id(0)
    delta = pl.program_id(1)
    shape = (MOBA_BLOCK, MOBA_BLOCK)
    d = delta * MOBA_BLOCK + lax.broadcasted_iota(jnp.int32, shape, 0) - lax.broadcasted_iota(jnp.int32, shape, 1)
    n = jnp.maximum(d, 0)
    max_exact = N_BUCKETS // 2
    nf = jnp.maximum(n, 1).astype(F32)
    large = max_exact + (jnp.log(nf / max_exact) / math.log(REL_MAX_DIST / max_exact)
                         * (N_BUCKETS - max_exact)).astype(jnp.int32)
    large = jnp.minimum(large, N_BUCKETS - 1)
    bucket = jnp.where(n < max_exact, n, large)
    val = jnp.zeros(shape, F32)
    for b in range(N_BUCKETS):
        val = jnp.where(bucket == b, rb_ref[b, h], val)
    o_ref[0, 0] = jnp.where(d >= 0, val, NEG)


def _bias_tiles(rel_bias):
    return pl.pallas_call(
        _bias_kernel,
        grid=(N_HEADS, N_BIAS_TILES),
        in_specs=[pl.BlockSpec(memory_space=pltpu.SMEM)],
        out_specs=pl.BlockSpec((1, 1, MOBA_BLOCK, MOBA_BLOCK), lambda h, d: (h, d, 0, 0)),
        out_shape=jax.ShapeDtypeStruct((N_HEADS, N_BIAS_TILES, MOBA_BLOCK, MOBA_BLOCK), F32),
        compiler_params=_params(("parallel", "parallel")),
        name="t5_bias_tiles",
    )(rel_bias)


def _attn_kernel(q_ref, k_ref, v_ref, km_ref, bias_ref, o_ref, m_ref, l_ref, acc_ref):
    i = pl.program_id(1)
    nb = km_ref.shape[0]
    tq = q_ref.shape[0]
    scale = HEAD_DIM ** -0.5
    nt = (((1,), (1,)), ((), ()))
    q = q_ref[...]

    gate = lax.dot_general(q.astype(F32), km_ref[...], nt, precision=lax.Precision.HIGHEST,
                           preferred_element_type=F32)
    col = lax.broadcasted_iota(jnp.int32, (tq, nb), 1)
    valid = col < i
    g = jnp.where(valid, gate, NEG)
    sel = jnp.zeros((tq, nb), jnp.bool_)
    for _ in range(MOBA_TOPK):
        top = jnp.max(g, axis=1, keepdims=True)
        idx = jnp.min(jnp.where(g == top, col, nb), axis=1, keepdims=True)
        hit = col == idx
        sel = jnp.logical_or(sel, hit)
        g = jnp.where(hit, -jnp.inf, g)
    sel_f = jnp.where(jnp.logical_and(sel, valid), 1.0, 0.0).astype(F32)

    r0 = pl.multiple_of(i * MOBA_BLOCK, MOBA_BLOCK)
    s = lax.dot_general(q, k_ref[pl.ds(r0, MOBA_BLOCK), :], nt, preferred_element_type=F32) * scale
    s = s + bias_ref[0, 0]
    m0 = jnp.max(s, axis=1, keepdims=True)
    p = jnp.exp(s - m0)
    m_ref[...] = jnp.broadcast_to(m0, m_ref.shape)
    l_ref[...] = jnp.broadcast_to(jnp.sum(p, axis=1, keepdims=True), l_ref.shape)
    acc_ref[...] = jnp.dot(p.astype(BF16), v_ref[pl.ds(r0, MOBA_BLOCK), :], preferred_element_type=F32)

    def body(j, carry):
        c0 = pl.multiple_of(j * MOBA_BLOCK, MOBA_BLOCK)
        s = lax.dot_general(q, k_ref[pl.ds(c0, MOBA_BLOCK), :], nt, preferred_element_type=F32) * scale
        s = s + bias_ref[0, jnp.minimum(i - j, N_BIAS_TILES - 1)]
        picked = jnp.sum(jnp.where(col == j, sel_f, 0.0), axis=1, keepdims=True)
        s = jnp.where(picked > 0.5, s, NEG)
        m_prev = m_ref[...]
        m_new = jnp.maximum(m_prev, jnp.max(s, axis=1, keepdims=True))
        alpha = jnp.exp(m_prev - m_new)
        p = jnp.exp(s - jnp.concatenate([m_new] * (MOBA_BLOCK // HEAD_DIM), axis=1))
        l_ref[...] = alpha * l_ref[...] + jnp.sum(p, axis=1, keepdims=True)
        acc_ref[...] = alpha * acc_ref[...] + jnp.dot(
            p.astype(BF16), v_ref[pl.ds(c0, MOBA_BLOCK), :], preferred_element_type=F32)
        m_ref[...] = m_new
        return carry
    lax.fori_loop(0, i, body, 0)
    o_ref[...] = (acc_ref[...] / l_ref[...]).astype(o_ref.dtype)


def _attention(qkv, km, bias):
    S = qkv.shape[0]
    nb = S // MOBA_BLOCK
    tq = MOBA_BLOCK
    return pl.pallas_call(
        _attn_kernel,
        grid=(N_HEADS, S // tq),
        in_specs=[pl.BlockSpec((tq, HEAD_DIM), lambda h, i: (i, h)),
                  pl.BlockSpec((S, HEAD_DIM), lambda h, i: (0, N_HEADS + h)),
                  pl.BlockSpec((S, HEAD_DIM), lambda h, i: (0, 2 * N_HEADS + h)),
                  pl.BlockSpec((nb, HEAD_DIM), lambda h, i: (0, N_HEADS + h)),
                  pl.BlockSpec((1, N_BIAS_TILES, MOBA_BLOCK, MOBA_BLOCK), lambda h, i: (h, 0, 0, 0))],
        out_specs=pl.BlockSpec((tq, HEAD_DIM), lambda h, i: (i, h)),
        out_shape=jax.ShapeDtypeStruct((S, D_ATTN), BF16),
        scratch_shapes=[pltpu.VMEM((tq, HEAD_DIM), F32)] * 3,
        compiler_params=_params(("parallel", "parallel")),
        name="moba_attention",
    )(qkv, qkv, qkv, km, bias)


def _mix_kernel(cs_ref, at_ref, ga_ref, gb_ref, x_ref, wc_ref, wa_ref, wo_ref, o_ref):
    yc = jnp.dot(cs_ref[...], wc_ref[...], preferred_element_type=F32)
    ya = jnp.dot(at_ref[...], wa_ref[...], preferred_element_type=F32)
    merged = ga_ref[...].astype(F32) * yc + gb_ref[...].astype(F32) * ya
    o_ref[...] = x_ref[...] + jnp.dot(merged.astype(BF16), wo_ref[...], preferred_element_type=F32)


def _mix(cs, attn, gates, x, wc, wa, wo, tm=256):
    S, D = x.shape
    full = lambda shape: pl.BlockSpec(shape, lambda i: (0, 0))
    return pl.pallas_call(
        _mix_kernel,
        grid=(S // tm,),
        in_specs=[pl.BlockSpec((tm, D_CONV), lambda i: (i, 0)),
                  pl.BlockSpec((tm, D_ATTN), lambda i: (i, 0)),
                  pl.BlockSpec((tm, D), lambda i: (i, 0)),
                  pl.BlockSpec((tm, D), lambda i: (i, 1)),
                  pl.BlockSpec((tm, D), lambda i: (i, 0)),
                  full((D_CONV, D)), full((D_ATTN, D)), full((D, D))],
        out_specs=pl.BlockSpec((tm, D), lambda i: (i, 0)),
        out_shape=jax.ShapeDtypeStruct((S, D), F32),
        compiler_params=_params(("parallel",)),
        name="mix_out_proj",
    )(cs, attn, gates, gates, x, wc, wa, wo)


def _ffn_kernel(xh_ref, x_ref, g_ref, wa_ref, wb_ref, dwa_ref, dwb_ref, ba_ref, bb_ref, wd_ref,
                fg_ref, o_ref, h_ref, acc_ref, *, final_norm):
    i = pl.program_id(0)
    f = pl.program_id(1)
    tm = x_ref.shape[0]

    @pl.when(f == 0)
    def _():
        xh = xh_ref[...]
        ms = jnp.mean(xh * xh, axis=-1, keepdims=True)
        hh = xh * lax.rsqrt(ms + EPS) * g_ref[...]
        h_ref[0:HALO, :] = jnp.where(i > 0, hh, jnp.zeros_like(hh)).astype(BF16)
        _rmsnorm_rows(x_ref, g_ref, h_ref, tm, dst_off=HALO)

    h = h_ref[...]

    def up_conv(w_ref, dw_ref, b_ref):
        u = jnp.dot(h, w_ref[...], preferred_element_type=F32)
        y = (u * dw_ref[2:3, :] + pltpu.roll(u, 1, 0) * dw_ref[1:2, :]
             + pltpu.roll(u, 2, 0) * dw_ref[0:1, :] + b_ref[...])
        return y[HALO:]

    ua = up_conv(wa_ref, dwa_ref, ba_ref)
    ub = up_conv(wb_ref, dwb_ref, bb_ref)
    gated = (jax.nn.silu(ub) * ua).astype(BF16)
    contrib = jnp.dot(gated, wd_ref[...], preferred_element_type=F32)

    @pl.when(f == 0)
    def _():
        acc_ref[...] = contrib

    @pl.when(f > 0)
    def _():
        acc_ref[...] += contrib

    @pl.when(f == pl.num_programs(1) - 1)
    def _():
        if final_norm:
            def body(c, carry):
                r = pl.multiple_of(c * NORM_ROWS, NORM_ROWS)
                xv = x_ref[pl.ds(r, NORM_ROWS), :] + acc_ref[pl.ds(r, NORM_ROWS), :]
                ms = jnp.mean(xv * xv, axis=-1, keepdims=True)
                o_ref[pl.ds(r, NORM_ROWS), :] = xv * lax.rsqrt(ms + EPS) * fg_ref[...]
                return carry
            lax.fori_loop(0, tm // NORM_ROWS, body, 0)
        else:
            o_ref[...] = x_ref[...] + acc_ref[...]


def _ffn(x, g, w_up, dw, dwb, w_down, final_g, final_norm, tm=512, tf=512):
    S, D = x.shape
    nf = D_FF // tf
    per = tm // HALO
    row = lambda shape: pl.BlockSpec(shape, lambda i, f: (0, 0))
    return pl.pallas_call(
        functools.partial(_ffn_kernel, final_norm=final_norm),
        grid=(S // tm, nf),
        in_specs=[pl.BlockSpec((HALO, D), lambda i, f: (jnp.maximum(i * per - 1, 0), 0)),
                  pl.BlockSpec((tm, D), lambda i, f: (i, 0)),
                  row((1, D)),
                  pl.BlockSpec((D, tf), lambda i, f: (0, f)),
                  pl.BlockSpec((D, tf), lambda i, f: (0, f + nf)),
                  pl.BlockSpec((FFN_CONV_K, tf), lambda i, f: (0, f)),
                  pl.BlockSpec((FFN_CONV_K, tf), lambda i, f: (0, f + nf)),
                  pl.BlockSpec((1, tf), lambda i, f: (0, f)),
                  pl.BlockSpec((1, tf), lambda i, f: (0, f + nf)),
                  pl.BlockSpec((tf, D), lambda i, f: (f, 0)),
                  row((1, D))],
        out_specs=pl.BlockSpec((tm, D), lambda i, f: (i, 0)),
        out_shape=jax.ShapeDtypeStruct((S, D), F32),
        scratch_shapes=[pltpu.VMEM((HALO + tm, D), BF16), pltpu.VMEM((tm, D), F32)],
        compiler_params=_params(("parallel", "arbitrary")),
        name="conv_ffn",
    )(x, x, g, w_up, w_up, dw, dw, dwb, dwb, w_down, final_g)


def kernel(x, norm1_g, w_in, conv_dw, conv_dw_b, conv_ln_g, conv_ln_b, w_conv_out, rel_bias,
           w_attn_out, w_out, norm2_g, w_up, ffn_dw, ffn_dw_b, w_down, final_g):
    B, S, D = x.shape
    depth = w_in.shape[0]
    bias = _bias_tiles(rel_bias)
    row = lambda v: v.reshape(1, -1)
    outs = []
    for b in range(B):
        xb = x[b]
        for l in range(depth):
            glu, qkv, km, gates = _inproj(xb, row(norm1_g[l]), w_in[l].astype(BF16))
            cs = _conv_branch(glu, conv_dw[l], row(conv_dw_b[l]), row(conv_ln_g[l]), row(conv_ln_b[l]))
            attn = _attention(qkv, km, bias)
            xb = _mix(cs, attn, gates, xb, w_conv_out[l].astype(BF16), w_attn_out[l].astype(BF16),
                      w_out[l].astype(BF16))
            xb = _ffn(xb, row(norm2_g[l]), w_up[l].astype(BF16), ffn_dw[l], row(ffn_dw_b[l]),
                      w_down[l].astype(BF16), row(final_g), final_norm=(l == depth - 1))
        outs.append(xb)
    return jnp.stack(outs, axis=0)
```

```python
import functools
import math

import jax
import jax.numpy as jnp
from jax import lax
from jax.experimental import pallas as pl
from jax.experimental.pallas import tpu as pltpu

F32 = jnp.float32
BF16 = jnp.bfloat16

D_MODEL = 2048
D_CONV = 1024
CONV_K = 31
N_HEADS = 8
HEAD_DIM = 128
D_ATTN = N_HEADS * HEAD_DIM
MOBA_BLOCK = 256
MOBA_TOPK = 3
N_BUCKETS = 32
REL_MAX_DIST = 2048
D_FF = 5632
FFN_CONV_K = 3
EPS = 1e-6
NEG = -1e30

N_BIAS_TILES = 8
KV_GROUP = 4
GROUP_KEYS = KV_GROUP * MOBA_BLOCK

V7X_VMEM_BYTES = 64 * 1024 * 1024
VMEM_LIMIT = V7X_VMEM_BYTES - 8 * 1024 * 1024

NORM_ROWS = 16
HALO = 8
CONV_HALO = 32

NT_DIMS = (((1,), (1,)), ((), ()))


def _params(sem):
    return pltpu.CompilerParams(dimension_semantics=sem, vmem_limit_bytes=VMEM_LIMIT)


def _rmsnorm_rows(x_ref, g_ref, dst_ref, n_rows, dst_off=0, out_dtype=BF16):
    def body(c, carry):
        r = pl.multiple_of(c * NORM_ROWS, NORM_ROWS)
        xv = x_ref[pl.ds(r, NORM_ROWS), :]
        ms = jnp.mean(xv * xv, axis=-1, keepdims=True)
        y = xv * lax.rsqrt(ms + EPS) * g_ref[...]
        dst_ref[pl.ds(dst_off + r, NORM_ROWS), :] = y.astype(out_dtype)
        return carry
    lax.fori_loop(0, n_rows // NORM_ROWS, body, 0)


def _inproj_glu_kernel(x_ref, g_ref, wa_ref, wb_ref, o_ref, h_ref):
    @pl.when(pl.program_id(1) == 0)
    def _():
        _rmsnorm_rows(x_ref, g_ref, h_ref, x_ref.shape[0])
    h = h_ref[...]
    a = jnp.dot(h, wa_ref[...], preferred_element_type=F32)
    b = jnp.dot(h, wb_ref[...], preferred_element_type=F32)
    o_ref[...] = a * jax.nn.sigmoid(b)


def _inproj_k_kernel(x_ref, g_ref, w_ref, o_ref, km_ref, h_ref):
    @pl.when(pl.program_id(1) == 0)
    def _():
        _rmsnorm_rows(x_ref, g_ref, h_ref, x_ref.shape[0])
    acc = jnp.dot(h_ref[...], w_ref[...], preferred_element_type=F32)
    o_ref[...] = acc.astype(o_ref.dtype)
    for b in range(acc.shape[0] // MOBA_BLOCK):
        blk = acc[b * MOBA_BLOCK:(b + 1) * MOBA_BLOCK]
        km_ref[0, b:b + 1, :] = jnp.mean(blk, axis=0, keepdims=True)


def _inproj_t_kernel(x_ref, g_ref, wt_ref, o_ref, h_ref):
    @pl.when(pl.program_id(1) == 0)
    def _():
        _rmsnorm_rows(x_ref, g_ref, h_ref, x_ref.shape[0])
    acc = lax.dot_general(wt_ref[...], h_ref[...], NT_DIMS, preferred_element_type=F32)
    o_ref[0] = acc.astype(o_ref.dtype)


def _inproj_gate_kernel(x_ref, g_ref, w_ref, o_ref, h_ref):
    @pl.when(pl.program_id(1) == 0)
    def _():
        _rmsnorm_rows(x_ref, g_ref, h_ref, x_ref.shape[0])
    acc = jnp.dot(h_ref[...], w_ref[...], preferred_element_type=F32)
    o_ref[...] = jax.nn.sigmoid(acc).astype(o_ref.dtype)


def _inproj(x, g, w_in, wt_qv, tn=512):
    S, D = x.shape
    tm = GROUP_KEYS
    nm = S // tm
    x_spec = pl.BlockSpec((tm, D), lambda i, j: (i, 0))
    g_spec = pl.BlockSpec((1, D), lambda i, j: (0, 0))
    scratch = [pltpu.VMEM((tm, D), BF16)]
    sem = ("parallel", "arbitrary")

    def w_spec(col0):
        off = col0 // tn
        return pl.BlockSpec((D, tn), lambda i, j: (0, j + off))

    glu = pl.pallas_call(
        _inproj_glu_kernel,
        grid=(nm, D_CONV // tn),
        in_specs=[x_spec, g_spec, w_spec(0), w_spec(D_CONV)],
        out_specs=pl.BlockSpec((tm, tn), lambda i, j: (i, j)),
        out_shape=jax.ShapeDtypeStruct((S, D_CONV), F32),
        scratch_shapes=scratch,
        compiler_params=_params(sem),
        name="inproj_glu",
    )(x, g, w_in, w_in)

    k, km = pl.pallas_call(
        _inproj_k_kernel,
        grid=(nm, D_ATTN // tn),
        in_specs=[x_spec, g_spec, w_spec(2 * D_CONV + D_ATTN)],
        out_specs=[pl.BlockSpec((tm, tn), lambda i, j: (i, j)),
                   pl.BlockSpec((1, tm // MOBA_BLOCK, tn), lambda i, j: (i, 0, j))],
        out_shape=[jax.ShapeDtypeStruct((S, D_ATTN), BF16),
                   jax.ShapeDtypeStruct((nm, tm // MOBA_BLOCK, D_ATTN), F32)],
        scratch_shapes=scratch,
        compiler_params=_params(sem),
        name="inproj_k",
    )(x, g, w_in)

    qv_t = pl.pallas_call(
        _inproj_t_kernel,
        grid=(nm, 2 * D_ATTN // tn),
        in_specs=[x_spec, g_spec, pl.BlockSpec((tn, D), lambda i, j: (j, 0))],
        out_specs=pl.BlockSpec((1, tn, tm), lambda i, j: (i, j, 0)),
        out_shape=jax.ShapeDtypeStruct((nm, 2 * D_ATTN, tm), BF16),
        scratch_shapes=scratch,
        compiler_params=_params(sem),
        name="inproj_qv_t",
    )(x, g, wt_qv)

    gates = pl.pallas_call(
        _inproj_gate_kernel,
        grid=(nm, 2 * D_MODEL // tn),
        in_specs=[x_spec, g_spec, w_spec(2 * D_CONV + 3 * D_ATTN)],
        out_specs=pl.BlockSpec((tm, tn), lambda i, j: (i, j)),
        out_shape=jax.ShapeDtypeStruct((S, 2 * D_MODEL), BF16),
        scratch_shapes=scratch,
        compiler_params=_params(sem),
        name="inproj_gates",
    )(x, g, w_in)
    return glu, k, km.reshape(S // MOBA_BLOCK, D_ATTN), qv_t, gates


def _conv_kernel(halo_ref, x_ref, w_ref, b_ref, lg_ref, lb_ref, o_ref, win_ref, y_ref,
                 *, rows, lanes):
    tm, C = x_ref.shape
    i = pl.program_id(0)
    halo = halo_ref[...]
    win_ref[0:CONV_HALO, :] = jnp.where(i > 0, halo, jnp.zeros_like(halo))
    win_ref[CONV_HALO:, :] = x_ref[...]
    first = CONV_HALO - (CONV_K - 1)

    for c0 in range(0, C, lanes):
        def body(rc, carry):
            r = pl.multiple_of(rc * rows, rows)
            acc = jnp.zeros((rows, lanes), F32) + b_ref[:, c0:c0 + lanes]
            win = win_ref[pl.ds(r, rows + CONV_HALO), c0:c0 + lanes]
            for k in range(CONV_K):
                acc = acc + win[first + k:first + k + rows] * w_ref[k:k + 1, c0:c0 + lanes]
            y_ref[pl.ds(r, rows), c0:c0 + lanes] = acc
            return carry
        lax.fori_loop(0, tm // rows, body, 0)

    def ln_body(rc, carry):
        r = pl.multiple_of(rc * NORM_ROWS, NORM_ROWS)
        v = y_ref[pl.ds(r, NORM_ROWS), :]
        mu = jnp.mean(v, axis=-1, keepdims=True)
        var = jnp.mean(jnp.square(v - mu), axis=-1, keepdims=True)
        z = (v - mu) * lax.rsqrt(var + EPS) * lg_ref[...] + lb_ref[...]
        o_ref[pl.ds(r, NORM_ROWS), :] = jax.nn.silu(z).astype(o_ref.dtype)
        return carry
    lax.fori_loop(0, tm // NORM_ROWS, ln_body, 0)


def _conv_branch(glu, w, b, lg, lb, tm=256):
    S, C = glu.shape
    per = tm // CONV_HALO
    return pl.pallas_call(
        functools.partial(_conv_kernel, rows=32, lanes=256),
        grid=(S // tm,),
        in_specs=[pl.BlockSpec((CONV_HALO, C), lambda i: (jnp.maximum(i * per - 1, 0), 0)),
                  pl.BlockSpec((tm, C), lambda i: (i, 0)),
                  pl.BlockSpec((CONV_K, C), lambda i: (0, 0)),
                  pl.BlockSpec((1, C), lambda i: (0, 0)),
                  pl.BlockSpec((1, C), lambda i: (0, 0)),
                  pl.BlockSpec((1, C), lambda i: (0, 0))],
        out_specs=pl.BlockSpec((tm, C), lambda i: (i, 0)),
        out_shape=jax.ShapeDtypeStruct((S, C), BF16),
        scratch_shapes=[pltpu.VMEM((tm + CONV_HALO, C), F32), pltpu.VMEM((tm, C), F32)],
        compiler_params=_params(("parallel",)),
        name="conv_branch",
    )(glu, glu, w, b, lg, lb)


def _bias_kernel(rb_ref, o_ref):
    h = pl.program_id(0)
    delta = pl.program_id(1)
    shape = (MOBA_BLOCK, MOBA_BLOCK)
    d = delta * MOBA_BLOCK + lax.broadcasted_iota(jnp.int32, shape, 1) - lax.broadcasted_iota(jnp.int32, shape, 0)
    n = jnp.maximum(d, 0)
    max_exact = N_BUCKETS // 2
    nf = jnp.maximum(n, 1).astype(F32)
    large = max_exact + (jnp.log(nf / max_exact) / math.log(REL_MAX_DIST / max_exact)
                         * (N_BUCKETS - max_exact)).astype(jnp.int32)
    large = jnp.minimum(large, N_BUCKETS - 1)
    bucket = jnp.where(n < max_exact, n, large)
    val = jnp.zeros(shape, F32)
    for b in range(N_BUCKETS):
        val = jnp.where(bucket == b, rb_ref[b, h], val)
    val = jnp.where(delta == N_BIAS_TILES - 1, 0.0, val)
    o_ref[0, 0] = jnp.where(d >= 0, val, NEG)


def _bias_tiles(rel_bias):
    return pl.pallas_call(
        _bias_kernel,
        grid=(N_HEADS, N_BIAS_TILES),
        in_specs=[pl.BlockSpec(memory_space=pltpu.SMEM)],
        out_specs=pl.BlockSpec((1, 1, MOBA_BLOCK, MOBA_BLOCK), lambda h, d: (h, d, 0, 0)),
        out_shape=jax.ShapeDtypeStruct((N_HEADS, N_BIAS_TILES, MOBA_BLOCK, MOBA_BLOCK), F32),
        compiler_params=_params(("parallel", "parallel")),
        name="t5_bias_tiles",
    )(rel_bias)


def _attn_kernel(rb_ref, qt_ref, k_ref, vt_ref, km_ref, ind_ref, bias_ref, o_ref, acc_ref):
    h = pl.program_id(0)
    i = pl.program_id(1)
    nb = km_ref.shape[0]
    tq = qt_ref.shape[2]
    scale = HEAD_DIM ** -0.5
    qt = qt_ref[0]
    qf = qt.astype(F32)

    gate = jnp.dot(km_ref[...], qf, precision=lax.Precision.HIGHEST, preferred_element_type=F32)
    row = lax.broadcasted_iota(jnp.int32, (nb, tq), 0)
    valid = row < i
    g = jnp.where(valid, gate, NEG)
    sel = jnp.zeros((nb, tq), jnp.bool_)
    for _ in range(MOBA_TOPK):
        top = jnp.max(g, axis=0, keepdims=True)
        idx = jnp.min(jnp.where(g == top, row, nb), axis=0, keepdims=True)
        hit = row == idx
        sel = jnp.logical_or(sel, hit)
        g = jnp.where(hit, -jnp.inf, g)
    picked = jnp.logical_and(sel, valid)
    far = jnp.logical_and(picked, row <= i - (N_BIAS_TILES - 1))
    far_bias = rb_ref[N_BUCKETS - 1, h]
    pen = jnp.where(row == i, 0.0, jnp.where(picked, jnp.where(far, far_bias, 0.0), NEG))
    pen_hi = pen.astype(BF16)
    pen_lo = jnp.where(far, pen - pen_hi.astype(F32), 0.0).astype(BF16)
    q_aug = jnp.concatenate(
        [(qf * scale).astype(BF16), pen_hi, pen_lo, jnp.zeros((HEAD_DIM - 2 * nb, tq), BF16)], axis=0)

    acc_ref[...] = jnp.zeros_like(acc_ref)
    n_groups = i // KV_GROUP + 1

    def body(t, carry):
        m_prev, l_prev = carry
        grp = n_groups - 1 - t
        r0 = pl.multiple_of(grp * GROUP_KEYS, GROUP_KEYS)
        k_aug = jnp.concatenate([k_ref[pl.ds(r0, GROUP_KEYS), :], ind_ref[pl.ds(r0, GROUP_KEYS), :]], axis=1)
        s = jnp.dot(k_aug, q_aug, preferred_element_type=F32)
        parts = []
        for u in range(KV_GROUP):
            delta = jnp.clip(i - (grp * KV_GROUP + u), 0, N_BIAS_TILES - 1)
            parts.append(s[u * MOBA_BLOCK:(u + 1) * MOBA_BLOCK] + bias_ref[0, delta])
        s = jnp.concatenate(parts, axis=0)
        m_new = jnp.maximum(m_prev, jnp.max(s, axis=0, keepdims=True))
        alpha = jnp.exp(m_prev - m_new)
        p = jnp.exp(s - m_new)
        l_new = alpha * l_prev + jnp.sum(p, axis=0, keepdims=True)
        acc_ref[...] = alpha * acc_ref[...] + jnp.dot(vt_ref[grp], p.astype(BF16), preferred_element_type=F32)
        return m_new, l_new

    m0 = jnp.full((1, tq), NEG, F32)
    l0 = jnp.zeros((1, tq), F32)
    _, l_fin = lax.fori_loop(0, n_groups, body, (m0, l0))
    o_ref[...] = (acc_ref[...] / l_fin).T.astype(o_ref.dtype)


def _attention(k, km, qv_t, bias, rel_bias):
    S = k.shape[0]
    nb = S // MOBA_BLOCK
    tq = MOBA_BLOCK
    per = GROUP_KEYS // tq
    heads_rows = D_ATTN // HEAD_DIM
    key_blk = jnp.arange(S, dtype=jnp.int32)[:, None] // MOBA_BLOCK
    lane = jnp.arange(HEAD_DIM, dtype=jnp.int32)[None, :]
    ind = jnp.logical_and(lane < 2 * nb, lane % nb == key_blk).astype(BF16)
    return pl.pallas_call(
        _attn_kernel,
        grid=(N_HEADS, S // tq),
        in_specs=[pl.BlockSpec(memory_space=pltpu.SMEM),
                  pl.BlockSpec((1, HEAD_DIM, tq), lambda h, i: (i // per, h, i % per)),
                  pl.BlockSpec((S, HEAD_DIM), lambda h, i: (0, h)),
                  pl.BlockSpec((S // GROUP_KEYS, HEAD_DIM, GROUP_KEYS), lambda h, i: (0, heads_rows + h, 0)),
                  pl.BlockSpec((nb, HEAD_DIM), lambda h, i: (0, h)),
                  pl.BlockSpec((S, HEAD_DIM), lambda h, i: (0, 0)),
                  pl.BlockSpec((1, N_BIAS_TILES, MOBA_BLOCK, MOBA_BLOCK), lambda h, i: (h, 0, 0, 0))],
        out_specs=pl.BlockSpec((tq, HEAD_DIM), lambda h, i: (i, h)),
        out_shape=jax.ShapeDtypeStruct((S, D_ATTN), BF16),
        scratch_shapes=[pltpu.VMEM((HEAD_DIM, tq), F32)],
        compiler_params=_params(("parallel", "parallel")),
        name="moba_attention",
    )(rel_bias, qv_t, k, qv_t, km, ind, bias)


def _mix_kernel(cs_ref, at_ref, ga_ref, gb_ref, x_ref, wc_ref, wa_ref, wo_ref, o_ref):
    yc = jnp.dot(cs_ref[...], wc_ref[...], preferred_element_type=F32)
    ya = jnp.dot(at_ref[...], wa_ref[...], preferred_element_type=F32)
    merged = ga_ref[...].astype(F32) * yc + gb_ref[...].astype(F32) * ya
    o_ref[...] = x_ref[...] + jnp.dot(merged.astype(BF16), wo_ref[...], preferred_element_type=F32)


def _mix(cs, attn, gates, x, wc, wa, wo, tm=256):
    S, D = x.shape
    full = lambda shape: pl.BlockSpec(shape, lambda i: (0, 0))
    return pl.pallas_call(
        _mix_kernel,
        grid=(S // tm,),
        in_specs=[pl.BlockSpec((tm, D_CONV), lambda i: (i, 0)),
                  pl.BlockSpec((tm, D_ATTN), lambda i: (i, 0)),
                  pl.BlockSpec((tm, D), lambda i: (i, 0)),
                  pl.BlockSpec((tm, D), lambda i: (i, 1)),
                  pl.BlockSpec((tm, D), lambda i: (i, 0)),
                  full((D_CONV, D)), full((D_ATTN, D)), full((D, D))],
        out_specs=pl.BlockSpec((tm, D), lambda i: (i, 0)),
        out_shape=jax.ShapeDtypeStruct((S, D), F32),
        compiler_params=_params(("parallel",)),
        name="mix_out_proj",
    )(cs, attn, gates, gates, x, wc, wa, wo)


def _ffn_kernel(xh_ref, x_ref, g_ref, wa_ref, wb_ref, dwa_ref, dwb_ref, ba_ref, bb_ref, wd_ref,
                fg_ref, o_ref, h_ref, acc_ref, *, final_norm):
    i = pl.program_id(0)
    f = pl.program_id(1)
    tm = x_ref.shape[0]

    @pl.when(f == 0)
    def _():
        xh = xh_ref[...]
        ms = jnp.mean(xh * xh, axis=-1, keepdims=True)
        hh = xh * lax.rsqrt(ms + EPS) * g_ref[...]
        h_ref[0:HALO, :] = jnp.where(i > 0, hh, jnp.zeros_like(hh)).astype(BF16)
        _rmsnorm_rows(x_ref, g_ref, h_ref, tm, dst_off=HALO)

    h = h_ref[...]

    def up_conv(w_ref, dw_ref, b_ref):
        u = jnp.dot(h, w_ref[...], preferred_element_type=F32)
        y = (u * dw_ref[2:3, :] + pltpu.roll(u, 1, 0) * dw_ref[1:2, :]
             + pltpu.roll(u, 2, 0) * dw_ref[0:1, :] + b_ref[...])
        return y[HALO:]

    ua = up_conv(wa_ref, dwa_ref, ba_ref)
    ub = up_conv(wb_ref, dwb_ref, bb_ref)
    gated = (jax.nn.silu(ub) * ua).astype(BF16)
    contrib = jnp.dot(gated, wd_ref[...], preferred_element_type=F32)

    @pl.when(f == 0)
    def _():
        acc_ref[...] = contrib

    @pl.when(f > 0)
    def _():
        acc_ref[...] += contrib

    @pl.when(f == pl.num_programs(1) - 1)
    def _():
        if final_norm:
            def body(c, carry):
                r = pl.multiple_of(c * NORM_ROWS, NORM_ROWS)
                xv = x_ref[pl.ds(r, NORM_ROWS), :] + acc_ref[pl.ds(r, NORM_ROWS), :]
                ms = jnp.mean(xv * xv, axis=-1, keepdims=True)
                o_ref[pl.ds(r, NORM_ROWS), :] = xv * lax.rsqrt(ms + EPS) * fg_ref[...]
                return carry
            lax.fori_loop(0, tm // NORM_ROWS, body, 0)
        else:
            o_ref[...] = x_ref[...] + acc_ref[...]


def _ffn(x, g, w_up, dw, dwb, w_down, final_g, final_norm, tm=512, tf=512):
    S, D = x.shape
    nf = D_FF // tf
    per = tm // HALO
    row = lambda shape: pl.BlockSpec(shape, lambda i, f: (0, 0))
    return pl.pallas_call(
        functools.partial(_ffn_kernel, final_norm=final_norm),
        grid=(S // tm, nf),
        in_specs=[pl.BlockSpec((HALO, D), lambda i, f: (jnp.maximum(i * per - 1, 0), 0)),
                  pl.BlockSpec((tm, D), lambda i, f: (i, 0)),
                  row((1, D)),
                  pl.BlockSpec((D, tf), lambda i, f: (0, f)),
                  pl.BlockSpec((D, tf), lambda i, f: (0, f + nf)),
                  pl.BlockSpec((FFN_CONV_K, tf), lambda i, f: (0, f)),
                  pl.BlockSpec((FFN_CONV_K, tf), lambda i, f: (0, f + nf)),
                  pl.BlockSpec((1, tf), lambda i, f: (0, f)),
                  pl.BlockSpec((1, tf), lambda i, f: (0, f + nf)),
                  pl.BlockSpec((tf, D), lambda i, f: (f, 0)),
                  row((1, D))],
        out_specs=pl.BlockSpec((tm, D), lambda i, f: (i, 0)),
        out_shape=jax.ShapeDtypeStruct((S, D), F32),
        scratch_shapes=[pltpu.VMEM((HALO + tm, D), BF16), pltpu.VMEM((tm, D), F32)],
        compiler_params=_params(("parallel", "arbitrary")),
        name="conv_ffn",
    )(x, x, g, w_up, w_up, dw, dw, dwb, dwb, w_down, final_g)


def kernel(x, norm1_g, w_in, conv_dw, conv_dw_b, conv_ln_g, conv_ln_b, w_conv_out, rel_bias,
           w_attn_out, w_out, norm2_g, w_up, ffn_dw, ffn_dw_b, w_down, final_g):
    B, S, D = x.shape
    assert D == D_MODEL and S % GROUP_KEYS == 0
    depth = w_in.shape[0]
    bias = _bias_tiles(rel_bias)
    row = lambda v: v.reshape(1, -1)
    q0 = 2 * D_CONV
    v0 = q0 + 2 * D_ATTN
    outs = []
    for b in range(B):
        xb = x[b]
        for l in range(depth):
            w_l = w_in[l]
            wt_qv = jnp.concatenate([w_l[:, q0:q0 + D_ATTN], w_l[:, v0:v0 + D_ATTN]], axis=1).T.astype(BF16)
            glu, k, km, qv_t, gates = _inproj(xb, row(norm1_g[l]), w_l.astype(BF16), wt_qv)
            cs = _conv_branch(glu, conv_dw[l], row(conv_dw_b[l]), row(conv_ln_g[l]), row(conv_ln_b[l]))
            attn = _attention(k, km, qv_t, bias, rel_bias)
            xb = _mix(cs, attn, gates, xb, w_conv_out[l].astype(BF16), w_attn_out[l].astype(BF16),
                      w_out[l].astype(BF16))
            xb = _ffn(xb, row(norm2_g[l]), w_up[l].astype(BF16), ffn_dw[l], row(ffn_dw_b[l]),
                      w_down[l].astype(BF16), row(final_g), final_norm=(l == depth - 1))
        outs.append(xb)
    return jnp.stack(outs, axis=0)
```

```python
import functools
import math

import jax
import jax.numpy as jnp
from jax import lax
from jax.experimental import pallas as pl
from jax.experimental.pallas import tpu as pltpu

F32 = jnp.float32
BF16 = jnp.bfloat16

D_MODEL = 2048
D_CONV = 1024
CONV_K = 31
N_HEADS = 8
HEAD_DIM = 128
D_ATTN = N_HEADS * HEAD_DIM
MOBA_BLOCK = 256
MOBA_TOPK = 3
N_BUCKETS = 32
REL_MAX_DIST = 2048
D_FF = 5632
FFN_CONV_K = 3
EPS = 1e-6
NEG = -1e30
LOG2E = math.log2(math.e)

N_BIAS_TILES = 8
KV_GROUP = 4
GROUP_KEYS = KV_GROUP * MOBA_BLOCK
SUM_ROWS = 16

V7X_VMEM_BYTES = 64 * 1024 * 1024
VMEM_LIMIT = V7X_VMEM_BYTES - 8 * 1024 * 1024

NORM_ROWS = 16
HALO = 8
CONV_HALO = 32

NT_DIMS = (((1,), (1,)), ((), ()))


def _params(sem):
    return pltpu.CompilerParams(dimension_semantics=sem, vmem_limit_bytes=VMEM_LIMIT)


def _rmsnorm_rows(x_ref, g_ref, dst_ref, n_rows, dst_off=0, out_dtype=BF16):
    def body(c, carry):
        r = pl.multiple_of(c * NORM_ROWS, NORM_ROWS)
        xv = x_ref[pl.ds(r, NORM_ROWS), :]
        ms = jnp.mean(xv * xv, axis=-1, keepdims=True)
        y = xv * lax.rsqrt(ms + EPS) * g_ref[...]
        dst_ref[pl.ds(dst_off + r, NORM_ROWS), :] = y.astype(out_dtype)
        return carry
    lax.fori_loop(0, n_rows // NORM_ROWS, body, 0)


def _inproj_glu_kernel(x_ref, g_ref, wa_ref, wb_ref, o_ref, h_ref):
    @pl.when(pl.program_id(1) == 0)
    def _():
        _rmsnorm_rows(x_ref, g_ref, h_ref, x_ref.shape[0])
    h = h_ref[...]
    a = jnp.dot(h, wa_ref[...], preferred_element_type=F32)
    b = jnp.dot(h, wb_ref[...], preferred_element_type=F32)
    o_ref[...] = a * jax.nn.sigmoid(b)


def _inproj_k_kernel(x_ref, g_ref, w_ref, o_ref, km_ref, h_ref):
    @pl.when(pl.program_id(1) == 0)
    def _():
        _rmsnorm_rows(x_ref, g_ref, h_ref, x_ref.shape[0])
    acc = jnp.dot(h_ref[...], w_ref[...], preferred_element_type=F32)
    o_ref[...] = acc.astype(o_ref.dtype)
    for b in range(acc.shape[0] // MOBA_BLOCK):
        blk = acc[b * MOBA_BLOCK:(b + 1) * MOBA_BLOCK]
        km_ref[0, b:b + 1, :] = jnp.mean(blk, axis=0, keepdims=True)


def _inproj_t_kernel(x_ref, g_ref, wt_ref, o_ref, h_ref):
    @pl.when(pl.program_id(1) == 0)
    def _():
        _rmsnorm_rows(x_ref, g_ref, h_ref, x_ref.shape[0])
    acc = lax.dot_general(wt_ref[...], h_ref[...], NT_DIMS, preferred_element_type=F32)
    o_ref[0] = acc.astype(o_ref.dtype)


def _inproj_gate_kernel(x_ref, g_ref, w_ref, o_ref, h_ref):
    @pl.when(pl.program_id(1) == 0)
    def _():
        _rmsnorm_rows(x_ref, g_ref, h_ref, x_ref.shape[0])
    acc = jnp.dot(h_ref[...], w_ref[...], preferred_element_type=F32)
    o_ref[...] = jax.nn.sigmoid(acc).astype(o_ref.dtype)


def _inproj(x, g, w_in, wt_qv, tn=512):
    S, D = x.shape
    tm = GROUP_KEYS
    nm = S // tm
    x_spec = pl.BlockSpec((tm, D), lambda i, j: (i, 0))
    g_spec = pl.BlockSpec((1, D), lambda i, j: (0, 0))
    scratch = [pltpu.VMEM((tm, D), BF16)]
    sem = ("parallel", "arbitrary")

    def w_spec(col0):
        off = col0 // tn
        return pl.BlockSpec((D, tn), lambda i, j: (0, j + off))

    glu = pl.pallas_call(
        _inproj_glu_kernel,
        grid=(nm, D_CONV // tn),
        in_specs=[x_spec, g_spec, w_spec(0), w_spec(D_CONV)],
        out_specs=pl.BlockSpec((tm, tn), lambda i, j: (i, j)),
        out_shape=jax.ShapeDtypeStruct((S, D_CONV), F32),
        scratch_shapes=scratch,
        compiler_params=_params(sem),
        name="inproj_glu",
    )(x, g, w_in, w_in)

    k, km = pl.pallas_call(
        _inproj_k_kernel,
        grid=(nm, D_ATTN // tn),
        in_specs=[x_spec, g_spec, w_spec(2 * D_CONV + D_ATTN)],
        out_specs=[pl.BlockSpec((tm, tn), lambda i, j: (i, j)),
                   pl.BlockSpec((1, tm // MOBA_BLOCK, tn), lambda i, j: (i, 0, j))],
        out_shape=[jax.ShapeDtypeStruct((S, D_ATTN), BF16),
                   jax.ShapeDtypeStruct((nm, tm // MOBA_BLOCK, D_ATTN), F32)],
        scratch_shapes=scratch,
        compiler_params=_params(sem),
        name="inproj_k",
    )(x, g, w_in)

    qv_t = pl.pallas_call(
        _inproj_t_kernel,
        grid=(nm, 2 * D_ATTN // tn),
        in_specs=[x_spec, g_spec, pl.BlockSpec((tn, D), lambda i, j: (j, 0))],
        out_specs=pl.BlockSpec((1, tn, tm), lambda i, j: (i, j, 0)),
        out_shape=jax.ShapeDtypeStruct((nm, 2 * D_ATTN, tm), BF16),
        scratch_shapes=scratch,
        compiler_params=_params(sem),
        name="inproj_qv_t",
    )(x, g, wt_qv)

    gates = pl.pallas_call(
        _inproj_gate_kernel,
        grid=(nm, 2 * D_MODEL // tn),
        in_specs=[x_spec, g_spec, w_spec(2 * D_CONV + 3 * D_ATTN)],
        out_specs=pl.BlockSpec((tm, tn), lambda i, j: (i, j)),
        out_shape=jax.ShapeDtypeStruct((S, 2 * D_MODEL), BF16),
        scratch_shapes=scratch,
        compiler_params=_params(sem),
        name="inproj_gates",
    )(x, g, w_in)
    return glu, k, km.reshape(S // MOBA_BLOCK, D_ATTN), qv_t, gates


def _conv_kernel(halo_ref, x_ref, w_ref, b_ref, lg_ref, lb_ref, o_ref, win_ref, y_ref,
                 *, rows, lanes):
    tm, C = x_ref.shape
    i = pl.program_id(0)
    halo = halo_ref[...]
    win_ref[0:CONV_HALO, :] = jnp.where(i > 0, halo, jnp.zeros_like(halo))
    win_ref[CONV_HALO:, :] = x_ref[...]
    first = CONV_HALO - (CONV_K - 1)

    for c0 in range(0, C, lanes):
        def body(rc, carry):
            r = pl.multiple_of(rc * rows, rows)
            acc = jnp.zeros((rows, lanes), F32) + b_ref[:, c0:c0 + lanes]
            win = win_ref[pl.ds(r, rows + CONV_HALO), c0:c0 + lanes]
            for k in range(CONV_K):
                acc = acc + win[first + k:first + k + rows] * w_ref[k:k + 1, c0:c0 + lanes]
            y_ref[pl.ds(r, rows), c0:c0 + lanes] = acc
            return carry
        lax.fori_loop(0, tm // rows, body, 0)

    def ln_body(rc, carry):
        r = pl.multiple_of(rc * NORM_ROWS, NORM_ROWS)
        v = y_ref[pl.ds(r, NORM_ROWS), :]
        mu = jnp.mean(v, axis=-1, keepdims=True)
        var = jnp.mean(jnp.square(v - mu), axis=-1, keepdims=True)
        z = (v - mu) * lax.rsqrt(var + EPS) * lg_ref[...] + lb_ref[...]
        o_ref[pl.ds(r, NORM_ROWS), :] = jax.nn.silu(z).astype(o_ref.dtype)
        return carry
    lax.fori_loop(0, tm // NORM_ROWS, ln_body, 0)


def _conv_branch(glu, w, b, lg, lb, tm=256):
    S, C = glu.shape
    per = tm // CONV_HALO
    return pl.pallas_call(
        functools.partial(_conv_kernel, rows=32, lanes=256),
        grid=(S // tm,),
        in_specs=[pl.BlockSpec((CONV_HALO, C), lambda i: (jnp.maximum(i * per - 1, 0), 0)),
                  pl.BlockSpec((tm, C), lambda i: (i, 0)),
                  pl.BlockSpec((CONV_K, C), lambda i: (0, 0)),
                  pl.BlockSpec((1, C), lambda i: (0, 0)),
                  pl.BlockSpec((1, C), lambda i: (0, 0)),
                  pl.BlockSpec((1, C), lambda i: (0, 0))],
        out_specs=pl.BlockSpec((tm, C), lambda i: (i, 0)),
        out_shape=jax.ShapeDtypeStruct((S, C), BF16),
        scratch_shapes=[pltpu.VMEM((tm + CONV_HALO, C), F32), pltpu.VMEM((tm, C), F32)],
        compiler_params=_params(("parallel",)),
        name="conv_branch",
    )(glu, glu, w, b, lg, lb)


def _bias_kernel(rb_ref, o_ref):
    h = pl.program_id(0)
    delta = pl.program_id(1)
    shape = (MOBA_BLOCK, MOBA_BLOCK)
    d = delta * MOBA_BLOCK + lax.broadcasted_iota(jnp.int32, shape, 1) - lax.broadcasted_iota(jnp.int32, shape, 0)
    n = jnp.maximum(d, 0)
    max_exact = N_BUCKETS // 2
    nf = jnp.maximum(n, 1).astype(F32)
    large = max_exact + (jnp.log(nf / max_exact) / math.log(REL_MAX_DIST / max_exact)
                         * (N_BUCKETS - max_exact)).astype(jnp.int32)
    large = jnp.minimum(large, N_BUCKETS - 1)
    bucket = jnp.where(n < max_exact, n, large)
    val = jnp.zeros(shape, F32)
    for b in range(N_BUCKETS):
        val = jnp.where(bucket == b, rb_ref[b, h], val)
    val = jnp.where(delta == N_BIAS_TILES - 1, 0.0, val * LOG2E)
    o_ref[0, 0] = jnp.where(d >= 0, val, NEG)


def _bias_tiles(rel_bias):
    return pl.pallas_call(
        _bias_kernel,
        grid=(N_HEADS, N_BIAS_TILES),
        in_specs=[pl.BlockSpec(memory_space=pltpu.SMEM)],
        out_specs=pl.BlockSpec((1, 1, MOBA_BLOCK, MOBA_BLOCK), lambda h, d: (h, d, 0, 0)),
        out_shape=jax.ShapeDtypeStruct((N_HEADS, N_BIAS_TILES, MOBA_BLOCK, MOBA_BLOCK), F32),
        compiler_params=_params(("parallel", "parallel")),
        name="t5_bias_tiles",
    )(rel_bias)


def _attn_kernel(rb_ref, qt_ref, k_ref, vt_ref, km_ref, ind_ref, bias_ref, o_ref,
                 acc_ref, sa_ref, sb_ref, pa_ref, pb_ref):
    h = pl.program_id(0)
    i = pl.program_id(1)
    nb = km_ref.shape[0]
    tq = qt_ref.shape[2]
    scale = HEAD_DIM ** -0.5
    qt = qt_ref[0]
    qf = qt.astype(F32)

    gate = jnp.dot(km_ref[...], qf, precision=lax.Precision.HIGHEST, preferred_element_type=F32)
    row = lax.broadcasted_iota(jnp.int32, (nb, tq), 0)
    valid = row < i
    g = jnp.where(valid, gate, NEG)
    sel = jnp.zeros((nb, tq), jnp.bool_)
    for _ in range(MOBA_TOPK):
        top = jnp.max(g, axis=0, keepdims=True)
        idx = jnp.min(jnp.where(g == top, row, nb), axis=0, keepdims=True)
        hit = row == idx
        sel = jnp.logical_or(sel, hit)
        g = jnp.where(hit, -jnp.inf, g)
    picked = jnp.logical_and(sel, valid)
    far = jnp.logical_and(picked, row <= i - (N_BIAS_TILES - 1))
    far_bias = rb_ref[N_BUCKETS - 1, h] * LOG2E
    pen = jnp.where(row == i, 0.0, jnp.where(picked, jnp.where(far, far_bias, 0.0), NEG))
    pen_hi = pen.astype(BF16)
    pen_lo = jnp.where(far, pen - pen_hi.astype(F32), 0.0).astype(BF16)
    q_aug = jnp.concatenate(
        [(qf * (scale * LOG2E)).astype(BF16), pen_hi, pen_lo, jnp.zeros((HEAD_DIM - 2 * nb, tq), BF16)],
        axis=0)

    n_groups = i // KV_GROUP + 1
    ones_rows = jnp.ones((SUM_ROWS, GROUP_KEYS), BF16)

    def scores(grp, dst_ref):
        r0 = pl.multiple_of(grp * GROUP_KEYS, GROUP_KEYS)
        k_aug = jnp.concatenate([k_ref[pl.ds(r0, GROUP_KEYS), :], ind_ref[pl.ds(r0, GROUP_KEYS), :]], axis=1)
        dst_ref[...] = jnp.dot(k_aug, q_aug, preferred_element_type=F32)

    def softmax(grp, src_ref, dst_ref, m_prev):
        parts = []
        for u in range(KV_GROUP):
            delta = jnp.clip(i - (grp * KV_GROUP + u), 0, N_BIAS_TILES - 1)
            parts.append(src_ref[u * MOBA_BLOCK:(u + 1) * MOBA_BLOCK, :] + bias_ref[0, delta])
        s = jnp.concatenate(parts, axis=0)
        m_new = jnp.maximum(m_prev, jnp.max(s, axis=0, keepdims=True))
        dst_ref[...] = jnp.exp2(s - m_new).astype(BF16)
        return m_new, jnp.exp2(m_prev - m_new)

    def weighted_values(grp, live, src_ref, alpha):
        vt = jnp.concatenate([vt_ref[grp], ones_rows], axis=0)
        vt = jnp.where(live, vt, jnp.zeros_like(vt))
        acc_ref[...] = alpha * acc_ref[...] + jnp.dot(vt, src_ref[...], preferred_element_type=F32)

    acc_ref[...] = jnp.zeros_like(acc_ref)
    pb_ref[...] = jnp.zeros_like(pb_ref)
    scores(n_groups - 1, sa_ref)

    def body(u, carry):
        m_prev, alpha_b, grp_b_prev = carry
        grp_a = n_groups - 1 - 2 * u
        grp_b = grp_a - 1
        weighted_values(jnp.maximum(grp_b_prev, 0), grp_b_prev >= 0, pb_ref, alpha_b)
        m_a, alpha_a = softmax(grp_a, sa_ref, pa_ref, m_prev)
        scores(jnp.maximum(grp_b, 0), sb_ref)
        weighted_values(grp_a, True, pa_ref, alpha_a)
        m_b, alpha_b = softmax(jnp.maximum(grp_b, 0), sb_ref, pb_ref, m_a)
        scores(jnp.maximum(grp_b - 1, 0), sa_ref)
        return m_b, alpha_b, grp_b

    m0 = jnp.full((1, tq), NEG, F32)
    a0 = jnp.ones((1, tq), F32)
    _, alpha_b, grp_b = lax.fori_loop(0, (n_groups + 1) // 2, body, (m0, a0, jnp.int32(-1)))
    weighted_values(jnp.maximum(grp_b, 0), grp_b >= 0, pb_ref, alpha_b)
    acc = acc_ref[...]
    o_ref[...] = (acc[:HEAD_DIM] / acc[HEAD_DIM:HEAD_DIM + 1]).T.astype(o_ref.dtype)


def _attention(k, km, qv_t, bias, rel_bias):
    S = k.shape[0]
    nb = S // MOBA_BLOCK
    tq = MOBA_BLOCK
    per = GROUP_KEYS // tq
    heads_rows = D_ATTN // HEAD_DIM
    key_blk = jnp.arange(S, dtype=jnp.int32)[:, None] // MOBA_BLOCK
    lane = jnp.arange(HEAD_DIM, dtype=jnp.int32)[None, :]
    ind = jnp.logical_and(lane < 2 * nb, lane % nb == key_blk).astype(BF16)
    return pl.pallas_call(
        _attn_kernel,
        grid=(N_HEADS, S // tq),
        in_specs=[pl.BlockSpec(memory_space=pltpu.SMEM),
                  pl.BlockSpec((1, HEAD_DIM, tq), lambda h, i: (i // per, h, i % per)),
                  pl.BlockSpec((S, HEAD_DIM), lambda h, i: (0, h)),
                  pl.BlockSpec((S // GROUP_KEYS, HEAD_DIM, GROUP_KEYS), lambda h, i: (0, heads_rows + h, 0)),
                  pl.BlockSpec((nb, HEAD_DIM), lambda h, i: (0, h)),
                  pl.BlockSpec((S, HEAD_DIM), lambda h, i: (0, 0)),
                  pl.BlockSpec((1, N_BIAS_TILES, MOBA_BLOCK, MOBA_BLOCK), lambda h, i: (h, 0, 0, 0))],
        out_specs=pl.BlockSpec((tq, HEAD_DIM), lambda h, i: (i, h)),
        out_shape=jax.ShapeDtypeStruct((S, D_ATTN), BF16),
        scratch_shapes=[pltpu.VMEM((HEAD_DIM + SUM_ROWS, tq), F32),
                        pltpu.VMEM((GROUP_KEYS, tq), F32), pltpu.VMEM((GROUP_KEYS, tq), F32),
                        pltpu.VMEM((GROUP_KEYS, tq), BF16), pltpu.VMEM((GROUP_KEYS, tq), BF16)],
        compiler_params=_params(("parallel", "parallel")),
        name="moba_attention",
    )(rel_bias, qv_t, k, qv_t, km, ind, bias)


def _mix_kernel(cs_ref, at_ref, ga_ref, gb_ref, x_ref, wc_ref, wa_ref, wo_ref, o_ref):
    yc = jnp.dot(cs_ref[...], wc_ref[...], preferred_element_type=F32)
    ya = jnp.dot(at_ref[...], wa_ref[...], preferred_element_type=F32)
    merged = ga_ref[...].astype(F32) * yc + gb_ref[...].astype(F32) * ya
    o_ref[...] = x_ref[...] + jnp.dot(merged.astype(BF16), wo_ref[...], preferred_element_type=F32)


def _mix(cs, attn, gates, x, wc, wa, wo, tm=256):
    S, D = x.shape
    full = lambda shape: pl.BlockSpec(shape, lambda i: (0, 0))
    return pl.pallas_call(
        _mix_kernel,
        grid=(S // tm,),
        in_specs=[pl.BlockSpec((tm, D_CONV), lambda i: (i, 0)),
                  pl.BlockSpec((tm, D_ATTN), lambda i: (i, 0)),
                  pl.BlockSpec((tm, D), lambda i: (i, 0)),
                  pl.BlockSpec((tm, D), lambda i: (i, 1)),
                  pl.BlockSpec((tm, D), lambda i: (i, 0)),
                  full((D_CONV, D)), full((D_ATTN, D)), full((D, D))],
        out_specs=pl.BlockSpec((tm, D), lambda i: (i, 0)),
        out_shape=jax.ShapeDtypeStruct((S, D), F32),
        compiler_params=_params(("parallel",)),
        name="mix_out_proj",
    )(cs, attn, gates, gates, x, wc, wa, wo)


def _ffn_kernel(xh_ref, x_ref, g_ref, wa_ref, wb_ref, dwa_ref, dwb_ref, ba_ref, bb_ref, wd_ref,
                fg_ref, o_ref, h_ref, acc_ref, *, final_norm):
    i = pl.program_id(0)
    f = pl.program_id(1)
    tm = x_ref.shape[0]

    @pl.when(f == 0)
    def _():
        xh = xh_ref[...]
        ms = jnp.mean(xh * xh, axis=-1, keepdims=True)
        hh = xh * lax.rsqrt(ms + EPS) * g_ref[...]
        h_ref[0:HALO, :] = jnp.where(i > 0, hh, jnp.zeros_like(hh)).astype(BF16)
        _rmsnorm_rows(x_ref, g_ref, h_ref, tm, dst_off=HALO)

    h = h_ref[...]

    def up_conv(w_ref, dw_ref, b_ref):
        u = jnp.dot(h, w_ref[...], preferred_element_type=F32)
        y = (u * dw_ref[2:3, :] + pltpu.roll(u, 1, 0) * dw_ref[1:2, :]
             + pltpu.roll(u, 2, 0) * dw_ref[0:1, :] + b_ref[...])
        return y[HALO:]

    ua = up_conv(wa_ref, dwa_ref, ba_ref)
    ub = up_conv(wb_ref, dwb_ref, bb_ref)
    gated = (jax.nn.silu(ub) * ua).astype(BF16)
    contrib = jnp.dot(gated, wd_ref[...], preferred_element_type=F32)

    @pl.when(f == 0)
    def _():
        acc_ref[...] = contrib

    @pl.when(f > 0)
    def _():
        acc_ref[...] += contrib

    @pl.when(f == pl.num_programs(1) - 1)
    def _():
        if final_norm:
            def body(c, carry):
                r = pl.multiple_of(c * NORM_ROWS, NORM_ROWS)
                xv = x_ref[pl.ds(r, NORM_ROWS), :] + acc_ref[pl.ds(r, NORM_ROWS), :]
                ms = jnp.mean(xv * xv, axis=-1, keepdims=True)
                o_ref[pl.ds(r, NORM_ROWS), :] = xv * lax.rsqrt(ms + EPS) * fg_ref[...]
                return carry
            lax.fori_loop(0, tm // NORM_ROWS, body, 0)
        else:
            o_ref[...] = x_ref[...] + acc_ref[...]


def _ffn(x, g, w_up, dw, dwb, w_down, final_g, final_norm, tm=512, tf=512):
    S, D = x.shape
    nf = D_FF // tf
    per = tm // HALO
    row = lambda shape: pl.BlockSpec(shape, lambda i, f: (0, 0))
    return pl.pallas_call(
        functools.partial(_ffn_kernel, final_norm=final_norm),
        grid=(S // tm, nf),
        in_specs=[pl.BlockSpec((HALO, D), lambda i, f: (jnp.maximum(i * per - 1, 0), 0)),
                  pl.BlockSpec((tm, D), lambda i, f: (i, 0)),
                  row((1, D)),
                  pl.BlockSpec((D, tf), lambda i, f: (0, f)),
                  pl.BlockSpec((D, tf), lambda i, f: (0, f + nf)),
                  pl.BlockSpec((FFN_CONV_K, tf), lambda i, f: (0, f)),
                  pl.BlockSpec((FFN_CONV_K, tf), lambda i, f: (0, f + nf)),
                  pl.BlockSpec((1, tf), lambda i, f: (0, f)),
                  pl.BlockSpec((1, tf), lambda i, f: (0, f + nf)),
                  pl.BlockSpec((tf, D), lambda i, f: (f, 0)),
                  row((1, D))],
        out_specs=pl.BlockSpec((tm, D), lambda i, f: (i, 0)),
        out_shape=jax.ShapeDtypeStruct((S, D), F32),
        scratch_shapes=[pltpu.VMEM((HALO + tm, D), BF16), pltpu.VMEM((tm, D), F32)],
        compiler_params=_params(("parallel", "arbitrary")),
        name="conv_ffn",
    )(x, x, g, w_up, w_up, dw, dw, dwb, dwb, w_down, final_g)


def kernel(x, norm1_g, w_in, conv_dw, conv_dw_b, conv_ln_g, conv_ln_b, w_conv_out, rel_bias,
           w_attn_out, w_out, norm2_g, w_up, ffn_dw, ffn_dw_b, w_down, final_g):
    B, S, D = x.shape
    assert D == D_MODEL and S % GROUP_KEYS == 0
    depth = w_in.shape[0]
    bias = _bias_tiles(rel_bias)
    row = lambda v: v.reshape(1, -1)
    q0 = 2 * D_CONV
    v0 = q0 + 2 * D_ATTN
    outs = []
    for b in range(B):
        xb = x[b]
        for l in range(depth):
            w_l = w_in[l]
            wt_qv = jnp.concatenate([w_l[:, q0:q0 + D_ATTN], w_l[:, v0:v0 + D_ATTN]], axis=1).T.astype(BF16)
            glu, k, km, qv_t, gates = _inproj(xb, row(norm1_g[l]), w_l.astype(BF16), wt_qv)
            cs = _conv_branch(glu, conv_dw[l], row(conv_dw_b[l]), row(conv_ln_g[l]), row(conv_ln_b[l]))
            attn = _attention(k, km, qv_t, bias, rel_bias)
            xb = _mix(cs, attn, gates, xb, w_conv_out[l].astype(BF16), w_attn_out[l].astype(BF16),
                      w_out[l].astype(BF16))
            xb = _ffn(xb, row(norm2_g[l]), w_up[l].astype(BF16), ffn_dw[l], row(ffn_dw_b[l]),
                      w_down[l].astype(BF16), row(final_g), final_norm=(l == depth - 1))
        outs.append(xb)
    return jnp.stack(outs, axis=0)
```

```python
import functools
import math

import jax
import jax.numpy as jnp
from jax import lax
from jax.experimental import pallas as pl
from jax.experimental.pallas import tpu as pltpu

F32 = jnp.float32
BF16 = jnp.bfloat16

D_MODEL = 2048
D_CONV = 1024
CONV_K = 31
N_HEADS = 8
HEAD_DIM = 128
D_ATTN = N_HEADS * HEAD_DIM
MOBA_BLOCK = 256
MOBA_TOPK = 3
N_BUCKETS = 32
REL_MAX_DIST = 2048
D_FF = 5632
FFN_CONV_K = 3
EPS = 1e-6
NEG = -1e30
LOG2E = math.log2(math.e)

N_BIAS_TILES = 8
KV_GROUP = 4
GROUP_KEYS = KV_GROUP * MOBA_BLOCK
SUM_ROWS = 16

V7X_VMEM_BYTES = 64 * 1024 * 1024
VMEM_LIMIT = V7X_VMEM_BYTES - 8 * 1024 * 1024

NORM_ROWS = 16
NORM_UNROLL = 4
HALO = 8
CONV_HALO = 32

NT_DIMS = (((1,), (1,)), ((), ()))


def _params(sem):
    return pltpu.CompilerParams(dimension_semantics=sem, vmem_limit_bytes=VMEM_LIMIT)


def _rmsnorm_rows(x_ref, g_ref, dst_ref, n_rows, dst_off=0, out_dtype=BF16):
    def body(c, carry):
        r = pl.multiple_of(c * NORM_ROWS, NORM_ROWS)
        xv = x_ref[pl.ds(r, NORM_ROWS), :]
        ms = jnp.mean(xv * xv, axis=-1, keepdims=True)
        y = xv * lax.rsqrt(ms + EPS) * g_ref[...]
        dst_ref[pl.ds(dst_off + r, NORM_ROWS), :] = y.astype(out_dtype)
        return carry
    lax.fori_loop(0, n_rows // NORM_ROWS, body, 0, unroll=NORM_UNROLL)


def _inproj_glu_kernel(x_ref, g_ref, wa_ref, wb_ref, o_ref, h_ref):
    @pl.when(pl.program_id(1) == 0)
    def _():
        _rmsnorm_rows(x_ref, g_ref, h_ref, x_ref.shape[0])
    h = h_ref[...]
    a = jnp.dot(h, wa_ref[...], preferred_element_type=F32)
    b = jnp.dot(h, wb_ref[...], preferred_element_type=F32)
    o_ref[...] = a * jax.nn.sigmoid(b)


def _inproj_k_kernel(x_ref, g_ref, w_ref, o_ref, km_ref, h_ref):
    @pl.when(pl.program_id(1) == 0)
    def _():
        _rmsnorm_rows(x_ref, g_ref, h_ref, x_ref.shape[0])
    acc = jnp.dot(h_ref[...], w_ref[...], preferred_element_type=F32)
    o_ref[...] = acc.astype(o_ref.dtype)
    for b in range(acc.shape[0] // MOBA_BLOCK):
        blk = acc[b * MOBA_BLOCK:(b + 1) * MOBA_BLOCK]
        km_ref[0, b:b + 1, :] = jnp.mean(blk, axis=0, keepdims=True)


def _inproj_t_kernel(x_ref, g_ref, wt_ref, o_ref, h_ref):
    @pl.when(pl.program_id(1) == 0)
    def _():
        _rmsnorm_rows(x_ref, g_ref, h_ref, x_ref.shape[0])
    acc = lax.dot_general(wt_ref[...], h_ref[...], NT_DIMS, preferred_element_type=F32)
    o_ref[0] = acc.astype(o_ref.dtype)


def _inproj_gate_kernel(x_ref, g_ref, w_ref, o_ref, h_ref):
    @pl.when(pl.program_id(1) == 0)
    def _():
        _rmsnorm_rows(x_ref, g_ref, h_ref, x_ref.shape[0])
    acc = jnp.dot(h_ref[...], w_ref[...], preferred_element_type=F32)
    o_ref[...] = jax.nn.sigmoid(acc).astype(o_ref.dtype)


def _inproj(x, g, w_in, wt_qv, tn=512):
    S, D = x.shape
    tm = GROUP_KEYS
    nm = S // tm
    x_spec = pl.BlockSpec((tm, D), lambda i, j: (i, 0))
    g_spec = pl.BlockSpec((1, D), lambda i, j: (0, 0))
    scratch = [pltpu.VMEM((tm, D), BF16)]
    sem = ("parallel", "arbitrary")

    def w_spec(col0):
        off = col0 // tn
        return pl.BlockSpec((D, tn), lambda i, j: (0, j + off))

    glu = pl.pallas_call(
        _inproj_glu_kernel,
        grid=(nm, D_CONV // tn),
        in_specs=[x_spec, g_spec, w_spec(0), w_spec(D_CONV)],
        out_specs=pl.BlockSpec((tm, tn), lambda i, j: (i, j)),
        out_shape=jax.ShapeDtypeStruct((S, D_CONV), F32),
        scratch_shapes=scratch,
        compiler_params=_params(sem),
        name="inproj_glu",
    )(x, g, w_in, w_in)

    k, km = pl.pallas_call(
        _inproj_k_kernel,
        grid=(nm, D_ATTN // tn),
        in_specs=[x_spec, g_spec, w_spec(2 * D_CONV + D_ATTN)],
        out_specs=[pl.BlockSpec((tm, tn), lambda i, j: (i, j)),
                   pl.BlockSpec((1, tm // MOBA_BLOCK, tn), lambda i, j: (i, 0, j))],
        out_shape=[jax.ShapeDtypeStruct((S, D_ATTN), BF16),
                   jax.ShapeDtypeStruct((nm, tm // MOBA_BLOCK, D_ATTN), F32)],
        scratch_shapes=scratch,
        compiler_params=_params(sem),
        name="inproj_k",
    )(x, g, w_in)

    qv_t = pl.pallas_call(
        _inproj_t_kernel,
        grid=(nm, 2 * D_ATTN // tn),
        in_specs=[x_spec, g_spec, pl.BlockSpec((tn, D), lambda i, j: (j, 0))],
        out_specs=pl.BlockSpec((1, tn, tm), lambda i, j: (i, j, 0)),
        out_shape=jax.ShapeDtypeStruct((nm, 2 * D_ATTN, tm), BF16),
        scratch_shapes=scratch,
        compiler_params=_params(sem),
        name="inproj_qv_t",
    )(x, g, wt_qv)

    gates = pl.pallas_call(
        _inproj_gate_kernel,
        grid=(nm, 2 * D_MODEL // tn),
        in_specs=[x_spec, g_spec, w_spec(2 * D_CONV + 3 * D_ATTN)],
        out_specs=pl.BlockSpec((tm, tn), lambda i, j: (i, j)),
        out_shape=jax.ShapeDtypeStruct((S, 2 * D_MODEL), BF16),
        scratch_shapes=scratch,
        compiler_params=_params(sem),
        name="inproj_gates",
    )(x, g, w_in)
    return glu, k, km.reshape(S // MOBA_BLOCK, D_ATTN), qv_t, gates


def _conv_kernel(halo_ref, x_ref, w_ref, b_ref, lg_ref, lb_ref, o_ref, win_ref, y_ref,
                 *, rows, lanes):
    tm, C = x_ref.shape
    i = pl.program_id(0)
    halo = halo_ref[...]
    win_ref[0:CONV_HALO, :] = jnp.where(i > 0, halo, jnp.zeros_like(halo))
    win_ref[CONV_HALO:, :] = x_ref[...]
    first = CONV_HALO - (CONV_K - 1)

    for c0 in range(0, C, lanes):
        def body(rc, carry):
            r = pl.multiple_of(rc * rows, rows)
            acc = jnp.zeros((rows, lanes), F32) + b_ref[:, c0:c0 + lanes]
            win = win_ref[pl.ds(r, rows + CONV_HALO), c0:c0 + lanes]
            for k in range(CONV_K):
                acc = acc + win[first + k:first + k + rows] * w_ref[k:k + 1, c0:c0 + lanes]
            y_ref[pl.ds(r, rows), c0:c0 + lanes] = acc
            return carry
        lax.fori_loop(0, tm // rows, body, 0)

    def ln_body(rc, carry):
        r = pl.multiple_of(rc * NORM_ROWS, NORM_ROWS)
        v = y_ref[pl.ds(r, NORM_ROWS), :]
        mu = jnp.mean(v, axis=-1, keepdims=True)
        var = jnp.mean(jnp.square(v - mu), axis=-1, keepdims=True)
        z = (v - mu) * lax.rsqrt(var + EPS) * lg_ref[...] + lb_ref[...]
        o_ref[pl.ds(r, NORM_ROWS), :] = jax.nn.silu(z).astype(o_ref.dtype)
        return carry
    lax.fori_loop(0, tm // NORM_ROWS, ln_body, 0, unroll=NORM_UNROLL)


def _conv_branch(glu, w, b, lg, lb, tm=256):
    S, C = glu.shape
    per = tm // CONV_HALO
    return pl.pallas_call(
        functools.partial(_conv_kernel, rows=32, lanes=256),
        grid=(S // tm,),
        in_specs=[pl.BlockSpec((CONV_HALO, C), lambda i: (jnp.maximum(i * per - 1, 0), 0)),
                  pl.BlockSpec((tm, C), lambda i: (i, 0)),
                  pl.BlockSpec((CONV_K, C), lambda i: (0, 0)),
                  pl.BlockSpec((1, C), lambda i: (0, 0)),
                  pl.BlockSpec((1, C), lambda i: (0, 0)),
                  pl.BlockSpec((1, C), lambda i: (0, 0))],
        out_specs=pl.BlockSpec((tm, C), lambda i: (i, 0)),
        out_shape=jax.ShapeDtypeStruct((S, C), BF16),
        scratch_shapes=[pltpu.VMEM((tm + CONV_HALO, C), F32), pltpu.VMEM((tm, C), F32)],
        compiler_params=_params(("parallel",)),
        name="conv_branch",
    )(glu, glu, w, b, lg, lb)


def _bias_kernel(rb_ref, o_ref):
    h = pl.program_id(0)
    delta = pl.program_id(1)
    shape = (MOBA_BLOCK, MOBA_BLOCK)
    d = delta * MOBA_BLOCK + lax.broadcasted_iota(jnp.int32, shape, 1) - lax.broadcasted_iota(jnp.int32, shape, 0)
    n = jnp.maximum(d, 0)
    max_exact = N_BUCKETS // 2
    nf = jnp.maximum(n, 1).astype(F32)
    large = max_exact + (jnp.log(nf / max_exact) / math.log(REL_MAX_DIST / max_exact)
                         * (N_BUCKETS - max_exact)).astype(jnp.int32)
    large = jnp.minimum(large, N_BUCKETS - 1)
    bucket = jnp.where(n < max_exact, n, large)
    val = jnp.zeros(shape, F32)
    for b in range(N_BUCKETS):
        val = jnp.where(bucket == b, rb_ref[b, h], val)
    val = jnp.where(delta == N_BIAS_TILES - 1, 0.0, val * LOG2E)
    o_ref[0, 0] = jnp.where(d >= 0, val, NEG)


def _bias_tiles(rel_bias):
    return pl.pallas_call(
        _bias_kernel,
        grid=(N_HEADS, N_BIAS_TILES),
        in_specs=[pl.BlockSpec(memory_space=pltpu.SMEM)],
        out_specs=pl.BlockSpec((1, 1, MOBA_BLOCK, MOBA_BLOCK), lambda h, d: (h, d, 0, 0)),
        out_shape=jax.ShapeDtypeStruct((N_HEADS, N_BIAS_TILES, MOBA_BLOCK, MOBA_BLOCK), F32),
        compiler_params=_params(("parallel", "parallel")),
        name="t5_bias_tiles",
    )(rel_bias)


def _attn_kernel(rb_ref, qt_ref, k_ref, vt_ref, km_ref, ind_ref, bias_ref, o_ref,
                 acc_ref, sa_ref, sb_ref, pa_ref, pb_ref):
    h = pl.program_id(0)
    i = pl.program_id(1)
    nb = km_ref.shape[0]
    tq = qt_ref.shape[2]
    scale = HEAD_DIM ** -0.5
    qt = qt_ref[0]
    qf = qt.astype(F32)

    gate = jnp.dot(km_ref[...], qf, precision=lax.Precision.HIGHEST, preferred_element_type=F32)
    row = lax.broadcasted_iota(jnp.int32, (nb, tq), 0)
    valid = row < i
    g = jnp.where(valid, gate, NEG)
    sel = jnp.zeros((nb, tq), jnp.bool_)
    for _ in range(MOBA_TOPK):
        top = jnp.max(g, axis=0, keepdims=True)
        idx = jnp.min(jnp.where(g == top, row, nb), axis=0, keepdims=True)
        hit = row == idx
        sel = jnp.logical_or(sel, hit)
        g = jnp.where(hit, -jnp.inf, g)
    picked = jnp.logical_and(sel, valid)
    far = jnp.logical_and(picked, row <= i - (N_BIAS_TILES - 1))
    far_bias = rb_ref[N_BUCKETS - 1, h] * LOG2E
    pen = jnp.where(row == i, 0.0, jnp.where(picked, jnp.where(far, far_bias, 0.0), NEG))
    pen_hi = pen.astype(BF16)
    pen_lo = jnp.where(far, pen - pen_hi.astype(F32), 0.0).astype(BF16)
    q_aug = jnp.concatenate(
        [(qf * (scale * LOG2E)).astype(BF16), pen_hi, pen_lo, jnp.zeros((HEAD_DIM - 2 * nb, tq), BF16)],
        axis=0)

    n_groups = i // KV_GROUP + 1
    ones_rows = jnp.ones((SUM_ROWS, GROUP_KEYS), BF16)

    def scores(grp, dst_ref):
        r0 = pl.multiple_of(grp * GROUP_KEYS, GROUP_KEYS)
        k_aug = jnp.concatenate([k_ref[pl.ds(r0, GROUP_KEYS), :], ind_ref[pl.ds(r0, GROUP_KEYS), :]], axis=1)
        dst_ref[...] = jnp.dot(k_aug, q_aug, preferred_element_type=F32)

    def softmax(grp, src_ref, dst_ref, m_prev):
        parts = []
        for u in range(KV_GROUP):
            delta = jnp.clip(i - (grp * KV_GROUP + u), 0, N_BIAS_TILES - 1)
            parts.append(src_ref[u * MOBA_BLOCK:(u + 1) * MOBA_BLOCK, :] + bias_ref[0, delta])
        s = jnp.concatenate(parts, axis=0)
        m_new = jnp.maximum(m_prev, jnp.max(s, axis=0, keepdims=True))
        dst_ref[...] = jnp.exp2(s - m_new).astype(BF16)
        return m_new, jnp.exp2(m_prev - m_new)

    def weighted_values(grp, live, src_ref, alpha):
        vt = jnp.concatenate([vt_ref[grp], ones_rows], axis=0)
        vt = jnp.where(live, vt, jnp.zeros_like(vt))
        acc_ref[...] = alpha * acc_ref[...] + jnp.dot(vt, src_ref[...], preferred_element_type=F32)

    acc_ref[...] = jnp.zeros_like(acc_ref)
    pb_ref[...] = jnp.zeros_like(pb_ref)
    scores(n_groups - 1, sa_ref)

    def body(u, carry):
        m_prev, alpha_b, grp_b_prev = carry
        grp_a = n_groups - 1 - 2 * u
        grp_b = grp_a - 1
        weighted_values(jnp.maximum(grp_b_prev, 0), grp_b_prev >= 0, pb_ref, alpha_b)
        m_a, alpha_a = softmax(grp_a, sa_ref, pa_ref, m_prev)
        scores(jnp.maximum(grp_b, 0), sb_ref)
        weighted_values(grp_a, True, pa_ref, alpha_a)
        m_b, alpha_b = softmax(jnp.maximum(grp_b, 0), sb_ref, pb_ref, m_a)
        scores(jnp.maximum(grp_b - 1, 0), sa_ref)
        return m_b, alpha_b, grp_b

    m0 = jnp.full((1, tq), NEG, F32)
    a0 = jnp.ones((1, tq), F32)
    _, alpha_b, grp_b = lax.fori_loop(0, (n_groups + 1) // 2, body, (m0, a0, jnp.int32(-1)))
    weighted_values(jnp.maximum(grp_b, 0), grp_b >= 0, pb_ref, alpha_b)
    acc = acc_ref[...]
    o_ref[...] = (acc[:HEAD_DIM] / acc[HEAD_DIM:HEAD_DIM + 1]).T.astype(o_ref.dtype)


def _attention(k, km, qv_t, bias, rel_bias):
    S = k.shape[0]
    nb = S // MOBA_BLOCK
    tq = MOBA_BLOCK
    per = GROUP_KEYS // tq
    heads_rows = D_ATTN // HEAD_DIM
    key_blk = jnp.arange(S, dtype=jnp.int32)[:, None] // MOBA_BLOCK
    lane = jnp.arange(HEAD_DIM, dtype=jnp.int32)[None, :]
    ind = jnp.logical_and(lane < 2 * nb, lane % nb == key_blk).astype(BF16)
    return pl.pallas_call(
        _attn_kernel,
        grid=(N_HEADS, S // tq),
        in_specs=[pl.BlockSpec(memory_space=pltpu.SMEM),
                  pl.BlockSpec((1, HEAD_DIM, tq), lambda h, i: (i // per, h, i % per)),
                  pl.BlockSpec((S, HEAD_DIM), lambda h, i: (0, h)),
                  pl.BlockSpec((S // GROUP_KEYS, HEAD_DIM, GROUP_KEYS), lambda h, i: (0, heads_rows + h, 0)),
                  pl.BlockSpec((nb, HEAD_DIM), lambda h, i: (0, h)),
                  pl.BlockSpec((S, HEAD_DIM), lambda h, i: (0, 0)),
                  pl.BlockSpec((1, N_BIAS_TILES, MOBA_BLOCK, MOBA_BLOCK), lambda h, i: (h, 0, 0, 0))],
        out_specs=pl.BlockSpec((tq, HEAD_DIM), lambda h, i: (i, h)),
        out_shape=jax.ShapeDtypeStruct((S, D_ATTN), BF16),
        scratch_shapes=[pltpu.VMEM((HEAD_DIM + SUM_ROWS, tq), F32),
                        pltpu.VMEM((GROUP_KEYS, tq), F32), pltpu.VMEM((GROUP_KEYS, tq), F32),
                        pltpu.VMEM((GROUP_KEYS, tq), BF16), pltpu.VMEM((GROUP_KEYS, tq), BF16)],
        compiler_params=_params(("parallel", "parallel")),
        name="moba_attention",
    )(rel_bias, qv_t, k, qv_t, km, ind, bias)


def _mix_kernel(cs_ref, at_ref, ga_ref, gb_ref, x_ref, wc_ref, wa_ref, wo_ref, o_ref):
    yc = jnp.dot(cs_ref[...], wc_ref[...], preferred_element_type=F32)
    ya = jnp.dot(at_ref[...], wa_ref[...], preferred_element_type=F32)
    merged = ga_ref[...].astype(F32) * yc + gb_ref[...].astype(F32) * ya
    o_ref[...] = x_ref[...] + jnp.dot(merged.astype(BF16), wo_ref[...], preferred_element_type=F32)


def _mix(cs, attn, gates, x, wc, wa, wo, tm=256):
    S, D = x.shape
    full = lambda shape: pl.BlockSpec(shape, lambda i: (0, 0))
    return pl.pallas_call(
        _mix_kernel,
        grid=(S // tm,),
        in_specs=[pl.BlockSpec((tm, D_CONV), lambda i: (i, 0)),
                  pl.BlockSpec((tm, D_ATTN), lambda i: (i, 0)),
                  pl.BlockSpec((tm, D), lambda i: (i, 0)),
                  pl.BlockSpec((tm, D), lambda i: (i, 1)),
                  pl.BlockSpec((tm, D), lambda i: (i, 0)),
                  full((D_CONV, D)), full((D_ATTN, D)), full((D, D))],
        out_specs=pl.BlockSpec((tm, D), lambda i: (i, 0)),
        out_shape=jax.ShapeDtypeStruct((S, D), F32),
        compiler_params=_params(("parallel",)),
        name="mix_out_proj",
    )(cs, attn, gates, gates, x, wc, wa, wo)


def _ffn_kernel(xh_ref, x_ref, g_ref, wa_ref, wb_ref, dwa_ref, dwb_ref, ba_ref, bb_ref, wd_ref,
                fg_ref, o_ref, h_ref, acc_ref, gate_ref, *, final_norm, n_tiles, up_chunk):
    i = pl.program_id(0)
    f = pl.program_id(1)
    tm = x_ref.shape[0]
    tf = wa_ref.shape[1]
    slot = f % 2

    def up_stage():
        h = h_ref[...]
        for c0 in range(0, tf, up_chunk):
            cols = slice(c0, c0 + up_chunk)

            def up_conv(w_ref, dw_ref, b_ref):
                u = jnp.dot(h, w_ref[:, cols], preferred_element_type=F32)
                y = (u * dw_ref[2:3, cols] + pltpu.roll(u, 1, 0) * dw_ref[1:2, cols]
                     + pltpu.roll(u, 2, 0) * dw_ref[0:1, cols] + b_ref[:, cols])
                return y[HALO:]

            ua = up_conv(wa_ref, dwa_ref, ba_ref)
            ub = up_conv(wb_ref, dwb_ref, bb_ref)
            gate_ref[slot, :, cols] = (jax.nn.silu(ub) * ua).astype(BF16)

    def down_product():
        return jnp.dot(gate_ref[1 - slot], wd_ref[...], preferred_element_type=F32)

    @pl.when(f == 0)
    def _():
        xh = xh_ref[...]
        ms = jnp.mean(xh * xh, axis=-1, keepdims=True)
        hh = xh * lax.rsqrt(ms + EPS) * g_ref[...]
        h_ref[0:HALO, :] = jnp.where(i > 0, hh, jnp.zeros_like(hh)).astype(BF16)
        _rmsnorm_rows(x_ref, g_ref, h_ref, tm, dst_off=HALO)
        acc_ref[...] = jnp.zeros_like(acc_ref)
        up_stage()

    @pl.when(jnp.logical_and(f > 0, f < n_tiles))
    def _():
        acc_ref[...] += down_product()
        up_stage()

    @pl.when(f == n_tiles)
    def _():
        if final_norm:
            acc_ref[...] += down_product()

            def body(c, carry):
                r = pl.multiple_of(c * NORM_ROWS, NORM_ROWS)
                xv = x_ref[pl.ds(r, NORM_ROWS), :] + acc_ref[pl.ds(r, NORM_ROWS), :]
                ms = jnp.mean(xv * xv, axis=-1, keepdims=True)
                o_ref[pl.ds(r, NORM_ROWS), :] = xv * lax.rsqrt(ms + EPS) * fg_ref[...]
                return carry
            lax.fori_loop(0, tm // NORM_ROWS, body, 0, unroll=NORM_UNROLL)
        else:
            o_ref[...] = x_ref[...] + acc_ref[...] + down_product()


def _ffn(x, g, w_up, dw, dwb, w_down, final_g, final_norm, tm=512, tf=512):
    S, D = x.shape
    nf = D_FF // tf
    per = tm // HALO
    row = lambda shape: pl.BlockSpec(shape, lambda i, f: (0, 0))
    up = lambda f: jnp.minimum(f, nf - 1)
    down = lambda f: jnp.maximum(f - 1, 0)
    return pl.pallas_call(
        functools.partial(_ffn_kernel, final_norm=final_norm, n_tiles=nf, up_chunk=tf // 2),
        grid=(S // tm, nf + 1),
        in_specs=[pl.BlockSpec((HALO, D), lambda i, f: (jnp.maximum(i * per - 1, 0), 0)),
                  pl.BlockSpec((tm, D), lambda i, f: (i, 0)),
                  row((1, D)),
                  pl.BlockSpec((D, tf), lambda i, f: (0, up(f))),
                  pl.BlockSpec((D, tf), lambda i, f: (0, up(f) + nf)),
                  pl.BlockSpec((FFN_CONV_K, tf), lambda i, f: (0, up(f))),
                  pl.BlockSpec((FFN_CONV_K, tf), lambda i, f: (0, up(f) + nf)),
                  pl.BlockSpec((1, tf), lambda i, f: (0, up(f))),
                  pl.BlockSpec((1, tf), lambda i, f: (0, up(f) + nf)),
                  pl.BlockSpec((tf, D), lambda i, f: (down(f), 0)),
                  row((1, D))],
        out_specs=pl.BlockSpec((tm, D), lambda i, f: (i, 0)),
        out_shape=jax.ShapeDtypeStruct((S, D), F32),
        scratch_shapes=[pltpu.VMEM((HALO + tm, D), BF16), pltpu.VMEM((tm, D), F32),
                        pltpu.VMEM((2, tm, tf), BF16)],
        compiler_params=_params(("parallel", "arbitrary")),
        name="conv_ffn",
    )(x, x, g, w_up, w_up, dw, dw, dwb, dwb, w_down, final_g)


def kernel(x, norm1_g, w_in, conv_dw, conv_dw_b, conv_ln_g, conv_ln_b, w_conv_out, rel_bias,
           w_attn_out, w_out, norm2_g, w_up, ffn_dw, ffn_dw_b, w_down, final_g):
    B, S, D = x.shape
    assert D == D_MODEL and S % GROUP_KEYS == 0
    depth = w_in.shape[0]
    bias = _bias_tiles(rel_bias)
    row = lambda v: v.reshape(1, -1)
    q0 = 2 * D_CONV
    v0 = q0 + 2 * D_ATTN
    outs = []
    for b in range(B):
        xb = x[b]
        for l in range(depth):
            w_l = w_in[l]
            wt_qv = jnp.concatenate([w_l[:, q0:q0 + D_ATTN], w_l[:, v0:v0 + D_ATTN]], axis=1).T.astype(BF16)
            glu, k, km, qv_t, gates = _inproj(xb, row(norm1_g[l]), w_l.astype(BF16), wt_qv)
            cs = _conv_branch(glu, conv_dw[l], row(conv_dw_b[l]), row(conv_ln_g[l]), row(conv_ln_b[l]))
            attn = _attention(k, km, qv_t, bias, rel_bias)
            xb = _mix(cs, attn, gates, xb, w_conv_out[l].astype(BF16), w_attn_out[l].astype(BF16),
                      w_out[l].astype(BF16))
            xb = _ffn(xb, row(norm2_g[l]), w_up[l].astype(BF16), ffn_dw[l], row(ffn_dw_b[l]),
                      w_down[l].astype(BF16), row(final_g), final_norm=(l == depth - 1))
        outs.append(xb)
    return jnp.stack(outs, axis=0)
```

```python
import functools
import math

import jax
import jax.numpy as jnp
from jax import lax
from jax.experimental import pallas as pl
from jax.experimental.pallas import tpu as pltpu

F32 = jnp.float32
BF16 = jnp.bfloat16

D_MODEL = 2048
D_CONV = 1024
CONV_K = 31
N_HEADS = 8
HEAD_DIM = 128
D_ATTN = N_HEADS * HEAD_DIM
MOBA_BLOCK = 256
MOBA_TOPK = 3
N_BUCKETS = 32
REL_MAX_DIST = 2048
D_FF = 5632
FFN_CONV_K = 3
EPS = 1e-6
NEG = -1e30
LOG2E = math.log2(math.e)

N_BIAS_TILES = 8
KV_GROUP = 4
GROUP_KEYS = KV_GROUP * MOBA_BLOCK
SUM_ROWS = 16
V_ROWS = HEAD_DIM + SUM_ROWS
ATTN_HEADS = 2

V7X_VMEM_BYTES = 64 * 1024 * 1024
VMEM_LIMIT = V7X_VMEM_BYTES - 8 * 1024 * 1024

NORM_ROWS = 16
NORM_UNROLL = 4
HALO = 8
CONV_HALO = 32

NT_DIMS = (((1,), (1,)), ((), ()))


def _params(sem):
    return pltpu.CompilerParams(dimension_semantics=sem, vmem_limit_bytes=VMEM_LIMIT)


def _rmsnorm_rows(x_ref, g_ref, dst_ref, n_rows, dst_off=0, out_dtype=BF16):
    def body(c, carry):
        r = pl.multiple_of(c * NORM_ROWS, NORM_ROWS)
        xv = x_ref[pl.ds(r, NORM_ROWS), :]
        ms = jnp.mean(xv * xv, axis=-1, keepdims=True)
        y = xv * lax.rsqrt(ms + EPS) * g_ref[...]
        dst_ref[pl.ds(dst_off + r, NORM_ROWS), :] = y.astype(out_dtype)
        return carry
    lax.fori_loop(0, n_rows // NORM_ROWS, body, 0, unroll=NORM_UNROLL)


def _inproj_glu_kernel(x_ref, g_ref, wa_ref, wb_ref, o_ref, h_ref):
    @pl.when(pl.program_id(1) == 0)
    def _():
        _rmsnorm_rows(x_ref, g_ref, h_ref, x_ref.shape[0])
    h = h_ref[...]
    a = jnp.dot(h, wa_ref[...], preferred_element_type=F32)
    b = jnp.dot(h, wb_ref[...], preferred_element_type=F32)
    o_ref[...] = a * jax.nn.sigmoid(b)


def _inproj_k_kernel(x_ref, g_ref, w_ref, o_ref, km_ref, h_ref):
    @pl.when(pl.program_id(1) == 0)
    def _():
        _rmsnorm_rows(x_ref, g_ref, h_ref, x_ref.shape[0])
    acc = jnp.dot(h_ref[...], w_ref[...], preferred_element_type=F32)
    o_ref[...] = acc.astype(o_ref.dtype)
    for b in range(acc.shape[0] // MOBA_BLOCK):
        blk = acc[b * MOBA_BLOCK:(b + 1) * MOBA_BLOCK]
        km_ref[0, b:b + 1, :] = jnp.mean(blk, axis=0, keepdims=True)


def _inproj_t_kernel(x_ref, g_ref, wt_ref, q_ref, v_ref, h_ref, *, q_tiles):
    j = pl.program_id(1)

    @pl.when(j == 0)
    def _():
        _rmsnorm_rows(x_ref, g_ref, h_ref, x_ref.shape[0])
    acc = lax.dot_general(wt_ref[...], h_ref[...], NT_DIMS, preferred_element_type=F32)

    @pl.when(j < q_tiles)
    def _():
        q_ref[0] = acc.astype(q_ref.dtype)

    @pl.when(j >= q_tiles)
    def _():
        for hh in range(acc.shape[0] // HEAD_DIM):
            r = hh * V_ROWS
            v_ref[0, r:r + HEAD_DIM, :] = acc[hh * HEAD_DIM:(hh + 1) * HEAD_DIM].astype(v_ref.dtype)
            v_ref[0, r + HEAD_DIM:r + V_ROWS, :] = jnp.ones((SUM_ROWS, acc.shape[1]), v_ref.dtype)


def _inproj_gate_kernel(x_ref, g_ref, w_ref, o_ref, h_ref):
    @pl.when(pl.program_id(1) == 0)
    def _():
        _rmsnorm_rows(x_ref, g_ref, h_ref, x_ref.shape[0])
    acc = jnp.dot(h_ref[...], w_ref[...], preferred_element_type=F32)
    o_ref[...] = jax.nn.sigmoid(acc).astype(o_ref.dtype)


def _inproj(x, g, w_in, wt_qv, tn=512):
    S, D = x.shape
    tm = GROUP_KEYS
    nm = S // tm
    x_spec = pl.BlockSpec((tm, D), lambda i, j: (i, 0))
    g_spec = pl.BlockSpec((1, D), lambda i, j: (0, 0))
    scratch = [pltpu.VMEM((tm, D), BF16)]
    sem = ("parallel", "arbitrary")

    def w_spec(col0):
        off = col0 // tn
        return pl.BlockSpec((D, tn), lambda i, j: (0, j + off))

    glu = pl.pallas_call(
        _inproj_glu_kernel,
        grid=(nm, D_CONV // tn),
        in_specs=[x_spec, g_spec, w_spec(0), w_spec(D_CONV)],
        out_specs=pl.BlockSpec((tm, tn), lambda i, j: (i, j)),
        out_shape=jax.ShapeDtypeStruct((S, D_CONV), F32),
        scratch_shapes=scratch,
        compiler_params=_params(sem),
        name="inproj_glu",
    )(x, g, w_in, w_in)

    k, km = pl.pallas_call(
        _inproj_k_kernel,
        grid=(nm, D_ATTN // tn),
        in_specs=[x_spec, g_spec, w_spec(2 * D_CONV + D_ATTN)],
        out_specs=[pl.BlockSpec((tm, tn), lambda i, j: (i, j)),
                   pl.BlockSpec((1, tm // MOBA_BLOCK, tn), lambda i, j: (i, 0, j))],
        out_shape=[jax.ShapeDtypeStruct((S, D_ATTN), BF16),
                   jax.ShapeDtypeStruct((nm, tm // MOBA_BLOCK, D_ATTN), F32)],
        scratch_shapes=scratch,
        compiler_params=_params(sem),
        name="inproj_k",
    )(x, g, w_in)

    q_tiles = D_ATTN // tn
    v_tile_rows = tn // HEAD_DIM * V_ROWS
    q_t, v_t = pl.pallas_call(
        functools.partial(_inproj_t_kernel, q_tiles=q_tiles),
        grid=(nm, 2 * q_tiles),
        in_specs=[x_spec, g_spec, pl.BlockSpec((tn, D), lambda i, j: (j, 0))],
        out_specs=[pl.BlockSpec((1, tn, tm), lambda i, j: (i, jnp.minimum(j, q_tiles - 1), 0)),
                   pl.BlockSpec((1, v_tile_rows, tm), lambda i, j: (i, jnp.maximum(j - q_tiles, 0), 0))],
        out_shape=[jax.ShapeDtypeStruct((nm, D_ATTN, tm), BF16),
                   jax.ShapeDtypeStruct((nm, N_HEADS * V_ROWS, tm), BF16)],
        scratch_shapes=scratch,
        compiler_params=_params(sem),
        name="inproj_qv_t",
    )(x, g, wt_qv)

    gates = pl.pallas_call(
        _inproj_gate_kernel,
        grid=(nm, 2 * D_MODEL // tn),
        in_specs=[x_spec, g_spec, w_spec(2 * D_CONV + 3 * D_ATTN)],
        out_specs=pl.BlockSpec((tm, tn), lambda i, j: (i, j)),
        out_shape=jax.ShapeDtypeStruct((S, 2 * D_MODEL), BF16),
        scratch_shapes=scratch,
        compiler_params=_params(sem),
        name="inproj_gates",
    )(x, g, w_in)
    return glu, k, km.reshape(S // MOBA_BLOCK, D_ATTN), q_t, v_t, gates


def _conv_kernel(halo_ref, x_ref, w_ref, b_ref, lg_ref, lb_ref, o_ref, win_ref, y_ref,
                 *, rows, lanes):
    tm, C = x_ref.shape
    i = pl.program_id(0)
    halo = halo_ref[...]
    win_ref[0:CONV_HALO, :] = jnp.where(i > 0, halo, jnp.zeros_like(halo))
    win_ref[CONV_HALO:, :] = x_ref[...]
    first = CONV_HALO - (CONV_K - 1)

    for c0 in range(0, C, lanes):
        def body(rc, carry):
            r = pl.multiple_of(rc * rows, rows)
            acc = jnp.zeros((rows, lanes), F32) + b_ref[:, c0:c0 + lanes]
            win = win_ref[pl.ds(r, rows + CONV_HALO), c0:c0 + lanes]
            for k in range(CONV_K):
                acc = acc + win[first + k:first + k + rows] * w_ref[k:k + 1, c0:c0 + lanes]
            y_ref[pl.ds(r, rows), c0:c0 + lanes] = acc
            return carry
        lax.fori_loop(0, tm // rows, body, 0)

    def ln_body(rc, carry):
        r = pl.multiple_of(rc * NORM_ROWS, NORM_ROWS)
        v = y_ref[pl.ds(r, NORM_ROWS), :]
        mu = jnp.mean(v, axis=-1, keepdims=True)
        var = jnp.mean(jnp.square(v - mu), axis=-1, keepdims=True)
        z = (v - mu) * lax.rsqrt(var + EPS) * lg_ref[...] + lb_ref[...]
        o_ref[pl.ds(r, NORM_ROWS), :] = jax.nn.silu(z).astype(o_ref.dtype)
        return carry
    lax.fori_loop(0, tm // NORM_ROWS, ln_body, 0, unroll=NORM_UNROLL)


def _conv_branch(glu, w, b, lg, lb, tm=256):
    S, C = glu.shape
    per = tm // CONV_HALO
    return pl.pallas_call(
        functools.partial(_conv_kernel, rows=32, lanes=256),
        grid=(S // tm,),
        in_specs=[pl.BlockSpec((CONV_HALO, C), lambda i: (jnp.maximum(i * per - 1, 0), 0)),
                  pl.BlockSpec((tm, C), lambda i: (i, 0)),
                  pl.BlockSpec((CONV_K, C), lambda i: (0, 0)),
                  pl.BlockSpec((1, C), lambda i: (0, 0)),
                  pl.BlockSpec((1, C), lambda i: (0, 0)),
                  pl.BlockSpec((1, C), lambda i: (0, 0))],
        out_specs=pl.BlockSpec((tm, C), lambda i: (i, 0)),
        out_shape=jax.ShapeDtypeStruct((S, C), BF16),
        scratch_shapes=[pltpu.VMEM((tm + CONV_HALO, C), F32), pltpu.VMEM((tm, C), F32)],
        compiler_params=_params(("parallel",)),
        name="conv_branch",
    )(glu, glu, w, b, lg, lb)


def _bias_kernel(rb_ref, o_ref):
    h = pl.program_id(0)
    delta = pl.program_id(1)
    shape = (MOBA_BLOCK, MOBA_BLOCK)
    d = delta * MOBA_BLOCK + lax.broadcasted_iota(jnp.int32, shape, 1) - lax.broadcasted_iota(jnp.int32, shape, 0)
    n = jnp.maximum(d, 0)
    max_exact = N_BUCKETS // 2
    nf = jnp.maximum(n, 1).astype(F32)
    large = max_exact + (jnp.log(nf / max_exact) / math.log(REL_MAX_DIST / max_exact)
                         * (N_BUCKETS - max_exact)).astype(jnp.int32)
    large = jnp.minimum(large, N_BUCKETS - 1)
    bucket = jnp.where(n < max_exact, n, large)
    val = jnp.zeros(shape, F32)
    for b in range(N_BUCKETS):
        val = jnp.where(bucket == b, rb_ref[b, h], val)
    val = jnp.where(delta == N_BIAS_TILES - 1, 0.0, val * LOG2E)
    o_ref[0, 0] = jnp.where(d >= 0, val, NEG)


def _bias_tiles(rel_bias):
    return pl.pallas_call(
        _bias_kernel,
        grid=(N_HEADS, N_BIAS_TILES),
        in_specs=[pl.BlockSpec(memory_space=pltpu.SMEM)],
        out_specs=pl.BlockSpec((1, 1, MOBA_BLOCK, MOBA_BLOCK), lambda h, d: (h, d, 0, 0)),
        out_shape=jax.ShapeDtypeStruct((N_HEADS, N_BIAS_TILES, MOBA_BLOCK, MOBA_BLOCK), F32),
        compiler_params=_params(("parallel", "parallel")),
        name="t5_bias_tiles",
    )(rel_bias)


def _attn_kernel(rb_ref, qt_ref, k_ref, vt_ref, km_ref, ind_ref, bias_ref, o_ref,
                 acc_ref, sa_ref, sb_ref, pa_ref, pb_ref):
    i = pl.program_id(1)
    nb = km_ref.shape[0]
    tq = qt_ref.shape[2]
    scale = HEAD_DIM ** -0.5
    heads = range(ATTN_HEADS)
    lanes = [slice(hh * HEAD_DIM, (hh + 1) * HEAD_DIM) for hh in heads]

    def widened_query(hh):
        qf = qt_ref[0, lanes[hh], :].astype(F32)
        gate = jnp.dot(km_ref[:, lanes[hh]], qf, precision=lax.Precision.HIGHEST, preferred_element_type=F32)
        row = lax.broadcasted_iota(jnp.int32, (nb, tq), 0)
        valid = row < i
        g = jnp.where(valid, gate, NEG)
        sel = jnp.zeros((nb, tq), jnp.bool_)
        for _ in range(MOBA_TOPK):
            top = jnp.max(g, axis=0, keepdims=True)
            idx = jnp.min(jnp.where(g == top, row, nb), axis=0, keepdims=True)
            hit = row == idx
            sel = jnp.logical_or(sel, hit)
            g = jnp.where(hit, -jnp.inf, g)
        picked = jnp.logical_and(sel, valid)
        far = jnp.logical_and(picked, row <= i - (N_BIAS_TILES - 1))
        far_bias = rb_ref[N_BUCKETS - 1, pl.program_id(0) * ATTN_HEADS + hh] * LOG2E
        pen = jnp.where(row == i, 0.0, jnp.where(picked, jnp.where(far, far_bias, 0.0), NEG))
        pen_hi = pen.astype(BF16)
        pen_lo = jnp.where(far, pen - pen_hi.astype(F32), 0.0).astype(BF16)
        return jnp.concatenate(
            [(qf * (scale * LOG2E)).astype(BF16), pen_hi, pen_lo, jnp.zeros((HEAD_DIM - 2 * nb, tq), BF16)],
            axis=0)

    q_aug = [widened_query(hh) for hh in heads]
    n_groups = i // KV_GROUP + 1

    def scores(grp, dst_ref):
        r0 = pl.multiple_of(grp * GROUP_KEYS, GROUP_KEYS)
        ind = ind_ref[pl.ds(r0, GROUP_KEYS), :]
        for hh in heads:
            k_aug = jnp.concatenate([k_ref[pl.ds(r0, GROUP_KEYS), lanes[hh]], ind], axis=1)
            dst_ref[hh] = jnp.dot(k_aug, q_aug[hh], preferred_element_type=F32)

    def softmax(grp, src_ref, dst_ref, m_prev):
        m_out, alpha_out = [], []
        for hh in heads:
            parts = []
            for u in range(KV_GROUP):
                delta = jnp.clip(i - (grp * KV_GROUP + u), 0, N_BIAS_TILES - 1)
                parts.append(src_ref[hh, u * MOBA_BLOCK:(u + 1) * MOBA_BLOCK, :] + bias_ref[hh, delta])
            s = jnp.concatenate(parts, axis=0)
            m_new = jnp.maximum(m_prev[hh], jnp.max(s, axis=0, keepdims=True))
            dst_ref[hh] = jnp.exp2(s - m_new).astype(BF16)
            m_out.append(m_new)
            alpha_out.append(jnp.exp2(m_prev[hh] - m_new))
        return tuple(m_out), tuple(alpha_out)

    def weighted_values(grp, live, src_ref, alpha):
        for hh in heads:
            pv = jnp.dot(vt_ref[grp, hh * V_ROWS:(hh + 1) * V_ROWS, :], src_ref[hh], preferred_element_type=F32)
            acc_ref[hh] = alpha[hh] * acc_ref[hh] + jnp.where(live, pv, 0.0)

    acc_ref[...] = jnp.zeros_like(acc_ref)
    pb_ref[...] = jnp.zeros_like(pb_ref)
    scores(n_groups - 1, sa_ref)

    def body(u, carry):
        m_prev, alpha_b, grp_b_prev = carry
        grp_a = n_groups - 1 - 2 * u
        grp_b = grp_a - 1
        weighted_values(jnp.maximum(grp_b_prev, 0), grp_b_prev >= 0, pb_ref, alpha_b)
        m_a, alpha_a = softmax(grp_a, sa_ref, pa_ref, m_prev)
        scores(jnp.maximum(grp_b, 0), sb_ref)
        weighted_values(grp_a, True, pa_ref, alpha_a)
        m_b, alpha_b = softmax(jnp.maximum(grp_b, 0), sb_ref, pb_ref, m_a)
        scores(jnp.maximum(grp_b - 1, 0), sa_ref)
        return m_b, alpha_b, grp_b

    m0 = tuple(jnp.full((1, tq), NEG, F32) for _ in heads)
    a0 = tuple(jnp.ones((1, tq), F32) for _ in heads)
    _, alpha_b, grp_b = lax.fori_loop(0, (n_groups + 1) // 2, body, (m0, a0, jnp.int32(-1)))
    weighted_values(jnp.maximum(grp_b, 0), grp_b >= 0, pb_ref, alpha_b)
    for hh in heads:
        acc = acc_ref[hh]
        o_ref[:, lanes[hh]] = (acc[:HEAD_DIM] / acc[HEAD_DIM:HEAD_DIM + 1]).T.astype(o_ref.dtype)


def _attention(k, km, q_t, v_t, bias, rel_bias):
    S = k.shape[0]
    nb = S // MOBA_BLOCK
    tq = MOBA_BLOCK
    per = GROUP_KEYS // tq
    width = ATTN_HEADS * HEAD_DIM
    key_blk = jnp.arange(S, dtype=jnp.int32)[:, None] // MOBA_BLOCK
    lane = jnp.arange(HEAD_DIM, dtype=jnp.int32)[None, :]
    ind = jnp.logical_and(lane < 2 * nb, lane % nb == key_blk).astype(BF16)
    return pl.pallas_call(
        _attn_kernel,
        grid=(N_HEADS // ATTN_HEADS, S // tq),
        in_specs=[pl.BlockSpec(memory_space=pltpu.SMEM),
                  pl.BlockSpec((1, width, tq), lambda h, i: (i // per, h, i % per)),
                  pl.BlockSpec((S, width), lambda h, i: (0, h)),
                  pl.BlockSpec((S // GROUP_KEYS, ATTN_HEADS * V_ROWS, GROUP_KEYS), lambda h, i: (0, h, 0)),
                  pl.BlockSpec((nb, width), lambda h, i: (0, h)),
                  pl.BlockSpec((S, HEAD_DIM), lambda h, i: (0, 0)),
                  pl.BlockSpec((ATTN_HEADS, N_BIAS_TILES, MOBA_BLOCK, MOBA_BLOCK), lambda h, i: (h, 0, 0, 0))],
        out_specs=pl.BlockSpec((tq, width), lambda h, i: (i, h)),
        out_shape=jax.ShapeDtypeStruct((S, D_ATTN), BF16),
        scratch_shapes=[pltpu.VMEM((ATTN_HEADS, HEAD_DIM + SUM_ROWS, tq), F32),
                        pltpu.VMEM((ATTN_HEADS, GROUP_KEYS, tq), F32),
                        pltpu.VMEM((ATTN_HEADS, GROUP_KEYS, tq), F32),
                        pltpu.VMEM((ATTN_HEADS, GROUP_KEYS, tq), BF16),
                        pltpu.VMEM((ATTN_HEADS, GROUP_KEYS, tq), BF16)],
        compiler_params=_params(("parallel", "parallel")),
        name="moba_attention",
    )(rel_bias, q_t, k, v_t, km, ind, bias)


def _mix_kernel(cs_ref, at_ref, ga_ref, gb_ref, x_ref, wc_ref, wa_ref, wo_ref, o_ref):
    yc = jnp.dot(cs_ref[...], wc_ref[...], preferred_element_type=F32)
    ya = jnp.dot(at_ref[...], wa_ref[...], preferred_element_type=F32)
    merged = ga_ref[...].astype(F32) * yc + gb_ref[...].astype(F32) * ya
    o_ref[...] = x_ref[...] + jnp.dot(merged.astype(BF16), wo_ref[...], preferred_element_type=F32)


def _mix(cs, attn, gates, x, wc, wa, wo, tm=256):
    S, D = x.shape
    full = lambda shape: pl.BlockSpec(shape, lambda i: (0, 0))
    return pl.pallas_call(
        _mix_kernel,
        grid=(S // tm,),
        in_specs=[pl.BlockSpec((tm, D_CONV), lambda i: (i, 0)),
                  pl.BlockSpec((tm, D_ATTN), lambda i: (i, 0)),
                  pl.BlockSpec((tm, D), lambda i: (i, 0)),
                  pl.BlockSpec((tm, D), lambda i: (i, 1)),
                  pl.BlockSpec((tm, D), lambda i: (i, 0)),
                  full((D_CONV, D)), full((D_ATTN, D)), full((D, D))],
        out_specs=pl.BlockSpec((tm, D), lambda i: (i, 0)),
        out_shape=jax.ShapeDtypeStruct((S, D), F32),
        compiler_params=_params(("parallel",)),
        name="mix_out_proj",
    )(cs, attn, gates, gates, x, wc, wa, wo)


def _ffn_kernel(xh_ref, x_ref, g_ref, wa_ref, wb_ref, dwa_ref, dwb_ref, ba_ref, bb_ref, wd_ref,
                fg_ref, o_ref, h_ref, acc_ref, gate_ref, *, final_norm, n_tiles, up_chunk):
    i = pl.program_id(0)
    f = pl.program_id(1)
    tm = x_ref.shape[0]
    tf = wa_ref.shape[1]
    slot = f % 2

    def up_stage():
        h = h_ref[...]
        for c0 in range(0, tf, up_chunk):
            cols = slice(c0, c0 + up_chunk)

            def up_conv(w_ref, dw_ref, b_ref):
                u = jnp.dot(h, w_ref[:, cols], preferred_element_type=F32)
                y = (u * dw_ref[2:3, cols] + pltpu.roll(u, 1, 0) * dw_ref[1:2, cols]
                     + pltpu.roll(u, 2, 0) * dw_ref[0:1, cols] + b_ref[:, cols])
                return y[HALO:]

            ua = up_conv(wa_ref, dwa_ref, ba_ref)
            ub = up_conv(wb_ref, dwb_ref, bb_ref)
            gate_ref[slot, :, cols] = (jax.nn.silu(ub) * ua).astype(BF16)

    def down_product():
        return jnp.dot(gate_ref[1 - slot], wd_ref[...], preferred_element_type=F32)

    @pl.when(f == 0)
    def _():
        xh = xh_ref[...]
        ms = jnp.mean(xh * xh, axis=-1, keepdims=True)
        hh = xh * lax.rsqrt(ms + EPS) * g_ref[...]
        h_ref[0:HALO, :] = jnp.where(i > 0, hh, jnp.zeros_like(hh)).astype(BF16)
        _rmsnorm_rows(x_ref, g_ref, h_ref, tm, dst_off=HALO)
        acc_ref[...] = jnp.zeros_like(acc_ref)
        up_stage()

    @pl.when(jnp.logical_and(f > 0, f < n_tiles))
    def _():
        acc_ref[...] += down_product()
        up_stage()

    @pl.when(f == n_tiles)
    def _():
        if final_norm:
            acc_ref[...] += down_product()

            def body(c, carry):
                r = pl.multiple_of(c * NORM_ROWS, NORM_ROWS)
                xv = x_ref[pl.ds(r, NORM_ROWS), :] + acc_ref[pl.ds(r, NORM_ROWS), :]
                ms = jnp.mean(xv * xv, axis=-1, keepdims=True)
                o_ref[pl.ds(r, NORM_ROWS), :] = xv * lax.rsqrt(ms + EPS) * fg_ref[...]
                return carry
            lax.fori_loop(0, tm // NORM_ROWS, body, 0, unroll=NORM_UNROLL)
        else:
            o_ref[...] = x_ref[...] + acc_ref[...] + down_product()


def _ffn(x, g, w_up, dw, dwb, w_down, final_g, final_norm, tm=512, tf=512):
    S, D = x.shape
    nf = D_FF // tf
    per = tm // HALO
    row = lambda shape: pl.BlockSpec(shape, lambda i, f: (0, 0))
    up = lambda f: jnp.minimum(f, nf - 1)
    down = lambda f: jnp.maximum(f - 1, 0)
    return pl.pallas_call(
        functools.partial(_ffn_kernel, final_norm=final_norm, n_tiles=nf, up_chunk=tf // 2),
        grid=(S // tm, nf + 1),
        in_specs=[pl.BlockSpec((HALO, D), lambda i, f: (jnp.maximum(i * per - 1, 0), 0)),
                  pl.BlockSpec((tm, D), lambda i, f: (i, 0)),
                  row((1, D)),
                  pl.BlockSpec((D, tf), lambda i, f: (0, up(f))),
                  pl.BlockSpec((D, tf), lambda i, f: (0, up(f) + nf)),
                  pl.BlockSpec((FFN_CONV_K, tf), lambda i, f: (0, up(f))),
                  pl.BlockSpec((FFN_CONV_K, tf), lambda i, f: (0, up(f) + nf)),
                  pl.BlockSpec((1, tf), lambda i, f: (0, up(f))),
                  pl.BlockSpec((1, tf), lambda i, f: (0, up(f) + nf)),
                  pl.BlockSpec((tf, D), lambda i, f: (down(f), 0)),
                  row((1, D))],
        out_specs=pl.BlockSpec((tm, D), lambda i, f: (i, 0)),
        out_shape=jax.ShapeDtypeStruct((S, D), F32),
        scratch_shapes=[pltpu.VMEM((HALO + tm, D), BF16), pltpu.VMEM((tm, D), F32),
                        pltpu.VMEM((2, tm, tf), BF16)],
        compiler_params=_params(("parallel", "arbitrary")),
        name="conv_ffn",
    )(x, x, g, w_up, w_up, dw, dw, dwb, dwb, w_down, final_g)


def kernel(x, norm1_g, w_in, conv_dw, conv_dw_b, conv_ln_g, conv_ln_b, w_conv_out, rel_bias,
           w_attn_out, w_out, norm2_g, w_up, ffn_dw, ffn_dw_b, w_down, final_g):
    B, S, D = x.shape
    assert D == D_MODEL and S % GROUP_KEYS == 0
    depth = w_in.shape[0]
    bias = _bias_tiles(rel_bias)
    row = lambda v: v.reshape(1, -1)
    q0 = 2 * D_CONV
    v0 = q0 + 2 * D_ATTN
    outs = []
    for b in range(B):
        xb = x[b]
        for l in range(depth):
            w_l = w_in[l]
            wt_qv = jnp.concatenate([w_l[:, q0:q0 + D_ATTN], w_l[:, v0:v0 + D_ATTN]], axis=1).T.astype(BF16)
            glu, k, km, q_t, v_t, gates = _inproj(xb, row(norm1_g[l]), w_l.astype(BF16), wt_qv)
            cs = _conv_branch(glu, conv_dw[l], row(conv_dw_b[l]), row(conv_ln_g[l]), row(conv_ln_b[l]))
            attn = _attention(k, km, q_t, v_t, bias, rel_bias)
            xb = _mix(cs, attn, gates, xb, w_conv_out[l].astype(BF16), w_attn_out[l].astype(BF16),
                      w_out[l].astype(BF16))
            xb = _ffn(xb, row(norm2_g[l]), w_up[l].astype(BF16), ffn_dw[l], row(ffn_dw_b[l]),
                      w_down[l].astype(BF16), row(final_g), final_norm=(l == depth - 1))
        outs.append(xb)
    return jnp.stack(outs, axis=0)
```

```python
import functools
import math

import jax
import jax.numpy as jnp
from jax import lax
from jax.experimental import pallas as pl
from jax.experimental.pallas import tpu as pltpu

F32 = jnp.float32
BF16 = jnp.bfloat16

D_MODEL = 2048
D_CONV = 1024
CONV_K = 31
N_HEADS = 8
HEAD_DIM = 128
D_ATTN = N_HEADS * HEAD_DIM
MOBA_BLOCK = 256
MOBA_TOPK = 3
N_BUCKETS = 32
REL_MAX_DIST = 2048
D_FF = 5632
FFN_CONV_K = 3
EPS = 1e-6
NEG = -1e30
LOG2E = math.log2(math.e)

N_BIAS_TILES = 8
KV_GROUP = 4
GROUP_KEYS = KV_GROUP * MOBA_BLOCK
SUM_ROWS = 16
V_ROWS = HEAD_DIM + SUM_ROWS
ATTN_HEADS = 2

V7X_VMEM_BYTES = 64 * 1024 * 1024
VMEM_LIMIT = V7X_VMEM_BYTES - 8 * 1024 * 1024

SUBLANES = 8
NORM_ROWS = 16
NORM_UNROLL = 4
HALO = 8
CONV_HALO = 32

W_TILE = 512

NT_DIMS = (((1,), (1,)), ((), ()))


def _params(sem):
    return pltpu.CompilerParams(dimension_semantics=sem, vmem_limit_bytes=VMEM_LIMIT)


def _rmsnorm_rows(x_ref, g_ref, dst_ref, n_rows, dst_off=0, out_dtype=BF16):
    def body(c, carry):
        r = pl.multiple_of(c * NORM_ROWS, NORM_ROWS)
        xv = x_ref[pl.ds(r, NORM_ROWS), :]
        ms = jnp.mean(xv * xv, axis=-1, keepdims=True)
        y = xv * lax.rsqrt(ms + EPS) * g_ref[...]
        dst_ref[pl.ds(dst_off + r, NORM_ROWS), :] = y.astype(out_dtype)
        return carry
    lax.fori_loop(0, n_rows // NORM_ROWS, body, 0, unroll=NORM_UNROLL)


def _inproj_glu_kernel(x_ref, g_ref, wa_ref, wb_ref, o_ref, h_ref):
    @pl.when(pl.program_id(1) == 0)
    def _():
        _rmsnorm_rows(x_ref, g_ref, h_ref, x_ref.shape[0])
    h = h_ref[...]
    a = jnp.dot(h, wa_ref[0], preferred_element_type=F32)
    b = jnp.dot(h, wb_ref[0], preferred_element_type=F32)
    o_ref[...] = a * jax.nn.sigmoid(b)


def _inproj_k_kernel(x_ref, g_ref, w_ref, o_ref, km_ref, h_ref):
    @pl.when(pl.program_id(1) == 0)
    def _():
        _rmsnorm_rows(x_ref, g_ref, h_ref, x_ref.shape[0])
    acc = jnp.dot(h_ref[...], w_ref[0], preferred_element_type=F32)
    o_ref[...] = acc.astype(o_ref.dtype)
    for b in range(acc.shape[0] // MOBA_BLOCK):
        blk = acc[b * MOBA_BLOCK:(b + 1) * MOBA_BLOCK]
        km_ref[0, b:b + 1, :] = jnp.mean(blk, axis=0, keepdims=True)


def _inproj_t_kernel(x_ref, g_ref, wt_ref, q_ref, v_ref, h_ref, *, q_tiles):
    j = pl.program_id(1)

    @pl.when(j == 0)
    def _():
        _rmsnorm_rows(x_ref, g_ref, h_ref, x_ref.shape[0])
    acc = lax.dot_general(wt_ref[...], h_ref[...], NT_DIMS, preferred_element_type=F32)

    @pl.when(j < q_tiles)
    def _():
        q_ref[0] = acc.astype(q_ref.dtype)

    @pl.when(j >= q_tiles)
    def _():
        for hh in range(acc.shape[0] // HEAD_DIM):
            r = hh * V_ROWS
            v_ref[0, r:r + HEAD_DIM, :] = acc[hh * HEAD_DIM:(hh + 1) * HEAD_DIM].astype(v_ref.dtype)
            v_ref[0, r + HEAD_DIM:r + V_ROWS, :] = jnp.ones((SUM_ROWS, acc.shape[1]), v_ref.dtype)


def _inproj_gate_kernel(x_ref, g_ref, w_ref, o_ref, h_ref):
    @pl.when(pl.program_id(1) == 0)
    def _():
        _rmsnorm_rows(x_ref, g_ref, h_ref, x_ref.shape[0])
    acc = jnp.dot(h_ref[...], w_ref[0], preferred_element_type=F32)
    o_ref[...] = jax.nn.sigmoid(acc).astype(o_ref.dtype)


def _inproj(x, g, w_in, wt_qv, tn=W_TILE):
    S, D = x.shape
    tm = GROUP_KEYS
    nm = S // tm
    x_spec = pl.BlockSpec((tm, D), lambda i, j: (i, 0))
    g_spec = pl.BlockSpec((1, D), lambda i, j: (0, 0))
    scratch = [pltpu.VMEM((tm, D), BF16)]
    sem = ("parallel", "arbitrary")

    def w_spec(col0):
        off = col0 // tn
        return pl.BlockSpec((1, D, tn), lambda i, j: (j + off, 0, 0))

    glu = pl.pallas_call(
        _inproj_glu_kernel,
        grid=(nm, D_CONV // tn),
        in_specs=[x_spec, g_spec, w_spec(0), w_spec(D_CONV)],
        out_specs=pl.BlockSpec((tm, tn), lambda i, j: (i, j)),
        out_shape=jax.ShapeDtypeStruct((S, D_CONV), F32),
        scratch_shapes=scratch,
        compiler_params=_params(sem),
        name="inproj_glu",
    )(x, g, w_in, w_in)

    k, km = pl.pallas_call(
        _inproj_k_kernel,
        grid=(nm, D_ATTN // tn),
        in_specs=[x_spec, g_spec, w_spec(2 * D_CONV + D_ATTN)],
        out_specs=[pl.BlockSpec((tm, tn), lambda i, j: (i, j)),
                   pl.BlockSpec((1, tm // MOBA_BLOCK, tn), lambda i, j: (i, 0, j))],
        out_shape=[jax.ShapeDtypeStruct((S, D_ATTN), BF16),
                   jax.ShapeDtypeStruct((nm, tm // MOBA_BLOCK, D_ATTN), F32)],
        scratch_shapes=scratch,
        compiler_params=_params(sem),
        name="inproj_k",
    )(x, g, w_in)

    q_tiles = D_ATTN // tn
    v_tile_rows = tn // HEAD_DIM * V_ROWS
    q_t, v_t = pl.pallas_call(
        functools.partial(_inproj_t_kernel, q_tiles=q_tiles),
        grid=(nm, 2 * q_tiles),
        in_specs=[x_spec, g_spec, pl.BlockSpec((tn, D), lambda i, j: (j, 0))],
        out_specs=[pl.BlockSpec((1, tn, tm), lambda i, j: (i, jnp.minimum(j, q_tiles - 1), 0)),
                   pl.BlockSpec((1, v_tile_rows, tm), lambda i, j: (i, jnp.maximum(j - q_tiles, 0), 0))],
        out_shape=[jax.ShapeDtypeStruct((nm, D_ATTN, tm), BF16),
                   jax.ShapeDtypeStruct((nm, N_HEADS * V_ROWS, tm), BF16)],
        scratch_shapes=scratch,
        compiler_params=_params(sem),
        name="inproj_qv_t",
    )(x, g, wt_qv)

    gates = pl.pallas_call(
        _inproj_gate_kernel,
        grid=(nm, 2 * D_MODEL // tn),
        in_specs=[x_spec, g_spec, w_spec(2 * D_CONV + 3 * D_ATTN)],
        out_specs=pl.BlockSpec((tm, tn), lambda i, j: (i, j)),
        out_shape=jax.ShapeDtypeStruct((S, 2 * D_MODEL), BF16),
        scratch_shapes=scratch,
        compiler_params=_params(sem),
        name="inproj_gates",
    )(x, g, w_in)
    return glu, k, km.reshape(S // MOBA_BLOCK, D_ATTN), q_t, v_t, gates


def _conv_kernel(halo_ref, x_ref, w_ref, b_ref, lg_ref, lb_ref, o_ref, win_ref, y_ref,
                 *, rows, lanes):
    tm, C = x_ref.shape
    i = pl.program_id(0)
    halo = halo_ref[...]
    win_ref[0:CONV_HALO, :] = jnp.where(i > 0, halo, jnp.zeros_like(halo))
    win_ref[CONV_HALO:, :] = x_ref[...]
    first = CONV_HALO - (CONV_K - 1)

    for c0 in range(0, C, lanes):
        def body(rc, carry):
            r = pl.multiple_of(rc * rows, rows)
            acc = jnp.zeros((rows, lanes), F32) + b_ref[:, c0:c0 + lanes]
            win = win_ref[pl.ds(r, rows + CONV_HALO), c0:c0 + lanes]
            for sub in range(SUBLANES):
                shifted = win if sub == 0 else pltpu.roll(win, win.shape[0] - sub, 0)
                usable = win.shape[0] - (SUBLANES if sub else 0)
                for base in range(0, usable - rows + 1, SUBLANES):
                    k = base + sub - first
                    if 0 <= k < CONV_K:
                        acc = acc + shifted[base:base + rows] * w_ref[k:k + 1, c0:c0 + lanes]
            y_ref[pl.ds(r, rows), c0:c0 + lanes] = acc
            return carry
        lax.fori_loop(0, tm // rows, body, 0)

    def ln_body(rc, carry):
        r = pl.multiple_of(rc * NORM_ROWS, NORM_ROWS)
        v = y_ref[pl.ds(r, NORM_ROWS), :]
        mu = jnp.mean(v, axis=-1, keepdims=True)
        var = jnp.mean(jnp.square(v - mu), axis=-1, keepdims=True)
        z = (v - mu) * lax.rsqrt(var + EPS) * lg_ref[...] + lb_ref[...]
        o_ref[pl.ds(r, NORM_ROWS), :] = jax.nn.silu(z).astype(o_ref.dtype)
        return carry
    lax.fori_loop(0, tm // NORM_ROWS, ln_body, 0, unroll=NORM_UNROLL)


def _conv_branch(glu, w, b, lg, lb, tm=256):
    S, C = glu.shape
    per = tm // CONV_HALO
    return pl.pallas_call(
        functools.partial(_conv_kernel, rows=32, lanes=256),
        grid=(S // tm,),
        in_specs=[pl.BlockSpec((CONV_HALO, C), lambda i: (jnp.maximum(i * per - 1, 0), 0)),
                  pl.BlockSpec((tm, C), lambda i: (i, 0)),
                  pl.BlockSpec((CONV_K, C), lambda i: (0, 0)),
                  pl.BlockSpec((1, C), lambda i: (0, 0)),
                  pl.BlockSpec((1, C), lambda i: (0, 0)),
                  pl.BlockSpec((1, C), lambda i: (0, 0))],
        out_specs=pl.BlockSpec((tm, C), lambda i: (i, 0)),
        out_shape=jax.ShapeDtypeStruct((S, C), BF16),
        scratch_shapes=[pltpu.VMEM((tm + CONV_HALO, C), F32), pltpu.VMEM((tm, C), F32)],
        compiler_params=_params(("parallel",)),
        name="conv_branch",
    )(glu, glu, w, b, lg, lb)


def _bias_kernel(rb_ref, o_ref):
    h = pl.program_id(0)
    delta = pl.program_id(1)
    shape = (MOBA_BLOCK, MOBA_BLOCK)
    d = delta * MOBA_BLOCK + lax.broadcasted_iota(jnp.int32, shape, 1) - lax.broadcasted_iota(jnp.int32, shape, 0)
    n = jnp.maximum(d, 0)
    max_exact = N_BUCKETS // 2
    nf = jnp.maximum(n, 1).astype(F32)
    large = max_exact + (jnp.log(nf / max_exact) / math.log(REL_MAX_DIST / max_exact)
                         * (N_BUCKETS - max_exact)).astype(jnp.int32)
    large = jnp.minimum(large, N_BUCKETS - 1)
    bucket = jnp.where(n < max_exact, n, large)
    val = jnp.zeros(shape, F32)
    for b in range(N_BUCKETS):
        val = jnp.where(bucket == b, rb_ref[b, h], val)
    val = jnp.where(delta == N_BIAS_TILES - 1, 0.0, val * LOG2E)
    o_ref[0, 0] = jnp.where(d >= 0, val, NEG)


def _bias_tiles(rel_bias):
    return pl.pallas_call(
        _bias_kernel,
        grid=(N_HEADS, N_BIAS_TILES),
        in_specs=[pl.BlockSpec(memory_space=pltpu.SMEM)],
        out_specs=pl.BlockSpec((1, 1, MOBA_BLOCK, MOBA_BLOCK), lambda h, d: (h, d, 0, 0)),
        out_shape=jax.ShapeDtypeStruct((N_HEADS, N_BIAS_TILES, MOBA_BLOCK, MOBA_BLOCK), F32),
        compiler_params=_params(("parallel", "parallel")),
        name="t5_bias_tiles",
    )(rel_bias)


def _attn_kernel(rb_ref, qt_ref, k_ref, vt_ref, km_ref, ind_ref, bias_ref, o_ref,
                 acc_ref, sa_ref, sb_ref, pa_ref, pb_ref):
    i = pl.program_id(1)
    nb = km_ref.shape[0]
    tq = qt_ref.shape[2]
    scale = HEAD_DIM ** -0.5
    heads = range(ATTN_HEADS)
    lanes = [slice(hh * HEAD_DIM, (hh + 1) * HEAD_DIM) for hh in heads]

    def widened_query(hh):
        qf = qt_ref[0, lanes[hh], :].astype(F32)
        gate = jnp.dot(km_ref[:, lanes[hh]], qf, precision=lax.Precision.HIGHEST, preferred_element_type=F32)
        row = lax.broadcasted_iota(jnp.int32, (nb, tq), 0)
        valid = row < i
        g = jnp.where(valid, gate, NEG)
        sel = jnp.zeros((nb, tq), jnp.bool_)
        for _ in range(MOBA_TOPK):
            top = jnp.max(g, axis=0, keepdims=True)
            idx = jnp.min(jnp.where(g == top, row, nb), axis=0, keepdims=True)
            hit = row == idx
            sel = jnp.logical_or(sel, hit)
            g = jnp.where(hit, -jnp.inf, g)
        picked = jnp.logical_and(sel, valid)
        far = jnp.logical_and(picked, row <= i - (N_BIAS_TILES - 1))
        far_bias = rb_ref[N_BUCKETS - 1, pl.program_id(0) * ATTN_HEADS + hh] * LOG2E
        pen = jnp.where(row == i, 0.0, jnp.where(picked, jnp.where(far, far_bias, 0.0), NEG))
        pen_hi = pen.astype(BF16)
        pen_lo = jnp.where(far, pen - pen_hi.astype(F32), 0.0).astype(BF16)
        return jnp.concatenate(
            [(qf * (scale * LOG2E)).astype(BF16), pen_hi, pen_lo, jnp.zeros((HEAD_DIM - 2 * nb, tq), BF16)],
            axis=0)

    q_aug = [widened_query(hh) for hh in heads]
    n_groups = i // KV_GROUP + 1

    def scores(grp, dst_ref):
        r0 = pl.multiple_of(grp * GROUP_KEYS, GROUP_KEYS)
        ind = ind_ref[pl.ds(r0, GROUP_KEYS), :]
        for hh in heads:
            k_aug = jnp.concatenate([k_ref[pl.ds(r0, GROUP_KEYS), lanes[hh]], ind], axis=1)
            dst_ref[hh] = jnp.dot(k_aug, q_aug[hh], preferred_element_type=F32)

    def softmax(grp, src_ref, dst_ref, m_prev):
        m_out, alpha_out = [], []
        for hh in heads:
            parts = []
            for u in range(KV_GROUP):
                delta = jnp.clip(i - (grp * KV_GROUP + u), 0, N_BIAS_TILES - 1)
                parts.append(src_ref[hh, u * MOBA_BLOCK:(u + 1) * MOBA_BLOCK, :] + bias_ref[hh, delta])
            s = jnp.concatenate(parts, axis=0)
            m_new = jnp.maximum(m_prev[hh], jnp.max(s, axis=0, keepdims=True))
            dst_ref[hh] = jnp.exp2(s - m_new).astype(BF16)
            m_out.append(m_new)
            alpha_out.append(jnp.exp2(m_prev[hh] - m_new))
        return tuple(m_out), tuple(alpha_out)

    def weighted_values(grp, live, src_ref, alpha):
        for hh in heads:
            pv = jnp.dot(vt_ref[grp, hh * V_ROWS:(hh + 1) * V_ROWS, :], src_ref[hh], preferred_element_type=F32)
            acc_ref[hh] = alpha[hh] * acc_ref[hh] + jnp.where(live, pv, 0.0)

    acc_ref[...] = jnp.zeros_like(acc_ref)
    pb_ref[...] = jnp.zeros_like(pb_ref)
    scores(n_groups - 1, sa_ref)

    def body(u, carry):
        m_prev, alpha_b, grp_b_prev = carry
        grp_a = n_groups - 1 - 2 * u
        grp_b = grp_a - 1
        weighted_values(jnp.maximum(grp_b_prev, 0), grp_b_prev >= 0, pb_ref, alpha_b)
        m_a, alpha_a = softmax(grp_a, sa_ref, pa_ref, m_prev)
        scores(jnp.maximum(grp_b, 0), sb_ref)
        weighted_values(grp_a, True, pa_ref, alpha_a)
        m_b, alpha_b = softmax(jnp.maximum(grp_b, 0), sb_ref, pb_ref, m_a)
        scores(jnp.maximum(grp_b - 1, 0), sa_ref)
        return m_b, alpha_b, grp_b

    m0 = tuple(jnp.full((1, tq), NEG, F32) for _ in heads)
    a0 = tuple(jnp.ones((1, tq), F32) for _ in heads)
    _, alpha_b, grp_b = lax.fori_loop(0, (n_groups + 1) // 2, body, (m0, a0, jnp.int32(-1)))
    weighted_values(jnp.maximum(grp_b, 0), grp_b >= 0, pb_ref, alpha_b)
    for hh in heads:
        acc = acc_ref[hh]
        o_ref[:, lanes[hh]] = (acc[:HEAD_DIM] / acc[HEAD_DIM:HEAD_DIM + 1]).T.astype(o_ref.dtype)


def _attention(k, km, q_t, v_t, bias, rel_bias):
    S = k.shape[0]
    nb = S // MOBA_BLOCK
    tq = MOBA_BLOCK
    per = GROUP_KEYS // tq
    width = ATTN_HEADS * HEAD_DIM
    key_blk = jnp.arange(S, dtype=jnp.int32)[:, None] // MOBA_BLOCK
    lane = jnp.arange(HEAD_DIM, dtype=jnp.int32)[None, :]
    ind = jnp.logical_and(lane < 2 * nb, lane % nb == key_blk).astype(BF16)
    return pl.pallas_call(
        _attn_kernel,
        grid=(N_HEADS // ATTN_HEADS, S // tq),
        in_specs=[pl.BlockSpec(memory_space=pltpu.SMEM),
                  pl.BlockSpec((1, width, tq), lambda h, i: (i // per, h, i % per)),
                  pl.BlockSpec((S, width), lambda h, i: (0, h)),
                  pl.BlockSpec((S // GROUP_KEYS, ATTN_HEADS * V_ROWS, GROUP_KEYS), lambda h, i: (0, h, 0)),
                  pl.BlockSpec((nb, width), lambda h, i: (0, h)),
                  pl.BlockSpec((S, HEAD_DIM), lambda h, i: (0, 0)),
                  pl.BlockSpec((ATTN_HEADS, N_BIAS_TILES, MOBA_BLOCK, MOBA_BLOCK), lambda h, i: (h, 0, 0, 0))],
        out_specs=pl.BlockSpec((tq, width), lambda h, i: (i, h)),
        out_shape=jax.ShapeDtypeStruct((S, D_ATTN), BF16),
        scratch_shapes=[pltpu.VMEM((ATTN_HEADS, HEAD_DIM + SUM_ROWS, tq), F32),
                        pltpu.VMEM((ATTN_HEADS, GROUP_KEYS, tq), F32),
                        pltpu.VMEM((ATTN_HEADS, GROUP_KEYS, tq), F32),
                        pltpu.VMEM((ATTN_HEADS, GROUP_KEYS, tq), BF16),
                        pltpu.VMEM((ATTN_HEADS, GROUP_KEYS, tq), BF16)],
        compiler_params=_params(("parallel", "parallel")),
        name="moba_attention",
    )(rel_bias, q_t, k, v_t, km, ind, bias)


def _mix_kernel(cs_ref, at_ref, ga_ref, gb_ref, x_ref, wc_ref, wa_ref, wo_ref, o_ref):
    yc = jnp.dot(cs_ref[...], wc_ref[...], preferred_element_type=F32)
    ya = jnp.dot(at_ref[...], wa_ref[...], preferred_element_type=F32)
    merged = ga_ref[...].astype(F32) * yc + gb_ref[...].astype(F32) * ya
    o_ref[...] = x_ref[...] + jnp.dot(merged.astype(BF16), wo_ref[...], preferred_element_type=F32)


def _mix(cs, attn, gates, x, wc, wa, wo, tm=256):
    S, D = x.shape
    full = lambda shape: pl.BlockSpec(shape, lambda i: (0, 0))
    return pl.pallas_call(
        _mix_kernel,
        grid=(S // tm,),
        in_specs=[pl.BlockSpec((tm, D_CONV), lambda i: (i, 0)),
                  pl.BlockSpec((tm, D_ATTN), lambda i: (i, 0)),
                  pl.BlockSpec((tm, D), lambda i: (i, 0)),
                  pl.BlockSpec((tm, D), lambda i: (i, 1)),
                  pl.BlockSpec((tm, D), lambda i: (i, 0)),
                  full((D_CONV, D)), full((D_ATTN, D)), full((D, D))],
        out_specs=pl.BlockSpec((tm, D), lambda i: (i, 0)),
        out_shape=jax.ShapeDtypeStruct((S, D), F32),
        compiler_params=_params(("parallel",)),
        name="mix_out_proj",
    )(cs, attn, gates, gates, x, wc, wa, wo)


def _ffn_kernel(xh_ref, x_ref, g_ref, wa_ref, wb_ref, dwa_ref, dwb_ref, ba_ref, bb_ref, wd_ref,
                fg_ref, o_ref, h_ref, acc_ref, gate_ref, *, final_norm, n_tiles, up_chunk):
    i = pl.program_id(0)
    f = pl.program_id(1)
    tm = x_ref.shape[0]
    tf = wa_ref.shape[2]
    slot = f % 2

    def up_stage():
        h = h_ref[...]
        for c0 in range(0, tf, up_chunk):
            cols = slice(c0, c0 + up_chunk)

            def up_conv(w_ref, dw_ref, b_ref):
                u = jnp.dot(h, w_ref[0, :, cols], preferred_element_type=F32)
                y = (u * dw_ref[2:3, cols] + pltpu.roll(u, 1, 0) * dw_ref[1:2, cols]
                     + pltpu.roll(u, 2, 0) * dw_ref[0:1, cols] + b_ref[:, cols])
                return y[HALO:]

            ua = up_conv(wa_ref, dwa_ref, ba_ref)
            ub = up_conv(wb_ref, dwb_ref, bb_ref)
            gate_ref[slot, :, cols] = (jax.nn.silu(ub) * ua).astype(BF16)

    def down_product():
        return jnp.dot(gate_ref[1 - slot], wd_ref[...], preferred_element_type=F32)

    @pl.when(f == 0)
    def _():
        xh = xh_ref[...]
        ms = jnp.mean(xh * xh, axis=-1, keepdims=True)
        hh = xh * lax.rsqrt(ms + EPS) * g_ref[...]
        h_ref[0:HALO, :] = jnp.where(i > 0, hh, jnp.zeros_like(hh)).astype(BF16)
        _rmsnorm_rows(x_ref, g_ref, h_ref, tm, dst_off=HALO)
        acc_ref[...] = jnp.zeros_like(acc_ref)
        up_stage()

    @pl.when(jnp.logical_and(f > 0, f < n_tiles))
    def _():
        acc_ref[...] += down_product()
        up_stage()

    @pl.when(f == n_tiles)
    def _():
        if final_norm:
            acc_ref[...] += down_product()

            def body(c, carry):
                r = pl.multiple_of(c * NORM_ROWS, NORM_ROWS)
                xv = x_ref[pl.ds(r, NORM_ROWS), :] + acc_ref[pl.ds(r, NORM_ROWS), :]
                ms = jnp.mean(xv * xv, axis=-1, keepdims=True)
                o_ref[pl.ds(r, NORM_ROWS), :] = xv * lax.rsqrt(ms + EPS) * fg_ref[...]
                return carry
            lax.fori_loop(0, tm // NORM_ROWS, body, 0, unroll=NORM_UNROLL)
        else:
            o_ref[...] = x_ref[...] + acc_ref[...] + down_product()


def _ffn(x, g, w_up, dw, dwb, w_down, final_g, final_norm, tm=512, tf=W_TILE):
    S, D = x.shape
    nf = D_FF // tf
    per = tm // HALO
    row = lambda shape: pl.BlockSpec(shape, lambda i, f: (0, 0))
    up = lambda f: jnp.minimum(f, nf - 1)
    down = lambda f: jnp.maximum(f - 1, 0)
    return pl.pallas_call(
        functools.partial(_ffn_kernel, final_norm=final_norm, n_tiles=nf, up_chunk=tf // 2),
        grid=(S // tm, nf + 1),
        in_specs=[pl.BlockSpec((HALO, D), lambda i, f: (jnp.maximum(i * per - 1, 0), 0)),
                  pl.BlockSpec((tm, D), lambda i, f: (i, 0)),
                  row((1, D)),
                  pl.BlockSpec((1, D, tf), lambda i, f: (up(f), 0, 0)),
                  pl.BlockSpec((1, D, tf), lambda i, f: (up(f) + nf, 0, 0)),
                  pl.BlockSpec((FFN_CONV_K, tf), lambda i, f: (0, up(f))),
                  pl.BlockSpec((FFN_CONV_K, tf), lambda i, f: (0, up(f) + nf)),
                  pl.BlockSpec((1, tf), lambda i, f: (0, up(f))),
                  pl.BlockSpec((1, tf), lambda i, f: (0, up(f) + nf)),
                  pl.BlockSpec((tf, D), lambda i, f: (down(f), 0)),
                  row((1, D))],
        out_specs=pl.BlockSpec((tm, D), lambda i, f: (i, 0)),
        out_shape=jax.ShapeDtypeStruct((S, D), F32),
        scratch_shapes=[pltpu.VMEM((HALO + tm, D), BF16), pltpu.VMEM((tm, D), F32),
                        pltpu.VMEM((2, tm, tf), BF16)],
        compiler_params=_params(("parallel", "arbitrary")),
        name="conv_ffn",
    )(x, x, g, w_up, w_up, dw, dw, dwb, dwb, w_down, final_g)


def _column_tiles(w, tn):
    K, N = w.shape
    return w.reshape(K, N // tn, tn).transpose(1, 0, 2).astype(BF16)


def kernel(x, norm1_g, w_in, conv_dw, conv_dw_b, conv_ln_g, conv_ln_b, w_conv_out, rel_bias,
           w_attn_out, w_out, norm2_g, w_up, ffn_dw, ffn_dw_b, w_down, final_g):
    B, S, D = x.shape
    assert D == D_MODEL and S % GROUP_KEYS == 0
    depth = w_in.shape[0]
    bias = _bias_tiles(rel_bias)
    row = lambda v: v.reshape(1, -1)
    q0 = 2 * D_CONV
    v0 = q0 + 2 * D_ATTN
    outs = []
    for b in range(B):
        xb = x[b]
        for l in range(depth):
            w_l = w_in[l]
            wt_qv = jnp.concatenate([w_l[:, q0:q0 + D_ATTN], w_l[:, v0:v0 + D_ATTN]], axis=1).T.astype(BF16)
            glu, k, km, q_t, v_t, gates = _inproj(xb, row(norm1_g[l]), _column_tiles(w_l, W_TILE), wt_qv)
            cs = _conv_branch(glu, conv_dw[l], row(conv_dw_b[l]), row(conv_ln_g[l]), row(conv_ln_b[l]))
            attn = _attention(k, km, q_t, v_t, bias, rel_bias)
            xb = _mix(cs, attn, gates, xb, w_conv_out[l].astype(BF16), w_attn_out[l].astype(BF16),
                      w_out[l].astype(BF16))
            xb = _ffn(xb, row(norm2_g[l]), _column_tiles(w_up[l], W_TILE), ffn_dw[l], row(ffn_dw_b[l]),
                      w_down[l].astype(BF16), row(final_g), final_norm=(l == depth - 1))
        outs.append(xb)
    return jnp.stack(outs, axis=0)
```

```python
import functools
import math

import jax
import jax.numpy as jnp
from jax import lax
from jax.experimental import pallas as pl
from jax.experimental.pallas import tpu as pltpu

F32 = jnp.float32
BF16 = jnp.bfloat16

D_MODEL = 2048
D_CONV = 1024
CONV_K = 31
N_HEADS = 8
HEAD_DIM = 128
D_ATTN = N_HEADS * HEAD_DIM
MOBA_BLOCK = 256
MOBA_TOPK = 3
N_BUCKETS = 32
REL_MAX_DIST = 2048
D_FF = 5632
FFN_CONV_K = 3
EPS = 1e-6
NEG = -1e30
LOG2E = math.log2(math.e)

N_BIAS_TILES = 8
KV_GROUP = 4
GROUP_KEYS = KV_GROUP * MOBA_BLOCK
SUM_ROWS = 16
V_ROWS = HEAD_DIM + SUM_ROWS
ATTN_HEADS = 2

V7X_VMEM_BYTES = 64 * 1024 * 1024
VMEM_LIMIT = V7X_VMEM_BYTES - 8 * 1024 * 1024

SUBLANES = 8
NORM_ROWS = 16
NORM_UNROLL = 4
HALO = 8
CONV_HALO = 32

W_TILE = 512

NT_DIMS = (((1,), (1,)), ((), ()))


def _params(sem):
    return pltpu.CompilerParams(dimension_semantics=sem, vmem_limit_bytes=VMEM_LIMIT)


def _rmsnorm_rows(x_ref, g_ref, dst_ref, n_rows, dst_off=0, out_dtype=BF16):
    def body(c, carry):
        r = pl.multiple_of(c * NORM_ROWS, NORM_ROWS)
        xv = x_ref[pl.ds(r, NORM_ROWS), :]
        ms = jnp.mean(xv * xv, axis=-1, keepdims=True)
        y = xv * lax.rsqrt(ms + EPS) * g_ref[...]
        dst_ref[pl.ds(dst_off + r, NORM_ROWS), :] = y.astype(out_dtype)
        return carry
    lax.fori_loop(0, n_rows // NORM_ROWS, body, 0, unroll=NORM_UNROLL)


def _inproj_glu_kernel(x_ref, g_ref, wa_ref, wb_ref, o_ref, h_ref):
    @pl.when(pl.program_id(1) == 0)
    def _():
        _rmsnorm_rows(x_ref, g_ref, h_ref, x_ref.shape[0])
    h = h_ref[...]
    a = jnp.dot(h, wa_ref[...], preferred_element_type=F32)
    b = jnp.dot(h, wb_ref[...], preferred_element_type=F32)
    o_ref[...] = a * jax.nn.sigmoid(b)


def _inproj_k_kernel(x_ref, g_ref, w_ref, o_ref, km_ref, h_ref):
    @pl.when(pl.program_id(1) == 0)
    def _():
        _rmsnorm_rows(x_ref, g_ref, h_ref, x_ref.shape[0])
    acc = jnp.dot(h_ref[...], w_ref[...], preferred_element_type=F32)
    o_ref[...] = acc.astype(o_ref.dtype)
    for b in range(acc.shape[0] // MOBA_BLOCK):
        blk = acc[b * MOBA_BLOCK:(b + 1) * MOBA_BLOCK]
        km_ref[0, b:b + 1, :] = jnp.mean(blk, axis=0, keepdims=True)


def _inproj_t_kernel(x_ref, g_ref, wt_ref, q_ref, v_ref, h_ref, *, q_tiles):
    j = pl.program_id(1)

    @pl.when(j == 0)
    def _():
        _rmsnorm_rows(x_ref, g_ref, h_ref, x_ref.shape[0])
    acc = lax.dot_general(wt_ref[...], h_ref[...], NT_DIMS, preferred_element_type=F32)

    @pl.when(j < q_tiles)
    def _():
        q_ref[0] = acc.astype(q_ref.dtype)

    @pl.when(j >= q_tiles)
    def _():
        for hh in range(acc.shape[0] // HEAD_DIM):
            r = hh * V_ROWS
            v_ref[0, r:r + HEAD_DIM, :] = acc[hh * HEAD_DIM:(hh + 1) * HEAD_DIM].astype(v_ref.dtype)
            v_ref[0, r + HEAD_DIM:r + V_ROWS, :] = jnp.ones((SUM_ROWS, acc.shape[1]), v_ref.dtype)


def _inproj_gate_kernel(x_ref, g_ref, w_ref, o_ref, h_ref):
    @pl.when(pl.program_id(1) == 0)
    def _():
        _rmsnorm_rows(x_ref, g_ref, h_ref, x_ref.shape[0])
    acc = jnp.dot(h_ref[...], w_ref[...], preferred_element_type=F32)
    o_ref[...] = jax.nn.sigmoid(acc).astype(o_ref.dtype)


def _inproj(x, g, w_in, wt_qv, tn=W_TILE):
    S, D = x.shape
    tm = GROUP_KEYS
    nm = S // tm
    x_spec = pl.BlockSpec((tm, D), lambda i, j: (i, 0))
    g_spec = pl.BlockSpec((1, D), lambda i, j: (0, 0))
    scratch = [pltpu.VMEM((tm, D), BF16)]
    sem = ("parallel", "arbitrary")

    def w_spec(col0):
        off = col0 // tn
        return pl.BlockSpec((D, tn), lambda i, j: (0, j + off))

    glu = pl.pallas_call(
        _inproj_glu_kernel,
        grid=(nm, D_CONV // tn),
        in_specs=[x_spec, g_spec, w_spec(0), w_spec(D_CONV)],
        out_specs=pl.BlockSpec((tm, tn), lambda i, j: (i, j)),
        out_shape=jax.ShapeDtypeStruct((S, D_CONV), F32),
        scratch_shapes=scratch,
        compiler_params=_params(sem),
        name="inproj_glu",
    )(x, g, w_in, w_in)

    k, km = pl.pallas_call(
        _inproj_k_kernel,
        grid=(nm, D_ATTN // tn),
        in_specs=[x_spec, g_spec, w_spec(2 * D_CONV + D_ATTN)],
        out_specs=[pl.BlockSpec((tm, tn), lambda i, j: (i, j)),
                   pl.BlockSpec((1, tm // MOBA_BLOCK, tn), lambda i, j: (i, 0, j))],
        out_shape=[jax.ShapeDtypeStruct((S, D_ATTN), BF16),
                   jax.ShapeDtypeStruct((nm, tm // MOBA_BLOCK, D_ATTN), F32)],
        scratch_shapes=scratch,
        compiler_params=_params(sem),
        name="inproj_k",
    )(x, g, w_in)

    q_tiles = D_ATTN // tn
    v_tile_rows = tn // HEAD_DIM * V_ROWS
    q_t, v_t = pl.pallas_call(
        functools.partial(_inproj_t_kernel, q_tiles=q_tiles),
        grid=(nm, 2 * q_tiles),
        in_specs=[x_spec, g_spec, pl.BlockSpec((tn, D), lambda i, j: (j, 0))],
        out_specs=[pl.BlockSpec((1, tn, tm), lambda i, j: (i, jnp.minimum(j, q_tiles - 1), 0)),
                   pl.BlockSpec((1, v_tile_rows, tm), lambda i, j: (i, jnp.maximum(j - q_tiles, 0), 0))],
        out_shape=[jax.ShapeDtypeStruct((nm, D_ATTN, tm), BF16),
                   jax.ShapeDtypeStruct((nm, N_HEADS * V_ROWS, tm), BF16)],
        scratch_shapes=scratch,
        compiler_params=_params(sem),
        name="inproj_qv_t",
    )(x, g, wt_qv)

    gates = pl.pallas_call(
        _inproj_gate_kernel,
        grid=(nm, 2 * D_MODEL // tn),
        in_specs=[x_spec, g_spec, w_spec(2 * D_CONV + 3 * D_ATTN)],
        out_specs=pl.BlockSpec((tm, tn), lambda i, j: (i, j)),
        out_shape=jax.ShapeDtypeStruct((S, 2 * D_MODEL), BF16),
        scratch_shapes=scratch,
        compiler_params=_params(sem),
        name="inproj_gates",
    )(x, g, w_in)
    return glu, k, km.reshape(S // MOBA_BLOCK, D_ATTN), q_t, v_t, gates


def _conv_kernel(halo_ref, x_ref, w_ref, b_ref, lg_ref, lb_ref, o_ref, win_ref, y_ref,
                 *, rows, lanes):
    tm, C = x_ref.shape
    i = pl.program_id(0)
    halo = halo_ref[...]
    win_ref[0:CONV_HALO, :] = jnp.where(i > 0, halo, jnp.zeros_like(halo))
    win_ref[CONV_HALO:, :] = x_ref[...]
    first = CONV_HALO - (CONV_K - 1)

    for c0 in range(0, C, lanes):
        def body(rc, carry):
            r = pl.multiple_of(rc * rows, rows)
            acc = jnp.zeros((rows, lanes), F32) + b_ref[:, c0:c0 + lanes]
            win = win_ref[pl.ds(r, rows + CONV_HALO), c0:c0 + lanes]
            for sub in range(SUBLANES):
                shifted = win if sub == 0 else pltpu.roll(win, win.shape[0] - sub, 0)
                usable = win.shape[0] - (SUBLANES if sub else 0)
                for base in range(0, usable - rows + 1, SUBLANES):
                    k = base + sub - first
                    if 0 <= k < CONV_K:
                        acc = acc + shifted[base:base + rows] * w_ref[k:k + 1, c0:c0 + lanes]
            y_ref[pl.ds(r, rows), c0:c0 + lanes] = acc
            return carry
        lax.fori_loop(0, tm // rows, body, 0)

    def ln_body(rc, carry):
        r = pl.multiple_of(rc * NORM_ROWS, NORM_ROWS)
        v = y_ref[pl.ds(r, NORM_ROWS), :]
        mu = jnp.mean(v, axis=-1, keepdims=True)
        var = jnp.mean(jnp.square(v - mu), axis=-1, keepdims=True)
        z = (v - mu) * lax.rsqrt(var + EPS) * lg_ref[...] + lb_ref[...]
        o_ref[pl.ds(r, NORM_ROWS), :] = jax.nn.silu(z).astype(o_ref.dtype)
        return carry
    lax.fori_loop(0, tm // NORM_ROWS, ln_body, 0, unroll=NORM_UNROLL)


def _conv_branch(glu, w, b, lg, lb, tm=256):
    S, C = glu.shape
    per = tm // CONV_HALO
    return pl.pallas_call(
        functools.partial(_conv_kernel, rows=32, lanes=256),
        grid=(S // tm,),
        in_specs=[pl.BlockSpec((CONV_HALO, C), lambda i: (jnp.maximum(i * per - 1, 0), 0)),
                  pl.BlockSpec((tm, C), lambda i: (i, 0)),
                  pl.BlockSpec((CONV_K, C), lambda i: (0, 0)),
                  pl.BlockSpec((1, C), lambda i: (0, 0)),
                  pl.BlockSpec((1, C), lambda i: (0, 0)),
                  pl.BlockSpec((1, C), lambda i: (0, 0))],
        out_specs=pl.BlockSpec((tm, C), lambda i: (i, 0)),
        out_shape=jax.ShapeDtypeStruct((S, C), BF16),
        scratch_shapes=[pltpu.VMEM((tm + CONV_HALO, C), F32), pltpu.VMEM((tm, C), F32)],
        compiler_params=_params(("parallel",)),
        name="conv_branch",
    )(glu, glu, w, b, lg, lb)


def _bias_kernel(rb_ref, o_ref):
    h = pl.program_id(0)
    delta = pl.program_id(1)
    shape = (MOBA_BLOCK, MOBA_BLOCK)
    d = delta * MOBA_BLOCK + lax.broadcasted_iota(jnp.int32, shape, 1) - lax.broadcasted_iota(jnp.int32, shape, 0)
    n = jnp.maximum(d, 0)
    max_exact = N_BUCKETS // 2
    nf = jnp.maximum(n, 1).astype(F32)
    large = max_exact + (jnp.log(nf / max_exact) / math.log(REL_MAX_DIST / max_exact)
                         * (N_BUCKETS - max_exact)).astype(jnp.int32)
    large = jnp.minimum(large, N_BUCKETS - 1)
    bucket = jnp.where(n < max_exact, n, large)
    val = jnp.zeros(shape, F32)
    for b in range(N_BUCKETS):
        val = jnp.where(bucket == b, rb_ref[b, h], val)
    val = jnp.where(delta == N_BIAS_TILES - 1, 0.0, val * LOG2E)
    o_ref[0, 0] = jnp.where(d >= 0, val, NEG)


def _bias_tiles(rel_bias):
    return pl.pallas_call(
        _bias_kernel,
        grid=(N_HEADS, N_BIAS_TILES),
        in_specs=[pl.BlockSpec(memory_space=pltpu.SMEM)],
        out_specs=pl.BlockSpec((1, 1, MOBA_BLOCK, MOBA_BLOCK), lambda h, d: (h, d, 0, 0)),
        out_shape=jax.ShapeDtypeStruct((N_HEADS, N_BIAS_TILES, MOBA_BLOCK, MOBA_BLOCK), F32),
        compiler_params=_params(("parallel", "parallel")),
        name="t5_bias_tiles",
    )(rel_bias)


def _attn_kernel(rb_ref, qt_ref, k_ref, vt_ref, km_ref, ind_ref, bias_ref, o_ref,
                 acc_ref, sa_ref, sb_ref, pa_ref, pb_ref):
    i = pl.program_id(1)
    nb = km_ref.shape[0]
    tq = qt_ref.shape[2]
    scale = HEAD_DIM ** -0.5
    heads = range(ATTN_HEADS)
    lanes = [slice(hh * HEAD_DIM, (hh + 1) * HEAD_DIM) for hh in heads]

    def widened_query(hh):
        qf = qt_ref[0, lanes[hh], :].astype(F32)
        gate = jnp.dot(km_ref[:, lanes[hh]], qf, precision=lax.Precision.HIGHEST, preferred_element_type=F32)
        row = lax.broadcasted_iota(jnp.int32, (nb, tq), 0)
        valid = row < i
        g = jnp.where(valid, gate, NEG)
        sel = jnp.zeros((nb, tq), jnp.bool_)
        for _ in range(MOBA_TOPK):
            top = jnp.max(g, axis=0, keepdims=True)
            idx = jnp.min(jnp.where(g == top, row, nb), axis=0, keepdims=True)
            hit = row == idx
            sel = jnp.logical_or(sel, hit)
            g = jnp.where(hit, -jnp.inf, g)
        picked = jnp.logical_and(sel, valid)
        far = jnp.logical_and(picked, row <= i - (N_BIAS_TILES - 1))
        far_bias = rb_ref[N_BUCKETS - 1, pl.program_id(0) * ATTN_HEADS + hh] * LOG2E
        pen = jnp.where(row == i, 0.0, jnp.where(picked, jnp.where(far, far_bias, 0.0), NEG))
        pen_hi = pen.astype(BF16)
        pen_lo = jnp.where(far, pen - pen_hi.astype(F32), 0.0).astype(BF16)
        return jnp.concatenate(
            [(qf * (scale * LOG2E)).astype(BF16), pen_hi, pen_lo, jnp.zeros((HEAD_DIM - 2 * nb, tq), BF16)],
            axis=0)

    q_aug = [widened_query(hh) for hh in heads]
    n_groups = i // KV_GROUP + 1

    def scores(grp, dst_ref):
        r0 = pl.multiple_of(grp * GROUP_KEYS, GROUP_KEYS)
        ind = ind_ref[pl.ds(r0, GROUP_KEYS), :]
        for hh in heads:
            k_aug = jnp.concatenate([k_ref[pl.ds(r0, GROUP_KEYS), lanes[hh]], ind], axis=1)
            dst_ref[hh] = jnp.dot(k_aug, q_aug[hh], preferred_element_type=F32)

    def softmax(grp, src_ref, dst_ref, m_prev):
        m_out, alpha_out = [], []
        for hh in heads:
            parts = []
            for u in range(KV_GROUP):
                delta = jnp.clip(i - (grp * KV_GROUP + u), 0, N_BIAS_TILES - 1)
                parts.append(src_ref[hh, u * MOBA_BLOCK:(u + 1) * MOBA_BLOCK, :] + bias_ref[hh, delta])
            s = jnp.concatenate(parts, axis=0)
            m_new = jnp.maximum(m_prev[hh], jnp.max(s, axis=0, keepdims=True))
            dst_ref[hh] = jnp.exp2(s - m_new).astype(BF16)
            m_out.append(m_new)
            alpha_out.append(jnp.exp2(m_prev[hh] - m_new))
        return tuple(m_out), tuple(alpha_out)

    def weighted_values(grp, live, src_ref, alpha):
        for hh in heads:
            pv = jnp.dot(vt_ref[grp, hh * V_ROWS:(hh + 1) * V_ROWS, :], src_ref[hh], preferred_element_type=F32)
            acc_ref[hh] = alpha[hh] * acc_ref[hh] + jnp.where(live, pv, 0.0)

    acc_ref[...] = jnp.zeros_like(acc_ref)
    pb_ref[...] = jnp.zeros_like(pb_ref)
    scores(n_groups - 1, sa_ref)

    def body(u, carry):
        m_prev, alpha_b, grp_b_prev = carry
        grp_a = n_groups - 1 - 2 * u
        grp_b = grp_a - 1
        weighted_values(jnp.maximum(grp_b_prev, 0), grp_b_prev >= 0, pb_ref, alpha_b)
        m_a, alpha_a = softmax(grp_a, sa_ref, pa_ref, m_prev)
        scores(jnp.maximum(grp_b, 0), sb_ref)
        weighted_values(grp_a, True, pa_ref, alpha_a)
        m_b, alpha_b = softmax(jnp.maximum(grp_b, 0), sb_ref, pb_ref, m_a)
        scores(jnp.maximum(grp_b - 1, 0), sa_ref)
        return m_b, alpha_b, grp_b

    m0 = tuple(jnp.full((1, tq), NEG, F32) for _ in heads)
    a0 = tuple(jnp.ones((1, tq), F32) for _ in heads)
    _, alpha_b, grp_b = lax.fori_loop(0, (n_groups + 1) // 2, body, (m0, a0, jnp.int32(-1)))
    weighted_values(jnp.maximum(grp_b, 0), grp_b >= 0, pb_ref, alpha_b)
    for hh in heads:
        acc = acc_ref[hh]
        o_ref[:, lanes[hh]] = (acc[:HEAD_DIM] / acc[HEAD_DIM:HEAD_DIM + 1]).T.astype(o_ref.dtype)


def _attention(k, km, q_t, v_t, bias, rel_bias):
    S = k.shape[0]
    nb = S // MOBA_BLOCK
    tq = MOBA_BLOCK
    per = GROUP_KEYS // tq
    width = ATTN_HEADS * HEAD_DIM
    key_blk = jnp.arange(S, dtype=jnp.int32)[:, None] // MOBA_BLOCK
    lane = jnp.arange(HEAD_DIM, dtype=jnp.int32)[None, :]
    ind = jnp.logical_and(lane < 2 * nb, lane % nb == key_blk).astype(BF16)
    return pl.pallas_call(
        _attn_kernel,
        grid=(N_HEADS // ATTN_HEADS, S // tq),
        in_specs=[pl.BlockSpec(memory_space=pltpu.SMEM),
                  pl.BlockSpec((1, width, tq), lambda h, i: (i // per, h, i % per)),
                  pl.BlockSpec((S, width), lambda h, i: (0, h)),
                  pl.BlockSpec((S // GROUP_KEYS, ATTN_HEADS * V_ROWS, GROUP_KEYS), lambda h, i: (0, h, 0)),
                  pl.BlockSpec((nb, width), lambda h, i: (0, h)),
                  pl.BlockSpec((S, HEAD_DIM), lambda h, i: (0, 0)),
                  pl.BlockSpec((ATTN_HEADS, N_BIAS_TILES, MOBA_BLOCK, MOBA_BLOCK), lambda h, i: (h, 0, 0, 0))],
        out_specs=pl.BlockSpec((tq, width), lambda h, i: (i, h)),
        out_shape=jax.ShapeDtypeStruct((S, D_ATTN), BF16),
        scratch_shapes=[pltpu.VMEM((ATTN_HEADS, HEAD_DIM + SUM_ROWS, tq), F32),
                        pltpu.VMEM((ATTN_HEADS, GROUP_KEYS, tq), F32),
                        pltpu.VMEM((ATTN_HEADS, GROUP_KEYS, tq), F32),
                        pltpu.VMEM((ATTN_HEADS, GROUP_KEYS, tq), BF16),
                        pltpu.VMEM((ATTN_HEADS, GROUP_KEYS, tq), BF16)],
        compiler_params=_params(("parallel", "parallel")),
        name="moba_attention",
    )(rel_bias, q_t, k, v_t, km, ind, bias)


def _mix_kernel(cs_ref, at_ref, ga_ref, gb_ref, x_ref, wc_ref, wa_ref, wo_ref, o_ref):
    yc = jnp.dot(cs_ref[...], wc_ref[...], preferred_element_type=F32)
    ya = jnp.dot(at_ref[...], wa_ref[...], preferred_element_type=F32)
    merged = ga_ref[...].astype(F32) * yc + gb_ref[...].astype(F32) * ya
    o_ref[...] = x_ref[...] + jnp.dot(merged.astype(BF16), wo_ref[...], preferred_element_type=F32)


def _mix(cs, attn, gates, x, wc, wa, wo, tm=256):
    S, D = x.shape
    full = lambda shape: pl.BlockSpec(shape, lambda i: (0, 0))
    return pl.pallas_call(
        _mix_kernel,
        grid=(S // tm,),
        in_specs=[pl.BlockSpec((tm, D_CONV), lambda i: (i, 0)),
                  pl.BlockSpec((tm, D_ATTN), lambda i: (i, 0)),
                  pl.BlockSpec((tm, D), lambda i: (i, 0)),
                  pl.BlockSpec((tm, D), lambda i: (i, 1)),
                  pl.BlockSpec((tm, D), lambda i: (i, 0)),
                  full((D_CONV, D)), full((D_ATTN, D)), full((D, D))],
        out_specs=pl.BlockSpec((tm, D), lambda i: (i, 0)),
        out_shape=jax.ShapeDtypeStruct((S, D), F32),
        compiler_params=_params(("parallel",)),
        name="mix_out_proj",
    )(cs, attn, gates, gates, x, wc, wa, wo)


def _ffn_kernel(xh_ref, x_ref, g_ref, wa_ref, wb_ref, dwa_ref, dwb_ref, ba_ref, bb_ref, wd_ref,
                fg_ref, o_ref, h_ref, acc_ref, gate_ref, *, final_norm, n_tiles, up_chunk):
    i = pl.program_id(0)
    f = pl.program_id(1)
    tm = x_ref.shape[0]
    tf = wa_ref.shape[1]
    slot = f % 2

    def up_stage():
        h = h_ref[...]
        for c0 in range(0, tf, up_chunk):
            cols = slice(c0, c0 + up_chunk)

            def up_conv(w_ref, dw_ref, b_ref):
                u = jnp.dot(h, w_ref[:, cols], preferred_element_type=F32)
                y = (u * dw_ref[2:3, cols] + pltpu.roll(u, 1, 0) * dw_ref[1:2, cols]
                     + pltpu.roll(u, 2, 0) * dw_ref[0:1, cols] + b_ref[:, cols])
                return y[HALO:]

            ua = up_conv(wa_ref, dwa_ref, ba_ref)
            ub = up_conv(wb_ref, dwb_ref, bb_ref)
            gate_ref[slot, :, cols] = (jax.nn.silu(ub) * ua).astype(BF16)

    def down_product():
        return jnp.dot(gate_ref[1 - slot], wd_ref[...], preferred_element_type=F32)

    @pl.when(f == 0)
    def _():
        xh = xh_ref[...]
        ms = jnp.mean(xh * xh, axis=-1, keepdims=True)
        hh = xh * lax.rsqrt(ms + EPS) * g_ref[...]
        h_ref[0:HALO, :] = jnp.where(i > 0, hh, jnp.zeros_like(hh)).astype(BF16)
        _rmsnorm_rows(x_ref, g_ref, h_ref, tm, dst_off=HALO)
        acc_ref[...] = jnp.zeros_like(acc_ref)
        up_stage()

    @pl.when(jnp.logical_and(f > 0, f < n_tiles))
    def _():
        acc_ref[...] += down_product()
        up_stage()

    @pl.when(f == n_tiles)
    def _():
        if final_norm:
            acc_ref[...] += down_product()

            def body(c, carry):
                r = pl.multiple_of(c * NORM_ROWS, NORM_ROWS)
                xv = x_ref[pl.ds(r, NORM_ROWS), :] + acc_ref[pl.ds(r, NORM_ROWS), :]
                ms = jnp.mean(xv * xv, axis=-1, keepdims=True)
                o_ref[pl.ds(r, NORM_ROWS), :] = xv * lax.rsqrt(ms + EPS) * fg_ref[...]
                return carry
            lax.fori_loop(0, tm // NORM_ROWS, body, 0, unroll=NORM_UNROLL)
        else:
            o_ref[...] = x_ref[...] + acc_ref[...] + down_product()


def _ffn(x, g, w_up, dw, dwb, w_down, final_g, final_norm, tm=512, tf=W_TILE):
    S, D = x.shape
    nf = D_FF // tf
    per = tm // HALO
    row = lambda shape: pl.BlockSpec(shape, lambda i, f: (0, 0))
    up = lambda f: jnp.minimum(f, nf - 1)
    down = lambda f: jnp.maximum(f - 1, 0)
    return pl.pallas_call(
        functools.partial(_ffn_kernel, final_norm=final_norm, n_tiles=nf, up_chunk=tf // 2),
        grid=(S // tm, nf + 1),
        in_specs=[pl.BlockSpec((HALO, D), lambda i, f: (jnp.maximum(i * per - 1, 0), 0)),
                  pl.BlockSpec((tm, D), lambda i, f: (i, 0)),
                  row((1, D)),
                  pl.BlockSpec((D, tf), lambda i, f: (0, up(f))),
                  pl.BlockSpec((D, tf), lambda i, f: (0, up(f) + nf)),
                  pl.BlockSpec((FFN_CONV_K, tf), lambda i, f: (0, up(f))),
                  pl.BlockSpec((FFN_CONV_K, tf), lambda i, f: (0, up(f) + nf)),
                  pl.BlockSpec((1, tf), lambda i, f: (0, up(f))),
                  pl.BlockSpec((1, tf), lambda i, f: (0, up(f) + nf)),
                  pl.BlockSpec((tf, D), lambda i, f: (down(f), 0)),
                  row((1, D))],
        out_specs=pl.BlockSpec((tm, D), lambda i, f: (i, 0)),
        out_shape=jax.ShapeDtypeStruct((S, D), F32),
        scratch_shapes=[pltpu.VMEM((HALO + tm, D), BF16), pltpu.VMEM((tm, D), F32),
                        pltpu.VMEM((2, tm, tf), BF16)],
        compiler_params=_params(("parallel", "arbitrary")),
        name="conv_ffn",
    )(x, x, g, w_up, w_up, dw, dw, dwb, dwb, w_down, final_g)


def kernel(x, norm1_g, w_in, conv_dw, conv_dw_b, conv_ln_g, conv_ln_b, w_conv_out, rel_bias,
           w_attn_out, w_out, norm2_g, w_up, ffn_dw, ffn_dw_b, w_down, final_g):
    B, S, D = x.shape
    assert D == D_MODEL and S % GROUP_KEYS == 0
    depth = w_in.shape[0]
    bias = _bias_tiles(rel_bias)
    row = lambda v: v.reshape(1, -1)
    q0 = 2 * D_CONV
    v0 = q0 + 2 * D_ATTN
    outs = []
    for b in range(B):
        xb = x[b]
        for l in range(depth):
            w_l = w_in[l]
            wt_qv = jnp.concatenate([w_l[:, q0:q0 + D_ATTN], w_l[:, v0:v0 + D_ATTN]], axis=1).T.astype(BF16)
            glu, k, km, q_t, v_t, gates = _inproj(xb, row(norm1_g[l]), w_l.astype(BF16), wt_qv)
            cs = _conv_branch(glu, conv_dw[l], row(conv_dw_b[l]), row(conv_ln_g[l]), row(conv_ln_b[l]))
            attn = _attention(k, km, q_t, v_t, bias, rel_bias)
            xb = _mix(cs, attn, gates, xb, w_conv_out[l].astype(BF16), w_attn_out[l].astype(BF16),
                      w_out[l].astype(BF16))
            xb = _ffn(xb, row(norm2_g[l]), w_up[l].astype(BF16), ffn_dw[l], row(ffn_dw_b[l]),
                      w_down[l].astype(BF16), row(final_g), final_norm=(l == depth - 1))
        outs.append(xb)
    return jnp.stack(outs, axis=0)
```

```python
import functools
import math

import jax
import jax.numpy as jnp
from jax import lax
from jax.experimental import pallas as pl
from jax.experimental.pallas import tpu as pltpu

F32 = jnp.float32
BF16 = jnp.bfloat16

D_MODEL = 2048
D_CONV = 1024
CONV_K = 31
N_HEADS = 8
HEAD_DIM = 128
D_ATTN = N_HEADS * HEAD_DIM
MOBA_BLOCK = 256
MOBA_TOPK = 3
N_BUCKETS = 32
REL_MAX_DIST = 2048
D_FF = 5632
FFN_CONV_K = 3
EPS = 1e-6
NEG = -1e30
LOG2E = math.log2(math.e)

N_BIAS_TILES = 8
KV_GROUP = 2
GROUP_KEYS = KV_GROUP * MOBA_BLOCK
INPROJ_ROWS = 1024
SUM_ROWS = 16
V_ROWS = HEAD_DIM + SUM_ROWS
ATTN_HEADS = 2

V7X_VMEM_BYTES = 64 * 1024 * 1024
VMEM_LIMIT = V7X_VMEM_BYTES - 8 * 1024 * 1024

SUBLANES = 8
NORM_ROWS = 16
NORM_UNROLL = 4
HALO = 8
CONV_HALO = 32

W_TILE = 512

NT_DIMS = (((1,), (1,)), ((), ()))


def _params(sem):
    return pltpu.CompilerParams(dimension_semantics=sem, vmem_limit_bytes=VMEM_LIMIT)


def _rmsnorm_rows(x_ref, g_ref, dst_ref, n_rows, dst_off=0, out_dtype=BF16):
    def body(c, carry):
        r = pl.multiple_of(c * NORM_ROWS, NORM_ROWS)
        xv = x_ref[pl.ds(r, NORM_ROWS), :]
        ms = jnp.mean(xv * xv, axis=-1, keepdims=True)
        y = xv * lax.rsqrt(ms + EPS) * g_ref[...]
        dst_ref[pl.ds(dst_off + r, NORM_ROWS), :] = y.astype(out_dtype)
        return carry
    lax.fori_loop(0, n_rows // NORM_ROWS, body, 0, unroll=NORM_UNROLL)


def _inproj_glu_kernel(x_ref, g_ref, wa_ref, wb_ref, o_ref, h_ref):
    @pl.when(pl.program_id(1) == 0)
    def _():
        _rmsnorm_rows(x_ref, g_ref, h_ref, x_ref.shape[0])
    h = h_ref[...]
    a = jnp.dot(h, wa_ref[...], preferred_element_type=F32)
    b = jnp.dot(h, wb_ref[...], preferred_element_type=F32)
    o_ref[...] = a * jax.nn.sigmoid(b)


def _inproj_k_kernel(x_ref, g_ref, w_ref, o_ref, km_ref, h_ref):
    @pl.when(pl.program_id(1) == 0)
    def _():
        _rmsnorm_rows(x_ref, g_ref, h_ref, x_ref.shape[0])
    acc = jnp.dot(h_ref[...], w_ref[...], preferred_element_type=F32)
    o_ref[...] = acc.astype(o_ref.dtype)
    for b in range(acc.shape[0] // MOBA_BLOCK):
        blk = acc[b * MOBA_BLOCK:(b + 1) * MOBA_BLOCK]
        km_ref[0, b:b + 1, :] = jnp.mean(blk, axis=0, keepdims=True)


def _inproj_t_kernel(x_ref, g_ref, wt_ref, q_ref, v_ref, h_ref, *, q_tiles):
    j = pl.program_id(1)

    @pl.when(j == 0)
    def _():
        _rmsnorm_rows(x_ref, g_ref, h_ref, x_ref.shape[0])
    acc = lax.dot_general(wt_ref[...], h_ref[...], NT_DIMS, preferred_element_type=F32)

    @pl.when(j < q_tiles)
    def _():
        q_ref[0] = acc.astype(q_ref.dtype)

    @pl.when(j >= q_tiles)
    def _():
        for grp in range(v_ref.shape[0]):
            pos = slice(grp * GROUP_KEYS, (grp + 1) * GROUP_KEYS)
            for hh in range(acc.shape[0] // HEAD_DIM):
                r = hh * V_ROWS
                v_ref[grp, r:r + HEAD_DIM, :] = acc[hh * HEAD_DIM:(hh + 1) * HEAD_DIM, pos].astype(v_ref.dtype)
                v_ref[grp, r + HEAD_DIM:r + V_ROWS, :] = jnp.ones((SUM_ROWS, GROUP_KEYS), v_ref.dtype)


def _inproj_gate_kernel(x_ref, g_ref, w_ref, o_ref, h_ref):
    @pl.when(pl.program_id(1) == 0)
    def _():
        _rmsnorm_rows(x_ref, g_ref, h_ref, x_ref.shape[0])
    acc = jnp.dot(h_ref[...], w_ref[...], preferred_element_type=F32)
    o_ref[...] = jax.nn.sigmoid(acc).astype(o_ref.dtype)


def _inproj(x, g, w_in, wt_qv, tn=W_TILE):
    S, D = x.shape
    tm = INPROJ_ROWS
    nm = S // tm
    groups = tm // GROUP_KEYS
    x_spec = pl.BlockSpec((tm, D), lambda i, j: (i, 0))
    g_spec = pl.BlockSpec((1, D), lambda i, j: (0, 0))
    scratch = [pltpu.VMEM((tm, D), BF16)]
    sem = ("parallel", "arbitrary")

    def w_spec(col0):
        off = col0 // tn
        return pl.BlockSpec((D, tn), lambda i, j: (0, j + off))

    glu = pl.pallas_call(
        _inproj_glu_kernel,
        grid=(nm, D_CONV // tn),
        in_specs=[x_spec, g_spec, w_spec(0), w_spec(D_CONV)],
        out_specs=pl.BlockSpec((tm, tn), lambda i, j: (i, j)),
        out_shape=jax.ShapeDtypeStruct((S, D_CONV), F32),
        scratch_shapes=scratch,
        compiler_params=_params(sem),
        name="inproj_glu",
    )(x, g, w_in, w_in)

    k, km = pl.pallas_call(
        _inproj_k_kernel,
        grid=(nm, D_ATTN // tn),
        in_specs=[x_spec, g_spec, w_spec(2 * D_CONV + D_ATTN)],
        out_specs=[pl.BlockSpec((tm, tn), lambda i, j: (i, j)),
                   pl.BlockSpec((1, tm // MOBA_BLOCK, tn), lambda i, j: (i, 0, j))],
        out_shape=[jax.ShapeDtypeStruct((S, D_ATTN), BF16),
                   jax.ShapeDtypeStruct((nm, tm // MOBA_BLOCK, D_ATTN), F32)],
        scratch_shapes=scratch,
        compiler_params=_params(sem),
        name="inproj_k",
    )(x, g, w_in)

    q_tiles = D_ATTN // tn
    v_tile_rows = tn // HEAD_DIM * V_ROWS
    q_t, v_t = pl.pallas_call(
        functools.partial(_inproj_t_kernel, q_tiles=q_tiles),
        grid=(nm, 2 * q_tiles),
        in_specs=[x_spec, g_spec, pl.BlockSpec((tn, D), lambda i, j: (j, 0))],
        out_specs=[pl.BlockSpec((1, tn, tm), lambda i, j: (i, jnp.minimum(j, q_tiles - 1), 0)),
                   pl.BlockSpec((groups, v_tile_rows, GROUP_KEYS),
                                lambda i, j: (i, jnp.maximum(j - q_tiles, 0), 0))],
        out_shape=[jax.ShapeDtypeStruct((nm, D_ATTN, tm), BF16),
                   jax.ShapeDtypeStruct((S // GROUP_KEYS, N_HEADS * V_ROWS, GROUP_KEYS), BF16)],
        scratch_shapes=scratch,
        compiler_params=_params(sem),
        name="inproj_qv_t",
    )(x, g, wt_qv)

    gates = pl.pallas_call(
        _inproj_gate_kernel,
        grid=(nm, 2 * D_MODEL // tn),
        in_specs=[x_spec, g_spec, w_spec(2 * D_CONV + 3 * D_ATTN)],
        out_specs=pl.BlockSpec((tm, tn), lambda i, j: (i, j)),
        out_shape=jax.ShapeDtypeStruct((S, 2 * D_MODEL), BF16),
        scratch_shapes=scratch,
        compiler_params=_params(sem),
        name="inproj_gates",
    )(x, g, w_in)
    return glu, k, km.reshape(S // MOBA_BLOCK, D_ATTN), q_t, v_t, gates


def _conv_kernel(halo_ref, x_ref, w_ref, b_ref, lg_ref, lb_ref, o_ref, win_ref, y_ref,
                 *, rows, lanes):
    tm, C = x_ref.shape
    i = pl.program_id(0)
    halo = halo_ref[...]
    win_ref[0:CONV_HALO, :] = jnp.where(i > 0, halo, jnp.zeros_like(halo))
    win_ref[CONV_HALO:, :] = x_ref[...]
    first = CONV_HALO - (CONV_K - 1)

    for c0 in range(0, C, lanes):
        def body(rc, carry):
            r = pl.multiple_of(rc * rows, rows)
            acc = jnp.zeros((rows, lanes), F32) + b_ref[:, c0:c0 + lanes]
            win = win_ref[pl.ds(r, rows + CONV_HALO), c0:c0 + lanes]
            for sub in range(SUBLANES):
                shifted = win if sub == 0 else pltpu.roll(win, win.shape[0] - sub, 0)
                usable = win.shape[0] - (SUBLANES if sub else 0)
                for base in range(0, usable - rows + 1, SUBLANES):
                    k = base + sub - first
                    if 0 <= k < CONV_K:
                        acc = acc + shifted[base:base + rows] * w_ref[k:k + 1, c0:c0 + lanes]
            y_ref[pl.ds(r, rows), c0:c0 + lanes] = acc
            return carry
        lax.fori_loop(0, tm // rows, body, 0)

    def ln_body(rc, carry):
        r = pl.multiple_of(rc * NORM_ROWS, NORM_ROWS)
        v = y_ref[pl.ds(r, NORM_ROWS), :]
        mu = jnp.mean(v, axis=-1, keepdims=True)
        var = jnp.mean(jnp.square(v - mu), axis=-1, keepdims=True)
        z = (v - mu) * lax.rsqrt(var + EPS) * lg_ref[...] + lb_ref[...]
        o_ref[pl.ds(r, NORM_ROWS), :] = jax.nn.silu(z).astype(o_ref.dtype)
        return carry
    lax.fori_loop(0, tm // NORM_ROWS, ln_body, 0, unroll=NORM_UNROLL)


def _conv_branch(glu, w, b, lg, lb, tm=256):
    S, C = glu.shape
    per = tm // CONV_HALO
    return pl.pallas_call(
        functools.partial(_conv_kernel, rows=32, lanes=256),
        grid=(S // tm,),
        in_specs=[pl.BlockSpec((CONV_HALO, C), lambda i: (jnp.maximum(i * per - 1, 0), 0)),
                  pl.BlockSpec((tm, C), lambda i: (i, 0)),
                  pl.BlockSpec((CONV_K, C), lambda i: (0, 0)),
                  pl.BlockSpec((1, C), lambda i: (0, 0)),
                  pl.BlockSpec((1, C), lambda i: (0, 0)),
                  pl.BlockSpec((1, C), lambda i: (0, 0))],
        out_specs=pl.BlockSpec((tm, C), lambda i: (i, 0)),
        out_shape=jax.ShapeDtypeStruct((S, C), BF16),
        scratch_shapes=[pltpu.VMEM((tm + CONV_HALO, C), F32), pltpu.VMEM((tm, C), F32)],
        compiler_params=_params(("parallel",)),
        name="conv_branch",
    )(glu, glu, w, b, lg, lb)


def _bias_kernel(rb_ref, o_ref):
    h = pl.program_id(0)
    delta = pl.program_id(1)
    shape = (MOBA_BLOCK, MOBA_BLOCK)
    d = delta * MOBA_BLOCK + lax.broadcasted_iota(jnp.int32, shape, 1) - lax.broadcasted_iota(jnp.int32, shape, 0)
    n = jnp.maximum(d, 0)
    max_exact = N_BUCKETS // 2
    nf = jnp.maximum(n, 1).astype(F32)
    large = max_exact + (jnp.log(nf / max_exact) / math.log(REL_MAX_DIST / max_exact)
                         * (N_BUCKETS - max_exact)).astype(jnp.int32)
    large = jnp.minimum(large, N_BUCKETS - 1)
    bucket = jnp.where(n < max_exact, n, large)
    val = jnp.zeros(shape, F32)
    for b in range(N_BUCKETS):
        val = jnp.where(bucket == b, rb_ref[b, h], val)
    val = jnp.where(delta == N_BIAS_TILES - 1, 0.0, val * LOG2E)
    o_ref[0, 0] = jnp.where(d >= 0, val, NEG)


def _bias_tiles(rel_bias):
    return pl.pallas_call(
        _bias_kernel,
        grid=(N_HEADS, N_BIAS_TILES),
        in_specs=[pl.BlockSpec(memory_space=pltpu.SMEM)],
        out_specs=pl.BlockSpec((1, 1, MOBA_BLOCK, MOBA_BLOCK), lambda h, d: (h, d, 0, 0)),
        out_shape=jax.ShapeDtypeStruct((N_HEADS, N_BIAS_TILES, MOBA_BLOCK, MOBA_BLOCK), F32),
        compiler_params=_params(("parallel", "parallel")),
        name="t5_bias_tiles",
    )(rel_bias)


def _attn_kernel(rb_ref, qt_ref, k_ref, vt_ref, km_ref, ind_ref, bias_ref, o_ref,
                 acc_ref, sa_ref, sb_ref, pa_ref, pb_ref):
    i = pl.program_id(1)
    nb = km_ref.shape[0]
    tq = qt_ref.shape[2]
    scale = HEAD_DIM ** -0.5
    heads = range(ATTN_HEADS)
    lanes = [slice(hh * HEAD_DIM, (hh + 1) * HEAD_DIM) for hh in heads]

    def widened_query(hh):
        qf = qt_ref[0, lanes[hh], :].astype(F32)
        gate = jnp.dot(km_ref[:, lanes[hh]], qf, precision=lax.Precision.HIGHEST, preferred_element_type=F32)
        row = lax.broadcasted_iota(jnp.int32, (nb, tq), 0)
        valid = row < i
        g = jnp.where(valid, gate, NEG)
        sel = jnp.zeros((nb, tq), jnp.bool_)
        for _ in range(MOBA_TOPK):
            top = jnp.max(g, axis=0, keepdims=True)
            idx = jnp.min(jnp.where(g == top, row, nb), axis=0, keepdims=True)
            hit = row == idx
            sel = jnp.logical_or(sel, hit)
            g = jnp.where(hit, -jnp.inf, g)
        picked = jnp.logical_and(sel, valid)
        far = jnp.logical_and(picked, row <= i - (N_BIAS_TILES - 1))
        far_bias = rb_ref[N_BUCKETS - 1, pl.program_id(0) * ATTN_HEADS + hh] * LOG2E
        pen = jnp.where(row == i, 0.0, jnp.where(picked, jnp.where(far, far_bias, 0.0), NEG))
        pen_hi = pen.astype(BF16)
        pen_lo = jnp.where(far, pen - pen_hi.astype(F32), 0.0).astype(BF16)
        return jnp.concatenate(
            [(qf * (scale * LOG2E)).astype(BF16), pen_hi, pen_lo, jnp.zeros((HEAD_DIM - 2 * nb, tq), BF16)],
            axis=0)

    q_aug = [widened_query(hh) for hh in heads]
    n_groups = i // KV_GROUP + 1

    def scores(grp, dst_ref):
        r0 = pl.multiple_of(grp * GROUP_KEYS, GROUP_KEYS)
        ind = ind_ref[pl.ds(r0, GROUP_KEYS), :]
        for hh in heads:
            k_aug = jnp.concatenate([k_ref[pl.ds(r0, GROUP_KEYS), lanes[hh]], ind], axis=1)
            dst_ref[hh] = jnp.dot(k_aug, q_aug[hh], preferred_element_type=F32)

    def softmax(grp, src_ref, dst_ref, m_prev):
        m_out, alpha_out = [], []
        for hh in heads:
            parts = []
            for u in range(KV_GROUP):
                delta = jnp.clip(i - (grp * KV_GROUP + u), 0, N_BIAS_TILES - 1)
                parts.append(src_ref[hh, u * MOBA_BLOCK:(u + 1) * MOBA_BLOCK, :] + bias_ref[hh, delta])
            s = jnp.concatenate(parts, axis=0)
            m_new = jnp.maximum(m_prev[hh], jnp.max(s, axis=0, keepdims=True))
            dst_ref[hh] = jnp.exp2(s - m_new).astype(BF16)
            m_out.append(m_new)
            alpha_out.append(jnp.exp2(m_prev[hh] - m_new))
        return tuple(m_out), tuple(alpha_out)

    def weighted_values(grp, live, src_ref, alpha):
        for hh in heads:
            pv = jnp.dot(vt_ref[grp, hh * V_ROWS:(hh + 1) * V_ROWS, :], src_ref[hh], preferred_element_type=F32)
            acc_ref[hh] = alpha[hh] * acc_ref[hh] + jnp.where(live, pv, 0.0)

    acc_ref[...] = jnp.zeros_like(acc_ref)
    pb_ref[...] = jnp.zeros_like(pb_ref)
    scores(n_groups - 1, sa_ref)

    def body(u, carry):
        m_prev, alpha_b, grp_b_prev = carry
        grp_a = n_groups - 1 - 2 * u
        grp_b = grp_a - 1
        weighted_values(jnp.maximum(grp_b_prev, 0), grp_b_prev >= 0, pb_ref, alpha_b)
        m_a, alpha_a = softmax(grp_a, sa_ref, pa_ref, m_prev)
        scores(jnp.maximum(grp_b, 0), sb_ref)
        weighted_values(grp_a, True, pa_ref, alpha_a)
        m_b, alpha_b = softmax(jnp.maximum(grp_b, 0), sb_ref, pb_ref, m_a)
        scores(jnp.maximum(grp_b - 1, 0), sa_ref)
        return m_b, alpha_b, grp_b

    m0 = tuple(jnp.full((1, tq), NEG, F32) for _ in heads)
    a0 = tuple(jnp.ones((1, tq), F32) for _ in heads)
    _, alpha_b, grp_b = lax.fori_loop(0, (n_groups + 1) // 2, body, (m0, a0, jnp.int32(-1)))
    weighted_values(jnp.maximum(grp_b, 0), grp_b >= 0, pb_ref, alpha_b)
    for hh in heads:
        acc = acc_ref[hh]
        o_ref[:, lanes[hh]] = (acc[:HEAD_DIM] / acc[HEAD_DIM:HEAD_DIM + 1]).T.astype(o_ref.dtype)


def _attention(k, km, q_t, v_t, bias, rel_bias):
    S = k.shape[0]
    nb = S // MOBA_BLOCK
    tq = MOBA_BLOCK
    per = INPROJ_ROWS // tq
    width = ATTN_HEADS * HEAD_DIM
    key_blk = jnp.arange(S, dtype=jnp.int32)[:, None] // MOBA_BLOCK
    lane = jnp.arange(HEAD_DIM, dtype=jnp.int32)[None, :]
    ind = jnp.logical_and(lane < 2 * nb, lane % nb == key_blk).astype(BF16)
    return pl.pallas_call(
        _attn_kernel,
        grid=(N_HEADS // ATTN_HEADS, S // tq),
        in_specs=[pl.BlockSpec(memory_space=pltpu.SMEM),
                  pl.BlockSpec((1, width, tq), lambda h, i: (i // per, h, i % per)),
                  pl.BlockSpec((S, width), lambda h, i: (0, h)),
                  pl.BlockSpec((S // GROUP_KEYS, ATTN_HEADS * V_ROWS, GROUP_KEYS), lambda h, i: (0, h, 0)),
                  pl.BlockSpec((nb, width), lambda h, i: (0, h)),
                  pl.BlockSpec((S, HEAD_DIM), lambda h, i: (0, 0)),
                  pl.BlockSpec((ATTN_HEADS, N_BIAS_TILES, MOBA_BLOCK, MOBA_BLOCK), lambda h, i: (h, 0, 0, 0))],
        out_specs=pl.BlockSpec((tq, width), lambda h, i: (i, h)),
        out_shape=jax.ShapeDtypeStruct((S, D_ATTN), BF16),
        scratch_shapes=[pltpu.VMEM((ATTN_HEADS, HEAD_DIM + SUM_ROWS, tq), F32),
                        pltpu.VMEM((ATTN_HEADS, GROUP_KEYS, tq), F32),
                        pltpu.VMEM((ATTN_HEADS, GROUP_KEYS, tq), F32),
                        pltpu.VMEM((ATTN_HEADS, GROUP_KEYS, tq), BF16),
                        pltpu.VMEM((ATTN_HEADS, GROUP_KEYS, tq), BF16)],
        compiler_params=_params(("parallel", "parallel")),
        name="moba_attention",
    )(rel_bias, q_t, k, v_t, km, ind, bias)


def _mix_kernel(cs_ref, at_ref, ga_ref, gb_ref, x_ref, wc_ref, wa_ref, wo_ref, o_ref):
    yc = jnp.dot(cs_ref[...], wc_ref[...], preferred_element_type=F32)
    ya = jnp.dot(at_ref[...], wa_ref[...], preferred_element_type=F32)
    merged = ga_ref[...].astype(F32) * yc + gb_ref[...].astype(F32) * ya
    o_ref[...] = x_ref[...] + jnp.dot(merged.astype(BF16), wo_ref[...], preferred_element_type=F32)


def _mix(cs, attn, gates, x, wc, wa, wo, tm=256):
    S, D = x.shape
    full = lambda shape: pl.BlockSpec(shape, lambda i: (0, 0))
    return pl.pallas_call(
        _mix_kernel,
        grid=(S // tm,),
        in_specs=[pl.BlockSpec((tm, D_CONV), lambda i: (i, 0)),
                  pl.BlockSpec((tm, D_ATTN), lambda i: (i, 0)),
                  pl.BlockSpec((tm, D), lambda i: (i, 0)),
                  pl.BlockSpec((tm, D), lambda i: (i, 1)),
                  pl.BlockSpec((tm, D), lambda i: (i, 0)),
                  full((D_CONV, D)), full((D_ATTN, D)), full((D, D))],
        out_specs=pl.BlockSpec((tm, D), lambda i: (i, 0)),
        out_shape=jax.ShapeDtypeStruct((S, D), F32),
        compiler_params=_params(("parallel",)),
        name="mix_out_proj",
    )(cs, attn, gates, gates, x, wc, wa, wo)


def _ffn_kernel(xh_ref, x_ref, g_ref, wa_ref, wb_ref, dwa_ref, dwb_ref, ba_ref, bb_ref, wd_ref,
                fg_ref, o_ref, h_ref, acc_ref, gate_ref, *, final_norm, n_tiles, up_chunk):
    i = pl.program_id(0)
    f = pl.program_id(1)
    tm = x_ref.shape[0]
    tf = wa_ref.shape[1]
    slot = f % 2

    def up_stage():
        h = h_ref[...]
        for c0 in range(0, tf, up_chunk):
            cols = slice(c0, c0 + up_chunk)

            def up_conv(w_ref, dw_ref, b_ref):
                u = jnp.dot(h, w_ref[:, cols], preferred_element_type=F32)
                y = (u * dw_ref[2:3, cols] + pltpu.roll(u, 1, 0) * dw_ref[1:2, cols]
                     + pltpu.roll(u, 2, 0) * dw_ref[0:1, cols] + b_ref[:, cols])
                return y[HALO:]

            ua = up_conv(wa_ref, dwa_ref, ba_ref)
            ub = up_conv(wb_ref, dwb_ref, bb_ref)
            gate_ref[slot, :, cols] = (jax.nn.silu(ub) * ua).astype(BF16)

    def down_product():
        return jnp.dot(gate_ref[1 - slot], wd_ref[...], preferred_element_type=F32)

    @pl.when(f == 0)
    def _():
        xh = xh_ref[...]
        ms = jnp.mean(xh * xh, axis=-1, keepdims=True)
        hh = xh * lax.rsqrt(ms + EPS) * g_ref[...]
        h_ref[0:HALO, :] = jnp.where(i > 0, hh, jnp.zeros_like(hh)).astype(BF16)
        _rmsnorm_rows(x_ref, g_ref, h_ref, tm, dst_off=HALO)
        acc_ref[...] = jnp.zeros_like(acc_ref)
        up_stage()

    @pl.when(jnp.logical_and(f > 0, f < n_tiles))
    def _():
        acc_ref[...] += down_product()
        up_stage()

    @pl.when(f == n_tiles)
    def _():
        if final_norm:
            acc_ref[...] += down_product()

            def body(c, carry):
                r = pl.multiple_of(c * NORM_ROWS, NORM_ROWS)
                xv = x_ref[pl.ds(r, NORM_ROWS), :] + acc_ref[pl.ds(r, NORM_ROWS), :]
                ms = jnp.mean(xv * xv, axis=-1, keepdims=True)
                o_ref[pl.ds(r, NORM_ROWS), :] = xv * lax.rsqrt(ms + EPS) * fg_ref[...]
                return carry
            lax.fori_loop(0, tm // NORM_ROWS, body, 0, unroll=NORM_UNROLL)
        else:
            o_ref[...] = x_ref[...] + acc_ref[...] + down_product()


def _ffn(x, g, w_up, dw, dwb, w_down, final_g, final_norm, tm=512, tf=W_TILE):
    S, D = x.shape
    nf = D_FF // tf
    per = tm // HALO
    row = lambda shape: pl.BlockSpec(shape, lambda i, f: (0, 0))
    up = lambda f: jnp.minimum(f, nf - 1)
    down = lambda f: jnp.maximum(f - 1, 0)
    return pl.pallas_call(
        functools.partial(_ffn_kernel, final_norm=final_norm, n_tiles=nf, up_chunk=tf // 2),
        grid=(S // tm, nf + 1),
        in_specs=[pl.BlockSpec((HALO, D), lambda i, f: (jnp.maximum(i * per - 1, 0), 0)),
                  pl.BlockSpec((tm, D), lambda i, f: (i, 0)),
                  row((1, D)),
                  pl.BlockSpec((D, tf), lambda i, f: (0, up(f))),
                  pl.BlockSpec((D, tf), lambda i, f: (0, up(f) + nf)),
                  pl.BlockSpec((FFN_CONV_K, tf), lambda i, f: (0, up(f))),
                  pl.BlockSpec((FFN_CONV_K, tf), lambda i, f: (0, up(f) + nf)),
                  pl.BlockSpec((1, tf), lambda i, f: (0, up(f))),
                  pl.BlockSpec((1, tf), lambda i, f: (0, up(f) + nf)),
                  pl.BlockSpec((tf, D), lambda i, f: (down(f), 0)),
                  row((1, D))],
        out_specs=pl.BlockSpec((tm, D), lambda i, f: (i, 0)),
        out_shape=jax.ShapeDtypeStruct((S, D), F32),
        scratch_shapes=[pltpu.VMEM((HALO + tm, D), BF16), pltpu.VMEM((tm, D), F32),
                        pltpu.VMEM((2, tm, tf), BF16)],
        compiler_params=_params(("parallel", "arbitrary")),
        name="conv_ffn",
    )(x, x, g, w_up, w_up, dw, dw, dwb, dwb, w_down, final_g)


def kernel(x, norm1_g, w_in, conv_dw, conv_dw_b, conv_ln_g, conv_ln_b, w_conv_out, rel_bias,
           w_attn_out, w_out, norm2_g, w_up, ffn_dw, ffn_dw_b, w_down, final_g):
    B, S, D = x.shape
    assert D == D_MODEL and S % INPROJ_ROWS == 0 and INPROJ_ROWS % GROUP_KEYS == 0
    depth = w_in.shape[0]
    bias = _bias_tiles(rel_bias)
    row = lambda v: v.reshape(1, -1)
    q0 = 2 * D_CONV
    v0 = q0 + 2 * D_ATTN
    outs = []
    for b in range(B):
        xb = x[b]
        for l in range(depth):
            w_l = w_in[l]
            wt_qv = jnp.concatenate([w_l[:, q0:q0 + D_ATTN], w_l[:, v0:v0 + D_ATTN]], axis=1).T.astype(BF16)
            glu, k, km, q_t, v_t, gates = _inproj(xb, row(norm1_g[l]), w_l.astype(BF16), wt_qv)
            cs = _conv_branch(glu, conv_dw[l], row(conv_dw_b[l]), row(conv_ln_g[l]), row(conv_ln_b[l]))
            attn = _attention(k, km, q_t, v_t, bias, rel_bias)
            xb = _mix(cs, attn, gates, xb, w_conv_out[l].astype(BF16), w_attn_out[l].astype(BF16),
                      w_out[l].astype(BF16))
            xb = _ffn(xb, row(norm2_g[l]), w_up[l].astype(BF16), ffn_dw[l], row(ffn_dw_b[l]),
                      w_down[l].astype(BF16), row(final_g), final_norm=(l == depth - 1))
        outs.append(xb)
    return jnp.stack(outs, axis=0)
```

```python
import functools
import math

import jax
import jax.numpy as jnp
from jax import lax
from jax.experimental import pallas as pl
from jax.experimental.pallas import tpu as pltpu

F32 = jnp.float32
BF16 = jnp.bfloat16

D_MODEL = 2048
D_CONV = 1024
CONV_K = 31
N_HEADS = 8
HEAD_DIM = 128
D_ATTN = N_HEADS * HEAD_DIM
MOBA_BLOCK = 256
MOBA_TOPK = 3
N_BUCKETS = 32
REL_MAX_DIST = 2048
D_FF = 5632
FFN_CONV_K = 3
EPS = 1e-6
NEG = -1e30
LOG2E = math.log2(math.e)

N_BIAS_TILES = 8
KV_GROUP = 2
GROUP_KEYS = KV_GROUP * MOBA_BLOCK
INPROJ_ROWS = 1024
SUM_ROWS = 16
V_ROWS = HEAD_DIM + SUM_ROWS
ATTN_HEADS = 2

V7X_VMEM_BYTES = 64 * 1024 * 1024
VMEM_LIMIT = V7X_VMEM_BYTES - 8 * 1024 * 1024

SUBLANES = 8
NORM_ROWS = 16
NORM_UNROLL = 4
HALO = 8
CONV_HALO = 32

W_TILE = 512

NT_DIMS = (((1,), (1,)), ((), ()))


def _params(sem):
    return pltpu.CompilerParams(dimension_semantics=sem, vmem_limit_bytes=VMEM_LIMIT)


def _rmsnorm_rows(x_ref, g_ref, dst_ref, n_rows, dst_off=0, out_dtype=BF16):
    def body(c, carry):
        r = pl.multiple_of(c * NORM_ROWS, NORM_ROWS)
        xv = x_ref[pl.ds(r, NORM_ROWS), :]
        ms = jnp.mean(xv * xv, axis=-1, keepdims=True)
        y = xv * lax.rsqrt(ms + EPS) * g_ref[...]
        dst_ref[pl.ds(dst_off + r, NORM_ROWS), :] = y.astype(out_dtype)
        return carry
    lax.fori_loop(0, n_rows // NORM_ROWS, body, 0, unroll=NORM_UNROLL)


def _inproj_glu_kernel(x_ref, g_ref, wa_ref, wb_ref, o_ref, h_ref):
    @pl.when(pl.program_id(1) == 0)
    def _():
        _rmsnorm_rows(x_ref, g_ref, h_ref, x_ref.shape[0])
    h = h_ref[...]
    a = jnp.dot(h, wa_ref[...], preferred_element_type=F32)
    b = jnp.dot(h, wb_ref[...], preferred_element_type=F32)
    o_ref[...] = a * jax.nn.sigmoid(b)


def _inproj_k_kernel(x_ref, g_ref, w_ref, o_ref, km_ref, h_ref):
    @pl.when(pl.program_id(1) == 0)
    def _():
        _rmsnorm_rows(x_ref, g_ref, h_ref, x_ref.shape[0])
    acc = jnp.dot(h_ref[...], w_ref[...], preferred_element_type=F32)
    o_ref[...] = acc.astype(o_ref.dtype)
    for b in range(acc.shape[0] // MOBA_BLOCK):
        blk = acc[b * MOBA_BLOCK:(b + 1) * MOBA_BLOCK]
        km_ref[0, b:b + 1, :] = jnp.mean(blk, axis=0, keepdims=True)


def _inproj_t_kernel(x_ref, g_ref, wt_ref, q_ref, v_ref, h_ref, *, q_tiles):
    j = pl.program_id(1)

    @pl.when(j == 0)
    def _():
        _rmsnorm_rows(x_ref, g_ref, h_ref, x_ref.shape[0])
    acc = lax.dot_general(wt_ref[...], h_ref[...], NT_DIMS, preferred_element_type=F32)

    @pl.when(j < q_tiles)
    def _():
        q_ref[0] = acc.astype(q_ref.dtype)

    @pl.when(j >= q_tiles)
    def _():
        for grp in range(v_ref.shape[0]):
            pos = slice(grp * GROUP_KEYS, (grp + 1) * GROUP_KEYS)
            for hh in range(acc.shape[0] // HEAD_DIM):
                r = hh * V_ROWS
                v_ref[grp, r:r + HEAD_DIM, :] = acc[hh * HEAD_DIM:(hh + 1) * HEAD_DIM, pos].astype(v_ref.dtype)
                v_ref[grp, r + HEAD_DIM:r + V_ROWS, :] = jnp.ones((SUM_ROWS, GROUP_KEYS), v_ref.dtype)


def _inproj_gate_kernel(x_ref, g_ref, w_ref, o_ref, h_ref):
    @pl.when(pl.program_id(1) == 0)
    def _():
        _rmsnorm_rows(x_ref, g_ref, h_ref, x_ref.shape[0])
    acc = jnp.dot(h_ref[...], w_ref[...], preferred_element_type=F32)
    o_ref[...] = jax.nn.sigmoid(acc).astype(o_ref.dtype)


def _inproj(x, g, w_in, wt_qv, tn=W_TILE):
    S, D = x.shape
    tm = INPROJ_ROWS
    nm = S // tm
    groups = tm // GROUP_KEYS
    x_spec = pl.BlockSpec((tm, D), lambda i, j: (i, 0))
    g_spec = pl.BlockSpec((1, D), lambda i, j: (0, 0))
    scratch = [pltpu.VMEM((tm, D), BF16)]
    sem = ("parallel", "arbitrary")

    def w_spec(col0):
        off = col0 // tn
        return pl.BlockSpec((D, tn), lambda i, j: (0, j + off))

    glu = pl.pallas_call(
        _inproj_glu_kernel,
        grid=(nm, D_CONV // tn),
        in_specs=[x_spec, g_spec, w_spec(0), w_spec(D_CONV)],
        out_specs=pl.BlockSpec((tm, tn), lambda i, j: (i, j)),
        out_shape=jax.ShapeDtypeStruct((S, D_CONV), F32),
        scratch_shapes=scratch,
        compiler_params=_params(sem),
        name="inproj_glu",
    )(x, g, w_in, w_in)

    k, km = pl.pallas_call(
        _inproj_k_kernel,
        grid=(nm, D_ATTN // tn),
        in_specs=[x_spec, g_spec, w_spec(2 * D_CONV + D_ATTN)],
        out_specs=[pl.BlockSpec((tm, tn), lambda i, j: (i, j)),
                   pl.BlockSpec((1, tm // MOBA_BLOCK, tn), lambda i, j: (i, 0, j))],
        out_shape=[jax.ShapeDtypeStruct((S, D_ATTN), BF16),
                   jax.ShapeDtypeStruct((nm, tm // MOBA_BLOCK, D_ATTN), F32)],
        scratch_shapes=scratch,
        compiler_params=_params(sem),
        name="inproj_k",
    )(x, g, w_in)

    q_tiles = D_ATTN // tn
    v_tile_rows = tn // HEAD_DIM * V_ROWS
    q_t, v_t = pl.pallas_call(
        functools.partial(_inproj_t_kernel, q_tiles=q_tiles),
        grid=(nm, 2 * q_tiles),
        in_specs=[x_spec, g_spec, pl.BlockSpec((tn, D), lambda i, j: (j, 0))],
        out_specs=[pl.BlockSpec((1, tn, tm), lambda i, j: (i, jnp.minimum(j, q_tiles - 1), 0)),
                   pl.BlockSpec((groups, v_tile_rows, GROUP_KEYS),
                                lambda i, j: (i, jnp.maximum(j - q_tiles, 0), 0))],
        out_shape=[jax.ShapeDtypeStruct((nm, D_ATTN, tm), BF16),
                   jax.ShapeDtypeStruct((S // GROUP_KEYS, N_HEADS * V_ROWS, GROUP_KEYS), BF16)],
        scratch_shapes=scratch,
        compiler_params=_params(sem),
        name="inproj_qv_t",
    )(x, g, wt_qv)

    gates = pl.pallas_call(
        _inproj_gate_kernel,
        grid=(nm, 2 * D_MODEL // tn),
        in_specs=[x_spec, g_spec, w_spec(2 * D_CONV + 3 * D_ATTN)],
        out_specs=pl.BlockSpec((tm, tn), lambda i, j: (i, j)),
        out_shape=jax.ShapeDtypeStruct((S, 2 * D_MODEL), BF16),
        scratch_shapes=scratch,
        compiler_params=_params(sem),
        name="inproj_gates",
    )(x, g, w_in)
    return glu, k, km.reshape(S // MOBA_BLOCK, D_ATTN), q_t, v_t, gates


def _conv_kernel(halo_ref, x_ref, w_ref, b_ref, lg_ref, lb_ref, o_ref, win_ref, y_ref,
                 *, rows, lanes):
    tm, C = x_ref.shape
    i = pl.program_id(0)
    halo = halo_ref[...]
    win_ref[0:CONV_HALO, :] = jnp.where(i > 0, halo, jnp.zeros_like(halo))
    win_ref[CONV_HALO:, :] = x_ref[...]
    first = CONV_HALO - (CONV_K - 1)

    for c0 in range(0, C, lanes):
        def body(rc, carry):
            r = pl.multiple_of(rc * rows, rows)
            acc = jnp.zeros((rows, lanes), F32) + b_ref[:, c0:c0 + lanes]
            win = win_ref[pl.ds(r, rows + CONV_HALO), c0:c0 + lanes]
            for sub in range(SUBLANES):
                shifted = win if sub == 0 else pltpu.roll(win, win.shape[0] - sub, 0)
                usable = win.shape[0] - (SUBLANES if sub else 0)
                for base in range(0, usable - rows + 1, SUBLANES):
                    k = base + sub - first
                    if 0 <= k < CONV_K:
                        acc = acc + shifted[base:base + rows] * w_ref[k:k + 1, c0:c0 + lanes]
            y_ref[pl.ds(r, rows), c0:c0 + lanes] = acc
            return carry
        lax.fori_loop(0, tm // rows, body, 0)

    def ln_body(rc, carry):
        r = pl.multiple_of(rc * NORM_ROWS, NORM_ROWS)
        v = y_ref[pl.ds(r, NORM_ROWS), :]
        mu = jnp.mean(v, axis=-1, keepdims=True)
        var = jnp.mean(jnp.square(v - mu), axis=-1, keepdims=True)
        z = (v - mu) * lax.rsqrt(var + EPS) * lg_ref[...] + lb_ref[...]
        o_ref[pl.ds(r, NORM_ROWS), :] = jax.nn.silu(z).astype(o_ref.dtype)
        return carry
    lax.fori_loop(0, tm // NORM_ROWS, ln_body, 0, unroll=NORM_UNROLL)


def _conv_branch(glu, w, b, lg, lb, tm=256):
    S, C = glu.shape
    per = tm // CONV_HALO
    return pl.pallas_call(
        functools.partial(_conv_kernel, rows=32, lanes=256),
        grid=(S // tm,),
        in_specs=[pl.BlockSpec((CONV_HALO, C), lambda i: (jnp.maximum(i * per - 1, 0), 0)),
                  pl.BlockSpec((tm, C), lambda i: (i, 0)),
                  pl.BlockSpec((CONV_K, C), lambda i: (0, 0)),
                  pl.BlockSpec((1, C), lambda i: (0, 0)),
                  pl.BlockSpec((1, C), lambda i: (0, 0)),
                  pl.BlockSpec((1, C), lambda i: (0, 0))],
        out_specs=pl.BlockSpec((tm, C), lambda i: (i, 0)),
        out_shape=jax.ShapeDtypeStruct((S, C), BF16),
        scratch_shapes=[pltpu.VMEM((tm + CONV_HALO, C), F32), pltpu.VMEM((tm, C), F32)],
        compiler_params=_params(("parallel",)),
        name="conv_branch",
    )(glu, glu, w, b, lg, lb)


def _bias_kernel(rb_ref, o_ref):
    h = pl.program_id(0)
    delta = pl.program_id(1)
    shape = (MOBA_BLOCK, MOBA_BLOCK)
    d = delta * MOBA_BLOCK + lax.broadcasted_iota(jnp.int32, shape, 1) - lax.broadcasted_iota(jnp.int32, shape, 0)
    n = jnp.maximum(d, 0)
    max_exact = N_BUCKETS // 2
    nf = jnp.maximum(n, 1).astype(F32)
    large = max_exact + (jnp.log(nf / max_exact) / math.log(REL_MAX_DIST / max_exact)
                         * (N_BUCKETS - max_exact)).astype(jnp.int32)
    large = jnp.minimum(large, N_BUCKETS - 1)
    bucket = jnp.where(n < max_exact, n, large)
    val = jnp.zeros(shape, F32)
    for b in range(N_BUCKETS):
        val = jnp.where(bucket == b, rb_ref[b, h], val)
    val = jnp.where(delta == N_BIAS_TILES - 1, 0.0, val * LOG2E)
    o_ref[0, 0] = jnp.where(d >= 0, val, NEG)


def _bias_tiles(rel_bias):
    return pl.pallas_call(
        _bias_kernel,
        grid=(N_HEADS, N_BIAS_TILES),
        in_specs=[pl.BlockSpec(memory_space=pltpu.SMEM)],
        out_specs=pl.BlockSpec((1, 1, MOBA_BLOCK, MOBA_BLOCK), lambda h, d: (h, d, 0, 0)),
        out_shape=jax.ShapeDtypeStruct((N_HEADS, N_BIAS_TILES, MOBA_BLOCK, MOBA_BLOCK), F32),
        compiler_params=_params(("parallel", "parallel")),
        name="t5_bias_tiles",
    )(rel_bias)


def _attn_kernel(rb_ref, qt_ref, k_ref, vt_ref, km_ref, ind_ref, bias_ref, o_ref,
                 acc_ref, sa_ref, sb_ref, pa_ref, pb_ref):
    i = pl.program_id(1)
    nb = km_ref.shape[0]
    tq = qt_ref.shape[2]
    scale = HEAD_DIM ** -0.5
    heads = range(ATTN_HEADS)
    lanes = [slice(hh * HEAD_DIM, (hh + 1) * HEAD_DIM) for hh in heads]

    def widened_query(hh):
        qf = qt_ref[0, lanes[hh], :].astype(F32)
        gate = jnp.dot(km_ref[:, lanes[hh]], qf, precision=lax.Precision.HIGHEST, preferred_element_type=F32)
        row = lax.broadcasted_iota(jnp.int32, (nb, tq), 0)
        valid = row < i
        g = jnp.where(valid, gate, NEG)
        sel = jnp.zeros((nb, tq), jnp.bool_)
        for _ in range(MOBA_TOPK):
            top = jnp.max(g, axis=0, keepdims=True)
            idx = jnp.min(jnp.where(g == top, row, nb), axis=0, keepdims=True)
            hit = row == idx
            sel = jnp.logical_or(sel, hit)
            g = jnp.where(hit, -jnp.inf, g)
        picked = jnp.logical_and(sel, valid)
        far = jnp.logical_and(picked, row <= i - (N_BIAS_TILES - 1))
        far_bias = rb_ref[N_BUCKETS - 1, pl.program_id(0) * ATTN_HEADS + hh] * LOG2E
        pen = jnp.where(row == i, 0.0, jnp.where(picked, jnp.where(far, far_bias, 0.0), NEG))
        pen_hi = pen.astype(BF16)
        pen_lo = jnp.where(far, pen - pen_hi.astype(F32), 0.0).astype(BF16)
        return jnp.concatenate(
            [(qf * (scale * LOG2E)).astype(BF16), pen_hi, pen_lo, jnp.zeros((HEAD_DIM - 2 * nb, tq), BF16)],
            axis=0)

    q_aug = [widened_query(hh) for hh in heads]
    n_groups = i // KV_GROUP + 1

    def scores(grp, dst_ref):
        r0 = pl.multiple_of(grp * GROUP_KEYS, GROUP_KEYS)
        ind = ind_ref[pl.ds(r0, GROUP_KEYS), :]
        for hh in heads:
            k_aug = jnp.concatenate([k_ref[pl.ds(r0, GROUP_KEYS), lanes[hh]], ind], axis=1)
            dst_ref[hh] = jnp.dot(k_aug, q_aug[hh], preferred_element_type=F32)

    def softmax(grp, src_ref, dst_ref, m_prev):
        m_out, alpha_out = [], []
        for hh in heads:
            parts = []
            for u in range(KV_GROUP):
                delta = jnp.clip(i - (grp * KV_GROUP + u), 0, N_BIAS_TILES - 1)
                parts.append(src_ref[hh, u * MOBA_BLOCK:(u + 1) * MOBA_BLOCK, :] + bias_ref[hh, delta])
            s = jnp.concatenate(parts, axis=0)
            m_new = jnp.maximum(m_prev[hh], jnp.max(s, axis=0, keepdims=True))
            dst_ref[hh] = jnp.exp2(s - m_new).astype(BF16)
            m_out.append(m_new)
            alpha_out.append(jnp.exp2(m_prev[hh] - m_new))
        return tuple(m_out), tuple(alpha_out)

    def weighted_values(grp, live, src_ref, alpha):
        for hh in heads:
            pv = jnp.dot(vt_ref[grp, hh * V_ROWS:(hh + 1) * V_ROWS, :], src_ref[hh], preferred_element_type=F32)
            acc_ref[hh] = alpha[hh] * acc_ref[hh] + jnp.where(live, pv, 0.0)

    acc_ref[...] = jnp.zeros_like(acc_ref)
    pb_ref[...] = jnp.zeros_like(pb_ref)
    scores(n_groups - 1, sa_ref)

    def body(u, carry):
        m_prev, alpha_b, grp_b_prev = carry
        grp_a = n_groups - 1 - 2 * u
        grp_b = grp_a - 1
        weighted_values(jnp.maximum(grp_b_prev, 0), grp_b_prev >= 0, pb_ref, alpha_b)
        m_a, alpha_a = softmax(grp_a, sa_ref, pa_ref, m_prev)
        scores(jnp.maximum(grp_b, 0), sb_ref)
        weighted_values(grp_a, True, pa_ref, alpha_a)
        m_b, alpha_b = softmax(jnp.maximum(grp_b, 0), sb_ref, pb_ref, m_a)
        scores(jnp.maximum(grp_b - 1, 0), sa_ref)
        return m_b, alpha_b, grp_b

    m0 = tuple(jnp.full((1, tq), NEG, F32) for _ in heads)
    a0 = tuple(jnp.ones((1, tq), F32) for _ in heads)
    _, alpha_b, grp_b = lax.fori_loop(0, (n_groups + 1) // 2, body, (m0, a0, jnp.int32(-1)))
    weighted_values(jnp.maximum(grp_b, 0), grp_b >= 0, pb_ref, alpha_b)
    for hh in heads:
        acc = acc_ref[hh]
        o_ref[:, lanes[hh]] = (acc[:HEAD_DIM] / acc[HEAD_DIM:HEAD_DIM + 1]).T.astype(o_ref.dtype)


def _attention(k, km, q_t, v_t, bias, rel_bias):
    S = k.shape[0]
    nb = S // MOBA_BLOCK
    tq = MOBA_BLOCK
    per = INPROJ_ROWS // tq
    width = ATTN_HEADS * HEAD_DIM
    key_blk = jnp.arange(S, dtype=jnp.int32)[:, None] // MOBA_BLOCK
    lane = jnp.arange(HEAD_DIM, dtype=jnp.int32)[None, :]
    ind = jnp.logical_and(lane < 2 * nb, lane % nb == key_blk).astype(BF16)
    return pl.pallas_call(
        _attn_kernel,
        grid=(N_HEADS // ATTN_HEADS, S // tq),
        in_specs=[pl.BlockSpec(memory_space=pltpu.SMEM),
                  pl.BlockSpec((1, width, tq), lambda h, i: (i // per, h, i % per)),
                  pl.BlockSpec((S, width), lambda h, i: (0, h)),
                  pl.BlockSpec((S // GROUP_KEYS, ATTN_HEADS * V_ROWS, GROUP_KEYS), lambda h, i: (0, h, 0)),
                  pl.BlockSpec((nb, width), lambda h, i: (0, h)),
                  pl.BlockSpec((S, HEAD_DIM), lambda h, i: (0, 0)),
                  pl.BlockSpec((ATTN_HEADS, N_BIAS_TILES, MOBA_BLOCK, MOBA_BLOCK), lambda h, i: (h, 0, 0, 0))],
        out_specs=pl.BlockSpec((tq, width), lambda h, i: (i, h)),
        out_shape=jax.ShapeDtypeStruct((S, D_ATTN), BF16),
        scratch_shapes=[pltpu.VMEM((ATTN_HEADS, HEAD_DIM + SUM_ROWS, tq), F32),
                        pltpu.VMEM((ATTN_HEADS, GROUP_KEYS, tq), F32),
                        pltpu.VMEM((ATTN_HEADS, GROUP_KEYS, tq), F32),
                        pltpu.VMEM((ATTN_HEADS, GROUP_KEYS, tq), BF16),
                        pltpu.VMEM((ATTN_HEADS, GROUP_KEYS, tq), BF16)],
        compiler_params=_params(("parallel", "parallel")),
        name="moba_attention",
    )(rel_bias, q_t, k, v_t, km, ind, bias)


def _mix_kernel(cs_ref, at_ref, ga_ref, gb_ref, x_ref, wc_ref, wa_ref, wo_ref, o_ref):
    yc = jnp.dot(cs_ref[...], wc_ref[...], preferred_element_type=F32)
    ya = jnp.dot(at_ref[...], wa_ref[...], preferred_element_type=F32)
    merged = ga_ref[...].astype(F32) * yc + gb_ref[...].astype(F32) * ya
    o_ref[...] = x_ref[...] + jnp.dot(merged.astype(BF16), wo_ref[...], preferred_element_type=F32)


def _mix(cs, attn, gates, x, wc, wa, wo, tm=256):
    S, D = x.shape
    full = lambda shape: pl.BlockSpec(shape, lambda i: (0, 0))
    return pl.pallas_call(
        _mix_kernel,
        grid=(S // tm,),
        in_specs=[pl.BlockSpec((tm, D_CONV), lambda i: (i, 0)),
                  pl.BlockSpec((tm, D_ATTN), lambda i: (i, 0)),
                  pl.BlockSpec((tm, D), lambda i: (i, 0)),
                  pl.BlockSpec((tm, D), lambda i: (i, 1)),
                  pl.BlockSpec((tm, D), lambda i: (i, 0)),
                  full((D_CONV, D)), full((D_ATTN, D)), full((D, D))],
        out_specs=pl.BlockSpec((tm, D), lambda i: (i, 0)),
        out_shape=jax.ShapeDtypeStruct((S, D), F32),
        compiler_params=_params(("parallel",)),
        name="mix_out_proj",
    )(cs, attn, gates, gates, x, wc, wa, wo)


def _ffn_kernel(xh_ref, x_ref, g_ref, wa_ref, wb_ref, dwa_ref, dwb_ref, ba_ref, bb_ref, wd_ref,
                fg_ref, o_ref, h_ref, gate_ref, *, final_norm, n_tiles, up_chunk):
    i = pl.program_id(0)
    f = pl.program_id(1)
    tm = x_ref.shape[0]
    tf = wa_ref.shape[1]
    slot = f % 2

    def up_stage():
        h = h_ref[...]
        for c0 in range(0, tf, up_chunk):
            cols = slice(c0, c0 + up_chunk)

            def up_conv(w_ref, dw_ref, b_ref):
                u = jnp.dot(h, w_ref[:, cols], preferred_element_type=F32)
                y = (u * dw_ref[2:3, cols] + pltpu.roll(u, 1, 0) * dw_ref[1:2, cols]
                     + pltpu.roll(u, 2, 0) * dw_ref[0:1, cols] + b_ref[:, cols])
                return y[HALO:]

            ua = up_conv(wa_ref, dwa_ref, ba_ref)
            ub = up_conv(wb_ref, dwb_ref, bb_ref)
            gate_ref[slot, :, cols] = (jax.nn.silu(ub) * ua).astype(BF16)

    def down_product():
        return jnp.dot(gate_ref[1 - slot], wd_ref[...], preferred_element_type=F32)

    @pl.when(f == 0)
    def _():
        xh = xh_ref[...]
        ms = jnp.mean(xh * xh, axis=-1, keepdims=True)
        hh = xh * lax.rsqrt(ms + EPS) * g_ref[...]
        h_ref[0:HALO, :] = jnp.where(i > 0, hh, jnp.zeros_like(hh)).astype(BF16)
        _rmsnorm_rows(x_ref, g_ref, h_ref, tm, dst_off=HALO)
        o_ref[...] = jnp.zeros_like(o_ref)
        up_stage()

    @pl.when(jnp.logical_and(f > 0, f < n_tiles))
    def _():
        o_ref[...] += down_product()
        up_stage()

    @pl.when(f == n_tiles)
    def _():
        if final_norm:
            o_ref[...] += down_product()

            def body(c, carry):
                r = pl.multiple_of(c * NORM_ROWS, NORM_ROWS)
                xv = x_ref[pl.ds(r, NORM_ROWS), :] + o_ref[pl.ds(r, NORM_ROWS), :]
                ms = jnp.mean(xv * xv, axis=-1, keepdims=True)
                o_ref[pl.ds(r, NORM_ROWS), :] = xv * lax.rsqrt(ms + EPS) * fg_ref[...]
                return carry
            lax.fori_loop(0, tm // NORM_ROWS, body, 0, unroll=NORM_UNROLL)
        else:
            o_ref[...] = x_ref[...] + o_ref[...] + down_product()


def _ffn(x, g, w_up, dw, dwb, w_down, final_g, final_norm, tm=1024, tf=W_TILE):
    S, D = x.shape
    nf = D_FF // tf
    per = tm // HALO
    row = lambda shape: pl.BlockSpec(shape, lambda i, f: (0, 0))
    up = lambda f: jnp.minimum(f, nf - 1)
    down = lambda f: jnp.maximum(f - 1, 0)
    return pl.pallas_call(
        functools.partial(_ffn_kernel, final_norm=final_norm, n_tiles=nf, up_chunk=tf // 2),
        grid=(S // tm, nf + 1),
        in_specs=[pl.BlockSpec((HALO, D), lambda i, f: (jnp.maximum(i * per - 1, 0), 0)),
                  pl.BlockSpec((tm, D), lambda i, f: (i, 0), pipeline_mode=pl.Buffered(1)),
                  row((1, D)),
                  pl.BlockSpec((D, tf), lambda i, f: (0, up(f))),
                  pl.BlockSpec((D, tf), lambda i, f: (0, up(f) + nf)),
                  pl.BlockSpec((FFN_CONV_K, tf), lambda i, f: (0, up(f))),
                  pl.BlockSpec((FFN_CONV_K, tf), lambda i, f: (0, up(f) + nf)),
                  pl.BlockSpec((1, tf), lambda i, f: (0, up(f))),
                  pl.BlockSpec((1, tf), lambda i, f: (0, up(f) + nf)),
                  pl.BlockSpec((tf, D), lambda i, f: (down(f), 0)),
                  row((1, D))],
        out_specs=pl.BlockSpec((tm, D), lambda i, f: (i, 0)),
        out_shape=jax.ShapeDtypeStruct((S, D), F32),
        scratch_shapes=[pltpu.VMEM((HALO + tm, D), BF16), pltpu.VMEM((2, tm, tf), BF16)],
        compiler_params=_params(("parallel", "arbitrary")),
        name="conv_ffn",
    )(x, x, g, w_up, w_up, dw, dw, dwb, dwb, w_down, final_g)


def kernel(x, norm1_g, w_in, conv_dw, conv_dw_b, conv_ln_g, conv_ln_b, w_conv_out, rel_bias,
           w_attn_out, w_out, norm2_g, w_up, ffn_dw, ffn_dw_b, w_down, final_g):
    B, S, D = x.shape
    assert D == D_MODEL and S % INPROJ_ROWS == 0 and INPROJ_ROWS % GROUP_KEYS == 0
    depth = w_in.shape[0]
    bias = _bias_tiles(rel_bias)
    row = lambda v: v.reshape(1, -1)
    q0 = 2 * D_CONV
    v0 = q0 + 2 * D_ATTN
    outs = []
    for b in range(B):
        xb = x[b]
        for l in range(depth):
            w_l = w_in[l]
            wt_qv = jnp.concatenate([w_l[:, q0:q0 + D_ATTN], w_l[:, v0:v0 + D_ATTN]], axis=1).T.astype(BF16)
            glu, k, km, q_t, v_t, gates = _inproj(xb, row(norm1_g[l]), w_l.astype(BF16), wt_qv)
            cs = _conv_branch(glu, conv_dw[l], row(conv_dw_b[l]), row(conv_ln_g[l]), row(conv_ln_b[l]))
            attn = _attention(k, km, q_t, v_t, bias, rel_bias)
            xb = _mix(cs, attn, gates, xb, w_conv_out[l].astype(BF16), w_attn_out[l].astype(BF16),
                      w_out[l].astype(BF16))
            xb = _ffn(xb, row(norm2_g[l]), w_up[l].astype(BF16), ffn_dw[l], row(ffn_dw_b[l]),
                      w_down[l].astype(BF16), row(final_g), final_norm=(l == depth - 1))
        outs.append(xb)
    return jnp.stack(outs, axis=0)
```

```python
import functools
import math

import jax
import jax.numpy as jnp
from jax import lax
from jax.experimental import pallas as pl
from jax.experimental.pallas import tpu as pltpu

F32 = jnp.float32
BF16 = jnp.bfloat16

D_MODEL = 2048
D_CONV = 1024
CONV_K = 31
N_HEADS = 8
HEAD_DIM = 128
D_ATTN = N_HEADS * HEAD_DIM
MOBA_BLOCK = 256
MOBA_TOPK = 3
N_BUCKETS = 32
REL_MAX_DIST = 2048
D_FF = 5632
FFN_CONV_K = 3
EPS = 1e-6
NEG = -1e30
LOG2E = math.log2(math.e)

N_BIAS_TILES = 8
KV_GROUP = 2
GROUP_KEYS = KV_GROUP * MOBA_BLOCK
INPROJ_ROWS = 1024
SUM_ROWS = 16
V_ROWS = HEAD_DIM + SUM_ROWS
ATTN_HEADS = 2

V7X_VMEM_BYTES = 64 * 1024 * 1024
VMEM_LIMIT = V7X_VMEM_BYTES - 8 * 1024 * 1024

SUBLANES = 8
NORM_ROWS = 16
NORM_UNROLL = 4
HALO = 8
CONV_HALO = 32

W_TILE = 512

NT_DIMS = (((1,), (1,)), ((), ()))


def _params(sem):
    return pltpu.CompilerParams(dimension_semantics=sem, vmem_limit_bytes=VMEM_LIMIT)


def _rmsnorm_rows(x_ref, g_ref, dst_ref, n_rows, dst_off=0, out_dtype=BF16):
    def body(c, carry):
        r = pl.multiple_of(c * NORM_ROWS, NORM_ROWS)
        xv = x_ref[pl.ds(r, NORM_ROWS), :]
        ms = jnp.mean(xv * xv, axis=-1, keepdims=True)
        y = xv * lax.rsqrt(ms + EPS) * g_ref[...]
        dst_ref[pl.ds(dst_off + r, NORM_ROWS), :] = y.astype(out_dtype)
        return carry
    lax.fori_loop(0, n_rows // NORM_ROWS, body, 0, unroll=NORM_UNROLL)


def _inproj_glu_kernel(x_ref, g_ref, wa_ref, wb_ref, o_ref, h_ref):
    @pl.when(pl.program_id(1) == 0)
    def _():
        _rmsnorm_rows(x_ref, g_ref, h_ref, x_ref.shape[0])
    h = h_ref[...]
    a = jnp.dot(h, wa_ref[...], preferred_element_type=F32)
    b = jnp.dot(h, wb_ref[...], preferred_element_type=F32)
    o_ref[...] = a * jax.nn.sigmoid(b)


def _inproj_k_kernel(x_ref, g_ref, w_ref, o_ref, km_ref, h_ref):
    @pl.when(pl.program_id(1) == 0)
    def _():
        _rmsnorm_rows(x_ref, g_ref, h_ref, x_ref.shape[0])
    acc = jnp.dot(h_ref[...], w_ref[...], preferred_element_type=F32)
    o_ref[...] = acc.astype(o_ref.dtype)
    for b in range(acc.shape[0] // MOBA_BLOCK):
        blk = acc[b * MOBA_BLOCK:(b + 1) * MOBA_BLOCK]
        km_ref[0, b:b + 1, :] = jnp.mean(blk, axis=0, keepdims=True)


def _inproj_t_kernel(x_ref, g_ref, wt_ref, q_ref, v_ref, h_ref, *, q_tiles):
    j = pl.program_id(1)

    @pl.when(j == 0)
    def _():
        _rmsnorm_rows(x_ref, g_ref, h_ref, x_ref.shape[0])
    acc = lax.dot_general(wt_ref[...], h_ref[...], NT_DIMS, preferred_element_type=F32)

    @pl.when(j < q_tiles)
    def _():
        q_ref[0] = acc.astype(q_ref.dtype)

    @pl.when(j >= q_tiles)
    def _():
        for grp in range(v_ref.shape[0]):
            pos = slice(grp * GROUP_KEYS, (grp + 1) * GROUP_KEYS)
            for hh in range(acc.shape[0] // HEAD_DIM):
                r = hh * V_ROWS
                v_ref[grp, r:r + HEAD_DIM, :] = acc[hh * HEAD_DIM:(hh + 1) * HEAD_DIM, pos].astype(v_ref.dtype)
                v_ref[grp, r + HEAD_DIM:r + V_ROWS, :] = jnp.ones((SUM_ROWS, GROUP_KEYS), v_ref.dtype)


def _inproj_gate_kernel(x_ref, g_ref, w_ref, o_ref, h_ref):
    @pl.when(pl.program_id(1) == 0)
    def _():
        _rmsnorm_rows(x_ref, g_ref, h_ref, x_ref.shape[0])
    acc = jnp.dot(h_ref[...], w_ref[...], preferred_element_type=F32)
    o_ref[...] = jax.nn.sigmoid(acc).astype(o_ref.dtype)


def _inproj(x, g, w_in, wt_qv, tn=W_TILE):
    S, D = x.shape
    tm = INPROJ_ROWS
    nm = S // tm
    groups = tm // GROUP_KEYS
    x_spec = pl.BlockSpec((tm, D), lambda i, j: (i, 0))
    g_spec = pl.BlockSpec((1, D), lambda i, j: (0, 0))
    scratch = [pltpu.VMEM((tm, D), BF16)]
    sem = ("parallel", "arbitrary")

    def w_spec(col0):
        off = col0 // tn
        return pl.BlockSpec((D, tn), lambda i, j: (0, j + off))

    glu = pl.pallas_call(
        _inproj_glu_kernel,
        grid=(nm, D_CONV // tn),
        in_specs=[x_spec, g_spec, w_spec(0), w_spec(D_CONV)],
        out_specs=pl.BlockSpec((tm, tn), lambda i, j: (i, j)),
        out_shape=jax.ShapeDtypeStruct((S, D_CONV), F32),
        scratch_shapes=scratch,
        compiler_params=_params(sem),
        name="inproj_glu",
    )(x, g, w_in, w_in)

    k, km = pl.pallas_call(
        _inproj_k_kernel,
        grid=(nm, D_ATTN // tn),
        in_specs=[x_spec, g_spec, w_spec(2 * D_CONV + D_ATTN)],
        out_specs=[pl.BlockSpec((tm, tn), lambda i, j: (i, j)),
                   pl.BlockSpec((1, tm // MOBA_BLOCK, tn), lambda i, j: (i, 0, j))],
        out_shape=[jax.ShapeDtypeStruct((S, D_ATTN), BF16),
                   jax.ShapeDtypeStruct((nm, tm // MOBA_BLOCK, D_ATTN), F32)],
        scratch_shapes=scratch,
        compiler_params=_params(sem),
        name="inproj_k",
    )(x, g, w_in)

    q_tiles = D_ATTN // tn
    v_tile_rows = tn // HEAD_DIM * V_ROWS
    q_t, v_t = pl.pallas_call(
        functools.partial(_inproj_t_kernel, q_tiles=q_tiles),
        grid=(nm, 2 * q_tiles),
        in_specs=[x_spec, g_spec, pl.BlockSpec((tn, D), lambda i, j: (j, 0))],
        out_specs=[pl.BlockSpec((1, tn, tm), lambda i, j: (i, jnp.minimum(j, q_tiles - 1), 0)),
                   pl.BlockSpec((groups, v_tile_rows, GROUP_KEYS),
                                lambda i, j: (i, jnp.maximum(j - q_tiles, 0), 0))],
        out_shape=[jax.ShapeDtypeStruct((nm, D_ATTN, tm), BF16),
                   jax.ShapeDtypeStruct((S // GROUP_KEYS, N_HEADS * V_ROWS, GROUP_KEYS), BF16)],
        scratch_shapes=scratch,
        compiler_params=_params(sem),
        name="inproj_qv_t",
    )(x, g, wt_qv)

    gates = pl.pallas_call(
        _inproj_gate_kernel,
        grid=(nm, 2 * D_MODEL // tn),
        in_specs=[x_spec, g_spec, w_spec(2 * D_CONV + 3 * D_ATTN)],
        out_specs=pl.BlockSpec((tm, tn), lambda i, j: (i, j)),
        out_shape=jax.ShapeDtypeStruct((S, 2 * D_MODEL), BF16),
        scratch_shapes=scratch,
        compiler_params=_params(sem),
        name="inproj_gates",
    )(x, g, w_in)
    return glu, k, km.reshape(S // MOBA_BLOCK, D_ATTN), q_t, v_t, gates


def _inproj_kernel(x_ref, g_ref, w_ref, w2_ref, wt_ref, glu_ref, k_ref, km_ref, q_ref, v_ref, gate_ref,
                   h_ref, *, seg):
    j = pl.program_id(1)
    n_glu, n_k, n_q, n_v, _ = seg
    k0 = n_glu
    q0 = k0 + n_k
    v0 = q0 + n_q
    g0 = v0 + n_v

    @pl.when(j == 0)
    def _():
        _rmsnorm_rows(x_ref, g_ref, h_ref, x_ref.shape[0])

    @pl.when(j < k0)
    def _():
        h = h_ref[...]
        a = jnp.dot(h, w_ref[...], preferred_element_type=F32)
        b = jnp.dot(h, w2_ref[...], preferred_element_type=F32)
        glu_ref[...] = a * jax.nn.sigmoid(b)

    @pl.when(jnp.logical_and(j >= k0, j < q0))
    def _():
        acc = jnp.dot(h_ref[...], w_ref[...], preferred_element_type=F32)
        k_ref[...] = acc.astype(k_ref.dtype)
        for b in range(acc.shape[0] // MOBA_BLOCK):
            blk = acc[b * MOBA_BLOCK:(b + 1) * MOBA_BLOCK]
            km_ref[0, b:b + 1, :] = jnp.mean(blk, axis=0, keepdims=True)

    @pl.when(jnp.logical_and(j >= q0, j < v0))
    def _():
        acc = lax.dot_general(wt_ref[...], h_ref[...], NT_DIMS, preferred_element_type=F32)
        q_ref[0] = acc.astype(q_ref.dtype)

    @pl.when(jnp.logical_and(j >= v0, j < g0))
    def _():
        acc = lax.dot_general(wt_ref[...], h_ref[...], NT_DIMS, preferred_element_type=F32)
        for grp in range(v_ref.shape[0]):
            pos = slice(grp * GROUP_KEYS, (grp + 1) * GROUP_KEYS)
            for hh in range(acc.shape[0] // HEAD_DIM):
                r = hh * V_ROWS
                v_ref[grp, r:r + HEAD_DIM, :] = acc[hh * HEAD_DIM:(hh + 1) * HEAD_DIM, pos].astype(v_ref.dtype)
                v_ref[grp, r + HEAD_DIM:r + V_ROWS, :] = jnp.ones((SUM_ROWS, GROUP_KEYS), v_ref.dtype)

    @pl.when(j >= g0)
    def _():
        acc = jnp.dot(h_ref[...], w_ref[...], preferred_element_type=F32)
        gate_ref[...] = jax.nn.sigmoid(acc).astype(gate_ref.dtype)


def _inproj_fused(x, g, w_in, wt_qv, tn=W_TILE):
    S, D = x.shape
    tm = INPROJ_ROWS
    nm = S // tm
    groups = tm // GROUP_KEYS
    seg = (D_CONV // tn, D_ATTN // tn, D_ATTN // tn, D_ATTN // tn, 2 * D_MODEL // tn)
    n_glu, n_k, n_q, n_v, n_g = seg
    k0, q0 = n_glu, n_glu + n_k
    v0 = q0 + n_q
    g0 = v0 + n_v
    col_k = (2 * D_CONV + D_ATTN) // tn
    col_g = (2 * D_CONV + 3 * D_ATTN) // tn

    def w_col(j):
        return jnp.where(j < k0, j, jnp.where(j < q0, col_k + (j - k0),
                         jnp.where(j < g0, col_k + n_k - 1, col_g + (j - g0))))

    seg_tile = lambda j, start, n: jnp.clip(j - start, 0, n - 1)
    v_tile_rows = tn // HEAD_DIM * V_ROWS
    glu, k, km, q_t, v_t, gates = pl.pallas_call(
        functools.partial(_inproj_kernel, seg=seg),
        grid=(nm, sum(seg)),
        in_specs=[pl.BlockSpec((tm, D), lambda i, j: (i, 0)),
                  pl.BlockSpec((1, D), lambda i, j: (0, 0)),
                  pl.BlockSpec((D, tn), lambda i, j: (0, w_col(j))),
                  pl.BlockSpec((D, tn), lambda i, j: (0, D_CONV // tn + seg_tile(j, 0, n_glu))),
                  pl.BlockSpec((tn, D), lambda i, j: (seg_tile(j, q0, n_q + n_v), 0))],
        out_specs=[pl.BlockSpec((tm, tn), lambda i, j: (i, seg_tile(j, 0, n_glu))),
                   pl.BlockSpec((tm, tn), lambda i, j: (i, seg_tile(j, k0, n_k))),
                   pl.BlockSpec((1, tm // MOBA_BLOCK, tn), lambda i, j: (i, 0, seg_tile(j, k0, n_k))),
                   pl.BlockSpec((1, tn, tm), lambda i, j: (i, seg_tile(j, q0, n_q), 0)),
                   pl.BlockSpec((groups, v_tile_rows, GROUP_KEYS), lambda i, j: (i, seg_tile(j, v0, n_v), 0)),
                   pl.BlockSpec((tm, tn), lambda i, j: (i, seg_tile(j, g0, n_g)))],
        out_shape=[jax.ShapeDtypeStruct((S, D_CONV), F32),
                   jax.ShapeDtypeStruct((S, D_ATTN), BF16),
                   jax.ShapeDtypeStruct((nm, tm // MOBA_BLOCK, D_ATTN), F32),
                   jax.ShapeDtypeStruct((nm, D_ATTN, tm), BF16),
                   jax.ShapeDtypeStruct((S // GROUP_KEYS, N_HEADS * V_ROWS, GROUP_KEYS), BF16),
                   jax.ShapeDtypeStruct((S, 2 * D_MODEL), BF16)],
        scratch_shapes=[pltpu.VMEM((tm, D), BF16)],
        compiler_params=_params(("parallel", "arbitrary")),
        name="inproj",
    )(x, g, w_in, w_in, wt_qv)
    return glu, k, km.reshape(S // MOBA_BLOCK, D_ATTN), q_t, v_t, gates


def _conv_kernel(halo_ref, x_ref, w_ref, b_ref, lg_ref, lb_ref, o_ref, win_ref, y_ref,
                 *, rows, lanes):
    tm, C = x_ref.shape
    i = pl.program_id(0)
    halo = halo_ref[...]
    win_ref[0:CONV_HALO, :] = jnp.where(i > 0, halo, jnp.zeros_like(halo))
    win_ref[CONV_HALO:, :] = x_ref[...]
    first = CONV_HALO - (CONV_K - 1)

    for c0 in range(0, C, lanes):
        def body(rc, carry):
            r = pl.multiple_of(rc * rows, rows)
            acc = jnp.zeros((rows, lanes), F32) + b_ref[:, c0:c0 + lanes]
            win = win_ref[pl.ds(r, rows + CONV_HALO), c0:c0 + lanes]
            for sub in range(SUBLANES):
                shifted = win if sub == 0 else pltpu.roll(win, win.shape[0] - sub, 0)
                usable = win.shape[0] - (SUBLANES if sub else 0)
                for base in range(0, usable - rows + 1, SUBLANES):
                    k = base + sub - first
                    if 0 <= k < CONV_K:
                        acc = acc + shifted[base:base + rows] * w_ref[k:k + 1, c0:c0 + lanes]
            y_ref[pl.ds(r, rows), c0:c0 + lanes] = acc
            return carry
        lax.fori_loop(0, tm // rows, body, 0)

    def ln_body(rc, carry):
        r = pl.multiple_of(rc * NORM_ROWS, NORM_ROWS)
        v = y_ref[pl.ds(r, NORM_ROWS), :]
        mu = jnp.mean(v, axis=-1, keepdims=True)
        var = jnp.mean(jnp.square(v - mu), axis=-1, keepdims=True)
        z = (v - mu) * lax.rsqrt(var + EPS) * lg_ref[...] + lb_ref[...]
        o_ref[pl.ds(r, NORM_ROWS), :] = jax.nn.silu(z).astype(o_ref.dtype)
        return carry
    lax.fori_loop(0, tm // NORM_ROWS, ln_body, 0, unroll=NORM_UNROLL)


def _conv_branch(glu, w, b, lg, lb, tm=256):
    S, C = glu.shape
    per = tm // CONV_HALO
    return pl.pallas_call(
        functools.partial(_conv_kernel, rows=32, lanes=256),
        grid=(S // tm,),
        in_specs=[pl.BlockSpec((CONV_HALO, C), lambda i: (jnp.maximum(i * per - 1, 0), 0)),
                  pl.BlockSpec((tm, C), lambda i: (i, 0)),
                  pl.BlockSpec((CONV_K, C), lambda i: (0, 0)),
                  pl.BlockSpec((1, C), lambda i: (0, 0)),
                  pl.BlockSpec((1, C), lambda i: (0, 0)),
                  pl.BlockSpec((1, C), lambda i: (0, 0))],
        out_specs=pl.BlockSpec((tm, C), lambda i: (i, 0)),
        out_shape=jax.ShapeDtypeStruct((S, C), BF16),
        scratch_shapes=[pltpu.VMEM((tm + CONV_HALO, C), F32), pltpu.VMEM((tm, C), F32)],
        compiler_params=_params(("parallel",)),
        name="conv_branch",
    )(glu, glu, w, b, lg, lb)


def _bias_kernel(rb_ref, o_ref):
    h = pl.program_id(0)
    delta = pl.program_id(1)
    shape = (MOBA_BLOCK, MOBA_BLOCK)
    d = delta * MOBA_BLOCK + lax.broadcasted_iota(jnp.int32, shape, 1) - lax.broadcasted_iota(jnp.int32, shape, 0)
    n = jnp.maximum(d, 0)
    max_exact = N_BUCKETS // 2
    nf = jnp.maximum(n, 1).astype(F32)
    large = max_exact + (jnp.log(nf / max_exact) / math.log(REL_MAX_DIST / max_exact)
                         * (N_BUCKETS - max_exact)).astype(jnp.int32)
    large = jnp.minimum(large, N_BUCKETS - 1)
    bucket = jnp.where(n < max_exact, n, large)
    val = jnp.zeros(shape, F32)
    for b in range(N_BUCKETS):
        val = jnp.where(bucket == b, rb_ref[b, h], val)
    val = jnp.where(delta == N_BIAS_TILES - 1, 0.0, val * LOG2E)
    o_ref[0, 0] = jnp.where(d >= 0, val, NEG)


def _bias_tiles(rel_bias):
    return pl.pallas_call(
        _bias_kernel,
        grid=(N_HEADS, N_BIAS_TILES),
        in_specs=[pl.BlockSpec(memory_space=pltpu.SMEM)],
        out_specs=pl.BlockSpec((1, 1, MOBA_BLOCK, MOBA_BLOCK), lambda h, d: (h, d, 0, 0)),
        out_shape=jax.ShapeDtypeStruct((N_HEADS, N_BIAS_TILES, MOBA_BLOCK, MOBA_BLOCK), F32),
        compiler_params=_params(("parallel", "parallel")),
        name="t5_bias_tiles",
    )(rel_bias)


def _attn_kernel(rb_ref, qt_ref, k_ref, vt_ref, km_ref, ind_ref, bias_ref, o_ref,
                 acc_ref, sa_ref, sb_ref, pa_ref, pb_ref):
    i = pl.program_id(1)
    nb = km_ref.shape[0]
    tq = qt_ref.shape[2]
    scale = HEAD_DIM ** -0.5
    heads = range(ATTN_HEADS)
    lanes = [slice(hh * HEAD_DIM, (hh + 1) * HEAD_DIM) for hh in heads]

    def widened_query(hh):
        qf = qt_ref[0, lanes[hh], :].astype(F32)
        gate = jnp.dot(km_ref[:, lanes[hh]], qf, precision=lax.Precision.HIGHEST, preferred_element_type=F32)
        row = lax.broadcasted_iota(jnp.int32, (nb, tq), 0)
        valid = row < i
        g = jnp.where(valid, gate, NEG)
        sel = jnp.zeros((nb, tq), jnp.bool_)
        for _ in range(MOBA_TOPK):
            top = jnp.max(g, axis=0, keepdims=True)
            idx = jnp.min(jnp.where(g == top, row, nb), axis=0, keepdims=True)
            hit = row == idx
            sel = jnp.logical_or(sel, hit)
            g = jnp.where(hit, -jnp.inf, g)
        picked = jnp.logical_and(sel, valid)
        far = jnp.logical_and(picked, row <= i - (N_BIAS_TILES - 1))
        far_bias = rb_ref[N_BUCKETS - 1, pl.program_id(0) * ATTN_HEADS + hh] * LOG2E
        pen = jnp.where(row == i, 0.0, jnp.where(picked, jnp.where(far, far_bias, 0.0), NEG))
        pen_hi = pen.astype(BF16)
        pen_lo = jnp.where(far, pen - pen_hi.astype(F32), 0.0).astype(BF16)
        return jnp.concatenate(
            [(qf * (scale * LOG2E)).astype(BF16), pen_hi, pen_lo, jnp.zeros((HEAD_DIM - 2 * nb, tq), BF16)],
            axis=0)

    q_aug = [widened_query(hh) for hh in heads]
    n_groups = i // KV_GROUP + 1

    def scores(grp, dst_ref):
        r0 = pl.multiple_of(grp * GROUP_KEYS, GROUP_KEYS)
        ind = ind_ref[pl.ds(r0, GROUP_KEYS), :]
        for hh in heads:
            k_aug = jnp.concatenate([k_ref[pl.ds(r0, GROUP_KEYS), lanes[hh]], ind], axis=1)
            dst_ref[hh] = jnp.dot(k_aug, q_aug[hh], preferred_element_type=F32)

    def softmax(grp, src_ref, dst_ref, m_prev):
        m_out, alpha_out = [], []
        for hh in heads:
            parts = []
            for u in range(KV_GROUP):
                delta = jnp.clip(i - (grp * KV_GROUP + u), 0, N_BIAS_TILES - 1)
                parts.append(src_ref[hh, u * MOBA_BLOCK:(u + 1) * MOBA_BLOCK, :] + bias_ref[hh, delta])
            s = jnp.concatenate(parts, axis=0)
            m_new = jnp.maximum(m_prev[hh], jnp.max(s, axis=0, keepdims=True))
            dst_ref[hh] = jnp.exp2(s - m_new).astype(BF16)
            m_out.append(m_new)
            alpha_out.append(jnp.exp2(m_prev[hh] - m_new))
        return tuple(m_out), tuple(alpha_out)

    def weighted_values(grp, live, src_ref, alpha):
        for hh in heads:
            pv = jnp.dot(vt_ref[grp, hh * V_ROWS:(hh + 1) * V_ROWS, :], src_ref[hh], preferred_element_type=F32)
            acc_ref[hh] = alpha[hh] * acc_ref[hh] + jnp.where(live, pv, 0.0)

    acc_ref[...] = jnp.zeros_like(acc_ref)
    pb_ref[...] = jnp.zeros_like(pb_ref)
    scores(n_groups - 1, sa_ref)

    def body(u, carry):
        m_prev, alpha_b, grp_b_prev = carry
        grp_a = n_groups - 1 - 2 * u
        grp_b = grp_a - 1
        weighted_values(jnp.maximum(grp_b_prev, 0), grp_b_prev >= 0, pb_ref, alpha_b)
        m_a, alpha_a = softmax(grp_a, sa_ref, pa_ref, m_prev)
        scores(jnp.maximum(grp_b, 0), sb_ref)
        weighted_values(grp_a, True, pa_ref, alpha_a)
        m_b, alpha_b = softmax(jnp.maximum(grp_b, 0), sb_ref, pb_ref, m_a)
        scores(jnp.maximum(grp_b - 1, 0), sa_ref)
        return m_b, alpha_b, grp_b

    m0 = tuple(jnp.full((1, tq), NEG, F32) for _ in heads)
    a0 = tuple(jnp.ones((1, tq), F32) for _ in heads)
    _, alpha_b, grp_b = lax.fori_loop(0, (n_groups + 1) // 2, body, (m0, a0, jnp.int32(-1)))
    weighted_values(jnp.maximum(grp_b, 0), grp_b >= 0, pb_ref, alpha_b)
    for hh in heads:
        acc = acc_ref[hh]
        o_ref[:, lanes[hh]] = (acc[:HEAD_DIM] / acc[HEAD_DIM:HEAD_DIM + 1]).T.astype(o_ref.dtype)


def _attention(k, km, q_t, v_t, bias, rel_bias):
    S = k.shape[0]
    nb = S // MOBA_BLOCK
    tq = MOBA_BLOCK
    per = INPROJ_ROWS // tq
    width = ATTN_HEADS * HEAD_DIM
    key_blk = jnp.arange(S, dtype=jnp.int32)[:, None] // MOBA_BLOCK
    lane = jnp.arange(HEAD_DIM, dtype=jnp.int32)[None, :]
    ind = jnp.logical_and(lane < 2 * nb, lane % nb == key_blk).astype(BF16)
    return pl.pallas_call(
        _attn_kernel,
        grid=(N_HEADS // ATTN_HEADS, S // tq),
        in_specs=[pl.BlockSpec(memory_space=pltpu.SMEM),
                  pl.BlockSpec((1, width, tq), lambda h, i: (i // per, h, i % per)),
                  pl.BlockSpec((S, width), lambda h, i: (0, h)),
                  pl.BlockSpec((S // GROUP_KEYS, ATTN_HEADS * V_ROWS, GROUP_KEYS), lambda h, i: (0, h, 0)),
                  pl.BlockSpec((nb, width), lambda h, i: (0, h)),
                  pl.BlockSpec((S, HEAD_DIM), lambda h, i: (0, 0)),
                  pl.BlockSpec((ATTN_HEADS, N_BIAS_TILES, MOBA_BLOCK, MOBA_BLOCK), lambda h, i: (h, 0, 0, 0))],
        out_specs=pl.BlockSpec((tq, width), lambda h, i: (i, h)),
        out_shape=jax.ShapeDtypeStruct((S, D_ATTN), BF16),
        scratch_shapes=[pltpu.VMEM((ATTN_HEADS, HEAD_DIM + SUM_ROWS, tq), F32),
                        pltpu.VMEM((ATTN_HEADS, GROUP_KEYS, tq), F32),
                        pltpu.VMEM((ATTN_HEADS, GROUP_KEYS, tq), F32),
                        pltpu.VMEM((ATTN_HEADS, GROUP_KEYS, tq), BF16),
                        pltpu.VMEM((ATTN_HEADS, GROUP_KEYS, tq), BF16)],
        compiler_params=_params(("parallel", "parallel")),
        name="moba_attention",
    )(rel_bias, q_t, k, v_t, km, ind, bias)


def _mix_kernel(cs_ref, at_ref, ga_ref, gb_ref, x_ref, wc_ref, wa_ref, wo_ref, o_ref):
    yc = jnp.dot(cs_ref[...], wc_ref[...], preferred_element_type=F32)
    ya = jnp.dot(at_ref[...], wa_ref[...], preferred_element_type=F32)
    merged = ga_ref[...].astype(F32) * yc + gb_ref[...].astype(F32) * ya
    o_ref[...] = x_ref[...] + jnp.dot(merged.astype(BF16), wo_ref[...], preferred_element_type=F32)


def _mix(cs, attn, gates, x, wc, wa, wo, tm=256):
    S, D = x.shape
    full = lambda shape: pl.BlockSpec(shape, lambda i: (0, 0))
    return pl.pallas_call(
        _mix_kernel,
        grid=(S // tm,),
        in_specs=[pl.BlockSpec((tm, D_CONV), lambda i: (i, 0)),
                  pl.BlockSpec((tm, D_ATTN), lambda i: (i, 0)),
                  pl.BlockSpec((tm, D), lambda i: (i, 0)),
                  pl.BlockSpec((tm, D), lambda i: (i, 1)),
                  pl.BlockSpec((tm, D), lambda i: (i, 0)),
                  full((D_CONV, D)), full((D_ATTN, D)), full((D, D))],
        out_specs=pl.BlockSpec((tm, D), lambda i: (i, 0)),
        out_shape=jax.ShapeDtypeStruct((S, D), F32),
        compiler_params=_params(("parallel",)),
        name="mix_out_proj",
    )(cs, attn, gates, gates, x, wc, wa, wo)


def _ffn_kernel(xh_ref, x_ref, g_ref, wa_ref, wb_ref, dwa_ref, dwb_ref, ba_ref, bb_ref, wd_ref,
                fg_ref, o_ref, h_ref, gate_ref, *, final_norm, n_tiles, up_chunk):
    i = pl.program_id(0)
    f = pl.program_id(1)
    tm = x_ref.shape[0]
    tf = wa_ref.shape[1]
    slot = f % 2

    def up_stage():
        h = h_ref[...]
        for c0 in range(0, tf, up_chunk):
            cols = slice(c0, c0 + up_chunk)

            def up_conv(w_ref, dw_ref, b_ref):
                u = jnp.dot(h, w_ref[:, cols], preferred_element_type=F32)
                y = (u * dw_ref[2:3, cols] + pltpu.roll(u, 1, 0) * dw_ref[1:2, cols]
                     + pltpu.roll(u, 2, 0) * dw_ref[0:1, cols] + b_ref[:, cols])
                return y[HALO:]

            ua = up_conv(wa_ref, dwa_ref, ba_ref)
            ub = up_conv(wb_ref, dwb_ref, bb_ref)
            gate_ref[slot, :, cols] = (jax.nn.silu(ub) * ua).astype(BF16)

    def down_product():
        return jnp.dot(gate_ref[1 - slot], wd_ref[...], preferred_element_type=F32)

    @pl.when(f == 0)
    def _():
        xh = xh_ref[...]
        ms = jnp.mean(xh * xh, axis=-1, keepdims=True)
        hh = xh * lax.rsqrt(ms + EPS) * g_ref[...]
        h_ref[0:HALO, :] = jnp.where(i > 0, hh, jnp.zeros_like(hh)).astype(BF16)
        _rmsnorm_rows(x_ref, g_ref, h_ref, tm, dst_off=HALO)
        o_ref[...] = jnp.zeros_like(o_ref)
        up_stage()

    @pl.when(jnp.logical_and(f > 0, f < n_tiles))
    def _():
        o_ref[...] += down_product()
        up_stage()

    @pl.when(f == n_tiles)
    def _():
        y = x_ref[...] + o_ref[...] + down_product()
        if final_norm:
            ms = jnp.mean(y * y, axis=-1, keepdims=True)
            y = y * lax.rsqrt(ms + EPS) * fg_ref[...]
        o_ref[...] = y


def _ffn(x, g, w_up, dw, dwb, w_down, final_g, final_norm, tm=1024, tf=W_TILE):
    S, D = x.shape
    nf = D_FF // tf
    per = tm // HALO
    row = lambda shape: pl.BlockSpec(shape, lambda i, f: (0, 0))
    up = lambda f: jnp.minimum(f, nf - 1)
    down = lambda f: jnp.maximum(f - 1, 0)
    return pl.pallas_call(
        functools.partial(_ffn_kernel, final_norm=final_norm, n_tiles=nf, up_chunk=tf // 2),
        grid=(S // tm, nf + 1),
        in_specs=[pl.BlockSpec((HALO, D), lambda i, f: (jnp.maximum(i * per - 1, 0), 0)),
                  pl.BlockSpec((tm, D), lambda i, f: (i, 0), pipeline_mode=pl.Buffered(1)),
                  row((1, D)),
                  pl.BlockSpec((D, tf), lambda i, f: (0, up(f))),
                  pl.BlockSpec((D, tf), lambda i, f: (0, up(f) + nf)),
                  pl.BlockSpec((FFN_CONV_K, tf), lambda i, f: (0, up(f))),
                  pl.BlockSpec((FFN_CONV_K, tf), lambda i, f: (0, up(f) + nf)),
                  pl.BlockSpec((1, tf), lambda i, f: (0, up(f))),
                  pl.BlockSpec((1, tf), lambda i, f: (0, up(f) + nf)),
                  pl.BlockSpec((tf, D), lambda i, f: (down(f), 0)),
                  row((1, D))],
        out_specs=pl.BlockSpec((tm, D), lambda i, f: (i, 0)),
        out_shape=jax.ShapeDtypeStruct((S, D), F32),
        scratch_shapes=[pltpu.VMEM((HALO + tm, D), BF16), pltpu.VMEM((2, tm, tf), BF16)],
        compiler_params=_params(("parallel", "arbitrary")),
        name="conv_ffn",
    )(x, x, g, w_up, w_up, dw, dw, dwb, dwb, w_down, final_g)


def kernel(x, norm1_g, w_in, conv_dw, conv_dw_b, conv_ln_g, conv_ln_b, w_conv_out, rel_bias,
           w_attn_out, w_out, norm2_g, w_up, ffn_dw, ffn_dw_b, w_down, final_g):
    B, S, D = x.shape
    assert D == D_MODEL and S % INPROJ_ROWS == 0 and INPROJ_ROWS % GROUP_KEYS == 0
    depth = w_in.shape[0]
    bias = _bias_tiles(rel_bias)
    row = lambda v: v.reshape(1, -1)
    q0 = 2 * D_CONV
    v0 = q0 + 2 * D_ATTN
    outs = []
    for b in range(B):
        xb = x[b]
        for l in range(depth):
            w_l = w_in[l]
            wt_qv = jnp.concatenate([w_l[:, q0:q0 + D_ATTN], w_l[:, v0:v0 + D_ATTN]], axis=1).T.astype(BF16)
            glu, k, km, q_t, v_t, gates = _inproj_fused(xb, row(norm1_g[l]), w_l.astype(BF16), wt_qv)
            cs = _conv_branch(glu, conv_dw[l], row(conv_dw_b[l]), row(conv_ln_g[l]), row(conv_ln_b[l]))
            attn = _attention(k, km, q_t, v_t, bias, rel_bias)
            xb = _mix(cs, attn, gates, xb, w_conv_out[l].astype(BF16), w_attn_out[l].astype(BF16),
                      w_out[l].astype(BF16))
            xb = _ffn(xb, row(norm2_g[l]), w_up[l].astype(BF16), ffn_dw[l], row(ffn_dw_b[l]),
                      w_down[l].astype(BF16), row(final_g), final_norm=(l == depth - 1))
        outs.append(xb)
    return jnp.stack(outs, axis=0)
```

```python
import functools
import math

import jax
import jax.numpy as jnp
from jax import lax
from jax.experimental import pallas as pl
from jax.experimental.pallas import tpu as pltpu

F32 = jnp.float32
BF16 = jnp.bfloat16

D_MODEL = 2048
D_CONV = 1024
CONV_K = 31
N_HEADS = 8
HEAD_DIM = 128
D_ATTN = N_HEADS * HEAD_DIM
MOBA_BLOCK = 256
MOBA_TOPK = 3
N_BUCKETS = 32
REL_MAX_DIST = 2048
D_FF = 5632
FFN_CONV_K = 3
EPS = 1e-6
NEG = -1e30
LOG2E = math.log2(math.e)

N_BIAS_TILES = 8
BIAS_ROWS = 16
KV_GROUP = 2
GROUP_KEYS = KV_GROUP * MOBA_BLOCK
INPROJ_ROWS = 1024
SUM_ROWS = 16
V_ROWS = HEAD_DIM + SUM_ROWS
ATTN_HEADS = 2

V7X_VMEM_BYTES = 64 * 1024 * 1024
VMEM_LIMIT = V7X_VMEM_BYTES - 8 * 1024 * 1024

SUBLANES = 8
NORM_ROWS = 16
NORM_UNROLL = 4
HALO = 8
CONV_HALO = 32

W_TILE = 512

NT_DIMS = (((1,), (1,)), ((), ()))


def _params(sem):
    return pltpu.CompilerParams(dimension_semantics=sem, vmem_limit_bytes=VMEM_LIMIT)


def _rmsnorm_rows(x_ref, g_ref, dst_ref, n_rows, dst_off=0, out_dtype=BF16):
    def body(c, carry):
        r = pl.multiple_of(c * NORM_ROWS, NORM_ROWS)
        xv = x_ref[pl.ds(r, NORM_ROWS), :]
        ms = jnp.mean(xv * xv, axis=-1, keepdims=True)
        y = xv * lax.rsqrt(ms + EPS) * g_ref[...]
        dst_ref[pl.ds(dst_off + r, NORM_ROWS), :] = y.astype(out_dtype)
        return carry
    lax.fori_loop(0, n_rows // NORM_ROWS, body, 0, unroll=NORM_UNROLL)


def _inproj_kernel(x_ref, g_ref, w_ref, w2_ref, wt_ref, glu_ref, k_ref, km_ref, q_ref, v_ref, gate_ref,
                   h_ref, *, seg):
    j = pl.program_id(1)
    n_glu, n_k, n_q, n_v, _ = seg
    k0 = n_glu
    q0 = k0 + n_k
    v0 = q0 + n_q
    g0 = v0 + n_v

    @pl.when(j == 0)
    def _():
        _rmsnorm_rows(x_ref, g_ref, h_ref, x_ref.shape[0])

    @pl.when(j < k0)
    def _():
        h = h_ref[...]
        a = jnp.dot(h, w_ref[...], preferred_element_type=F32)
        b = jnp.dot(h, w2_ref[...], preferred_element_type=F32)
        glu_ref[...] = a * jax.nn.sigmoid(b)

    @pl.when(jnp.logical_and(j >= k0, j < q0))
    def _():
        acc = jnp.dot(h_ref[...], w_ref[...], preferred_element_type=F32)
        k_ref[...] = acc.astype(k_ref.dtype)
        for b in range(acc.shape[0] // MOBA_BLOCK):
            blk = acc[b * MOBA_BLOCK:(b + 1) * MOBA_BLOCK]
            km_ref[0, b:b + 1, :] = jnp.mean(blk, axis=0, keepdims=True)

    @pl.when(jnp.logical_and(j >= q0, j < v0))
    def _():
        acc = lax.dot_general(wt_ref[...], h_ref[...], NT_DIMS, preferred_element_type=F32)
        q_ref[0] = acc.astype(q_ref.dtype)

    @pl.when(jnp.logical_and(j >= v0, j < g0))
    def _():
        acc = lax.dot_general(wt_ref[...], h_ref[...], NT_DIMS, preferred_element_type=F32)
        for grp in range(v_ref.shape[0]):
            pos = slice(grp * GROUP_KEYS, (grp + 1) * GROUP_KEYS)
            for hh in range(acc.shape[0] // HEAD_DIM):
                r = hh * V_ROWS
                v_ref[grp, r:r + HEAD_DIM, :] = acc[hh * HEAD_DIM:(hh + 1) * HEAD_DIM, pos].astype(v_ref.dtype)
                v_ref[grp, r + HEAD_DIM:r + V_ROWS, :] = jnp.ones((SUM_ROWS, GROUP_KEYS), v_ref.dtype)

    @pl.when(j >= g0)
    def _():
        acc = jnp.dot(h_ref[...], w_ref[...], preferred_element_type=F32)
        gate_ref[...] = jax.nn.sigmoid(acc).astype(gate_ref.dtype)


def _inproj_fused(x, g, w_in, wt_qv, tn=W_TILE):
    S, D = x.shape
    tm = INPROJ_ROWS
    nm = S // tm
    groups = tm // GROUP_KEYS
    seg = (D_CONV // tn, D_ATTN // tn, D_ATTN // tn, D_ATTN // tn, 2 * D_MODEL // tn)
    n_glu, n_k, n_q, n_v, n_g = seg
    k0, q0 = n_glu, n_glu + n_k
    v0 = q0 + n_q
    g0 = v0 + n_v
    col_k = (2 * D_CONV + D_ATTN) // tn
    col_g = (2 * D_CONV + 3 * D_ATTN) // tn

    def w_col(j):
        return jnp.where(j < k0, j, jnp.where(j < q0, col_k + (j - k0),
                         jnp.where(j < g0, col_k + n_k - 1, col_g + (j - g0))))

    seg_tile = lambda j, start, n: jnp.clip(j - start, 0, n - 1)
    v_tile_rows = tn // HEAD_DIM * V_ROWS
    glu, k, km, q_t, v_t, gates = pl.pallas_call(
        functools.partial(_inproj_kernel, seg=seg),
        grid=(nm, sum(seg)),
        in_specs=[pl.BlockSpec((tm, D), lambda i, j: (i, 0)),
                  pl.BlockSpec((1, D), lambda i, j: (0, 0)),
                  pl.BlockSpec((D, tn), lambda i, j: (0, w_col(j))),
                  pl.BlockSpec((D, tn), lambda i, j: (0, D_CONV // tn + seg_tile(j, 0, n_glu))),
                  pl.BlockSpec((tn, D), lambda i, j: (seg_tile(j, q0, n_q + n_v), 0))],
        out_specs=[pl.BlockSpec((tm, tn), lambda i, j: (i, seg_tile(j, 0, n_glu))),
                   pl.BlockSpec((tm, tn), lambda i, j: (i, seg_tile(j, k0, n_k))),
                   pl.BlockSpec((1, tm // MOBA_BLOCK, tn), lambda i, j: (i, 0, seg_tile(j, k0, n_k))),
                   pl.BlockSpec((1, tn, tm), lambda i, j: (i, seg_tile(j, q0, n_q), 0)),
                   pl.BlockSpec((groups, v_tile_rows, GROUP_KEYS), lambda i, j: (i, seg_tile(j, v0, n_v), 0)),
                   pl.BlockSpec((tm, tn), lambda i, j: (i, seg_tile(j, g0, n_g)))],
        out_shape=[jax.ShapeDtypeStruct((S, D_CONV), F32),
                   jax.ShapeDtypeStruct((S, D_ATTN), BF16),
                   jax.ShapeDtypeStruct((nm, tm // MOBA_BLOCK, D_ATTN), F32),
                   jax.ShapeDtypeStruct((nm, D_ATTN, tm), BF16),
                   jax.ShapeDtypeStruct((S // GROUP_KEYS, N_HEADS * V_ROWS, GROUP_KEYS), BF16),
                   jax.ShapeDtypeStruct((S, 2 * D_MODEL), BF16)],
        scratch_shapes=[pltpu.VMEM((tm, D), BF16)],
        compiler_params=_params(("parallel", "arbitrary")),
        name="inproj",
    )(x, g, w_in, w_in, wt_qv)
    return glu, k, km.reshape(S // MOBA_BLOCK, D_ATTN), q_t, v_t, gates


def _conv_kernel(halo_ref, x_ref, w_ref, b_ref, lg_ref, lb_ref, o_ref, win_ref, y_ref,
                 *, rows, lanes):
    tm, C = x_ref.shape
    i = pl.program_id(0)
    halo = halo_ref[...]
    win_ref[0:CONV_HALO, :] = jnp.where(i > 0, halo, jnp.zeros_like(halo))
    win_ref[CONV_HALO:, :] = x_ref[...]
    first = CONV_HALO - (CONV_K - 1)

    for c0 in range(0, C, lanes):
        def body(rc, carry):
            r = pl.multiple_of(rc * rows, rows)
            acc = jnp.zeros((rows, lanes), F32) + b_ref[:, c0:c0 + lanes]
            win = win_ref[pl.ds(r, rows + CONV_HALO), c0:c0 + lanes]
            for sub in range(SUBLANES):
                shifted = win if sub == 0 else pltpu.roll(win, win.shape[0] - sub, 0)
                usable = win.shape[0] - (SUBLANES if sub else 0)
                for base in range(0, usable - rows + 1, SUBLANES):
                    k = base + sub - first
                    if 0 <= k < CONV_K:
                        acc = acc + shifted[base:base + rows] * w_ref[k:k + 1, c0:c0 + lanes]
            y_ref[pl.ds(r, rows), c0:c0 + lanes] = acc
            return carry
        lax.fori_loop(0, tm // rows, body, 0)

    def ln_body(rc, carry):
        r = pl.multiple_of(rc * NORM_ROWS, NORM_ROWS)
        v = y_ref[pl.ds(r, NORM_ROWS), :]
        mu = jnp.mean(v, axis=-1, keepdims=True)
        var = jnp.mean(jnp.square(v - mu), axis=-1, keepdims=True)
        z = (v - mu) * lax.rsqrt(var + EPS) * lg_ref[...] + lb_ref[...]
        o_ref[pl.ds(r, NORM_ROWS), :] = jax.nn.silu(z).astype(o_ref.dtype)
        return carry
    lax.fori_loop(0, tm // NORM_ROWS, ln_body, 0, unroll=NORM_UNROLL)


def _conv_branch(glu, w, b, lg, lb, tm=256):
    S, C = glu.shape
    per = tm // CONV_HALO
    return pl.pallas_call(
        functools.partial(_conv_kernel, rows=32, lanes=256),
        grid=(S // tm,),
        in_specs=[pl.BlockSpec((CONV_HALO, C), lambda i: (jnp.maximum(i * per - 1, 0), 0)),
                  pl.BlockSpec((tm, C), lambda i: (i, 0)),
                  pl.BlockSpec((CONV_K, C), lambda i: (0, 0)),
                  pl.BlockSpec((1, C), lambda i: (0, 0)),
                  pl.BlockSpec((1, C), lambda i: (0, 0)),
                  pl.BlockSpec((1, C), lambda i: (0, 0))],
        out_specs=pl.BlockSpec((tm, C), lambda i: (i, 0)),
        out_shape=jax.ShapeDtypeStruct((S, C), BF16),
        scratch_shapes=[pltpu.VMEM((tm + CONV_HALO, C), F32), pltpu.VMEM((tm, C), F32)],
        compiler_params=_params(("parallel",)),
        name="conv_branch",
    )(glu, glu, w, b, lg, lb)


def _bias_kernel(rb_ref, o_ref):
    delta = pl.program_id(0)
    shape = (BIAS_ROWS, MOBA_BLOCK)

    def body(c, carry):
        c0 = pl.multiple_of(c * BIAS_ROWS, BIAS_ROWS)
        key = c0 + lax.broadcasted_iota(jnp.int32, shape, 0)
        d = delta * MOBA_BLOCK + lax.broadcasted_iota(jnp.int32, shape, 1) - key
        n = jnp.maximum(d, 0)
        max_exact = N_BUCKETS // 2
        nf = jnp.maximum(n, 1).astype(F32)
        large = max_exact + (jnp.log(nf / max_exact) / math.log(REL_MAX_DIST / max_exact)
                             * (N_BUCKETS - max_exact)).astype(jnp.int32)
        large = jnp.minimum(large, N_BUCKETS - 1)
        bucket = jnp.where(n < max_exact, n, large)
        vals = [jnp.zeros(shape, F32) for _ in range(N_HEADS)]
        for b in range(N_BUCKETS):
            hit = bucket == b
            vals = [jnp.where(hit, rb_ref[b, h], vals[h]) for h in range(N_HEADS)]
        for h in range(N_HEADS):
            val = jnp.where(delta == N_BIAS_TILES - 1, 0.0, vals[h] * LOG2E)
            o_ref[h, 0, pl.ds(c0, BIAS_ROWS), :] = jnp.where(d >= 0, val, NEG)
        return carry
    lax.fori_loop(0, MOBA_BLOCK // BIAS_ROWS, body, 0)


def _bias_tiles(rel_bias):
    return pl.pallas_call(
        _bias_kernel,
        grid=(N_BIAS_TILES,),
        in_specs=[pl.BlockSpec(memory_space=pltpu.SMEM)],
        out_specs=pl.BlockSpec((N_HEADS, 1, MOBA_BLOCK, MOBA_BLOCK), lambda d: (0, d, 0, 0)),
        out_shape=jax.ShapeDtypeStruct((N_HEADS, N_BIAS_TILES, MOBA_BLOCK, MOBA_BLOCK), F32),
        compiler_params=_params(("parallel",)),
        name="t5_bias_tiles",
    )(rel_bias)


def _attn_kernel(rb_ref, qt_ref, k_ref, vt_ref, km_ref, ind_ref, bias_ref, o_ref,
                 acc_ref, sa_ref, sb_ref, pa_ref, pb_ref):
    i = pl.program_id(1)
    nb = km_ref.shape[0]
    tq = qt_ref.shape[2]
    scale = HEAD_DIM ** -0.5
    heads = range(ATTN_HEADS)
    lanes = [slice(hh * HEAD_DIM, (hh + 1) * HEAD_DIM) for hh in heads]

    def widened_query(hh):
        qf = qt_ref[0, lanes[hh], :].astype(F32)
        gate = jnp.dot(km_ref[:, lanes[hh]], qf, precision=lax.Precision.HIGHEST, preferred_element_type=F32)
        row = lax.broadcasted_iota(jnp.int32, (nb, tq), 0)
        valid = row < i
        g = jnp.where(valid, gate, NEG)
        sel = jnp.zeros((nb, tq), jnp.bool_)
        for _ in range(MOBA_TOPK):
            top = jnp.max(g, axis=0, keepdims=True)
            idx = jnp.min(jnp.where(g == top, row, nb), axis=0, keepdims=True)
            hit = row == idx
            sel = jnp.logical_or(sel, hit)
            g = jnp.where(hit, -jnp.inf, g)
        picked = jnp.logical_and(sel, valid)
        far = jnp.logical_and(picked, row <= i - (N_BIAS_TILES - 1))
        far_bias = rb_ref[N_BUCKETS - 1, pl.program_id(0) * ATTN_HEADS + hh] * LOG2E
        pen = jnp.where(row == i, 0.0, jnp.where(picked, jnp.where(far, far_bias, 0.0), NEG))
        pen_hi = pen.astype(BF16)
        pen_lo = jnp.where(far, pen - pen_hi.astype(F32), 0.0).astype(BF16)
        return jnp.concatenate(
            [(qf * (scale * LOG2E)).astype(BF16), pen_hi, pen_lo, jnp.zeros((HEAD_DIM - 2 * nb, tq), BF16)],
            axis=0)

    q_aug = [widened_query(hh) for hh in heads]
    n_groups = i // KV_GROUP + 1

    def scores(grp, dst_ref):
        r0 = pl.multiple_of(grp * GROUP_KEYS, GROUP_KEYS)
        ind = ind_ref[pl.ds(r0, GROUP_KEYS), :]
        for hh in heads:
            k_aug = jnp.concatenate([k_ref[pl.ds(r0, GROUP_KEYS), lanes[hh]], ind], axis=1)
            dst_ref[hh] = jnp.dot(k_aug, q_aug[hh], preferred_element_type=F32)

    def softmax(grp, src_ref, dst_ref, m_prev):
        m_out, alpha_out = [], []
        for hh in heads:
            parts = []
            for u in range(KV_GROUP):
                delta = jnp.clip(i - (grp * KV_GROUP + u), 0, N_BIAS_TILES - 1)
                parts.append(src_ref[hh, u * MOBA_BLOCK:(u + 1) * MOBA_BLOCK, :] + bias_ref[hh, delta])
            s = jnp.concatenate(parts, axis=0)
            m_new = jnp.maximum(m_prev[hh], jnp.max(s, axis=0, keepdims=True))
            dst_ref[hh] = jnp.exp2(s - m_new).astype(BF16)
            m_out.append(m_new)
            alpha_out.append(jnp.exp2(m_prev[hh] - m_new))
        return tuple(m_out), tuple(alpha_out)

    def weighted_values(grp, live, src_ref, alpha):
        for hh in heads:
            pv = jnp.dot(vt_ref[grp, hh * V_ROWS:(hh + 1) * V_ROWS, :], src_ref[hh], preferred_element_type=F32)
            acc_ref[hh] = alpha[hh] * acc_ref[hh] + jnp.where(live, pv, 0.0)

    acc_ref[...] = jnp.zeros_like(acc_ref)
    pb_ref[...] = jnp.zeros_like(pb_ref)
    scores(n_groups - 1, sa_ref)

    def body(u, carry):
        m_prev, alpha_b, grp_b_prev = carry
        grp_a = n_groups - 1 - 2 * u
        grp_b = grp_a - 1
        weighted_values(jnp.maximum(grp_b_prev, 0), grp_b_prev >= 0, pb_ref, alpha_b)
        m_a, alpha_a = softmax(grp_a, sa_ref, pa_ref, m_prev)
        scores(jnp.maximum(grp_b, 0), sb_ref)
        weighted_values(grp_a, True, pa_ref, alpha_a)
        m_b, alpha_b = softmax(jnp.maximum(grp_b, 0), sb_ref, pb_ref, m_a)
        scores(jnp.maximum(grp_b - 1, 0), sa_ref)
        return m_b, alpha_b, grp_b

    m0 = tuple(jnp.full((1, tq), NEG, F32) for _ in heads)
    a0 = tuple(jnp.ones((1, tq), F32) for _ in heads)
    _, alpha_b, grp_b = lax.fori_loop(0, (n_groups + 1) // 2, body, (m0, a0, jnp.int32(-1)))
    weighted_values(jnp.maximum(grp_b, 0), grp_b >= 0, pb_ref, alpha_b)
    for hh in heads:
        acc = acc_ref[hh]
        o_ref[:, lanes[hh]] = (acc[:HEAD_DIM] / acc[HEAD_DIM:HEAD_DIM + 1]).T.astype(o_ref.dtype)


def _attention(k, km, q_t, v_t, bias, rel_bias):
    S = k.shape[0]
    nb = S // MOBA_BLOCK
    tq = MOBA_BLOCK
    per = INPROJ_ROWS // tq
    width = ATTN_HEADS * HEAD_DIM
    key_blk = jnp.arange(S, dtype=jnp.int32)[:, None] // MOBA_BLOCK
    lane = jnp.arange(HEAD_DIM, dtype=jnp.int32)[None, :]
    ind = jnp.logical_and(lane < 2 * nb, lane % nb == key_blk).astype(BF16)
    return pl.pallas_call(
        _attn_kernel,
        grid=(N_HEADS // ATTN_HEADS, S // tq),
        in_specs=[pl.BlockSpec(memory_space=pltpu.SMEM),
                  pl.BlockSpec((1, width, tq), lambda h, i: (i // per, h, i % per)),
                  pl.BlockSpec((S, width), lambda h, i: (0, h)),
                  pl.BlockSpec((S // GROUP_KEYS, ATTN_HEADS * V_ROWS, GROUP_KEYS), lambda h, i: (0, h, 0)),
                  pl.BlockSpec((nb, width), lambda h, i: (0, h)),
                  pl.BlockSpec((S, HEAD_DIM), lambda h, i: (0, 0)),
                  pl.BlockSpec((ATTN_HEADS, N_BIAS_TILES, MOBA_BLOCK, MOBA_BLOCK), lambda h, i: (h, 0, 0, 0))],
        out_specs=pl.BlockSpec((tq, width), lambda h, i: (i, h)),
        out_shape=jax.ShapeDtypeStruct((S, D_ATTN), BF16),
        scratch_shapes=[pltpu.VMEM((ATTN_HEADS, HEAD_DIM + SUM_ROWS, tq), F32),
                        pltpu.VMEM((ATTN_HEADS, GROUP_KEYS, tq), F32),
                        pltpu.VMEM((ATTN_HEADS, GROUP_KEYS, tq), F32),
                        pltpu.VMEM((ATTN_HEADS, GROUP_KEYS, tq), BF16),
                        pltpu.VMEM((ATTN_HEADS, GROUP_KEYS, tq), BF16)],
        compiler_params=_params(("parallel", "parallel")),
        name="moba_attention",
    )(rel_bias, q_t, k, v_t, km, ind, bias)


def _mix_kernel(cs_ref, at_ref, ga_ref, gb_ref, x_ref, wc_ref, wa_ref, wo_ref, o_ref):
    yc = jnp.dot(cs_ref[...], wc_ref[...], preferred_element_type=F32)
    ya = jnp.dot(at_ref[...], wa_ref[...], preferred_element_type=F32)
    merged = ga_ref[...].astype(F32) * yc + gb_ref[...].astype(F32) * ya
    o_ref[...] = x_ref[...] + jnp.dot(merged.astype(BF16), wo_ref[...], preferred_element_type=F32)


def _mix(cs, attn, gates, x, wc, wa, wo, tm=256):
    S, D = x.shape
    full = lambda shape: pl.BlockSpec(shape, lambda i: (0, 0))
    return pl.pallas_call(
        _mix_kernel,
        grid=(S // tm,),
        in_specs=[pl.BlockSpec((tm, D_CONV), lambda i: (i, 0)),
                  pl.BlockSpec((tm, D_ATTN), lambda i: (i, 0)),
                  pl.BlockSpec((tm, D), lambda i: (i, 0)),
                  pl.BlockSpec((tm, D), lambda i: (i, 1)),
                  pl.BlockSpec((tm, D), lambda i: (i, 0)),
                  full((D_CONV, D)), full((D_ATTN, D)), full((D, D))],
        out_specs=pl.BlockSpec((tm, D), lambda i: (i, 0)),
        out_shape=jax.ShapeDtypeStruct((S, D), F32),
        compiler_params=_params(("parallel",)),
        name="mix_out_proj",
    )(cs, attn, gates, gates, x, wc, wa, wo)


def _ffn_kernel(xh_ref, x_ref, g_ref, wa_ref, wb_ref, dwa_ref, dwb_ref, ba_ref, bb_ref, wd_ref,
                fg_ref, o_ref, h_ref, gate_ref, *, final_norm, n_tiles, up_chunk):
    i = pl.program_id(0)
    f = pl.program_id(1)
    tm = x_ref.shape[0]
    tf = wa_ref.shape[1]
    slot = f % 2

    def up_stage():
        h = h_ref[...]
        for c0 in range(0, tf, up_chunk):
            cols = slice(c0, c0 + up_chunk)

            def up_conv(w_ref, dw_ref, b_ref):
                u = jnp.dot(h, w_ref[:, cols], preferred_element_type=F32)
                y = (u * dw_ref[2:3, cols] + pltpu.roll(u, 1, 0) * dw_ref[1:2, cols]
                     + pltpu.roll(u, 2, 0) * dw_ref[0:1, cols] + b_ref[:, cols])
                return y[HALO:]

            ua = up_conv(wa_ref, dwa_ref, ba_ref)
            ub = up_conv(wb_ref, dwb_ref, bb_ref)
            gate_ref[slot, :, cols] = (jax.nn.silu(ub) * ua).astype(BF16)

    def down_product():
        return jnp.dot(gate_ref[1 - slot], wd_ref[...], preferred_element_type=F32)

    @pl.when(f == 0)
    def _():
        xh = xh_ref[...]
        ms = jnp.mean(xh * xh, axis=-1, keepdims=True)
        hh = xh * lax.rsqrt(ms + EPS) * g_ref[...]
        h_ref[0:HALO, :] = jnp.where(i > 0, hh, jnp.zeros_like(hh)).astype(BF16)
        _rmsnorm_rows(x_ref, g_ref, h_ref, tm, dst_off=HALO)
        o_ref[...] = jnp.zeros_like(o_ref)
        up_stage()

    @pl.when(jnp.logical_and(f > 0, f < n_tiles))
    def _():
        o_ref[...] += down_product()
        up_stage()

    @pl.when(f == n_tiles)
    def _():
        y = x_ref[...] + o_ref[...] + down_product()
        if final_norm:
            ms = jnp.mean(y * y, axis=-1, keepdims=True)
            y = y * lax.rsqrt(ms + EPS) * fg_ref[...]
        o_ref[...] = y


def _ffn(x, g, w_up, dw, dwb, w_down, final_g, final_norm, tm=1024, tf=W_TILE):
    S, D = x.shape
    nf = D_FF // tf
    per = tm // HALO
    row = lambda shape: pl.BlockSpec(shape, lambda i, f: (0, 0))
    up = lambda f: jnp.minimum(f, nf - 1)
    down = lambda f: jnp.maximum(f - 1, 0)
    return pl.pallas_call(
        functools.partial(_ffn_kernel, final_norm=final_norm, n_tiles=nf, up_chunk=tf // 2),
        grid=(S // tm, nf + 1),
        in_specs=[pl.BlockSpec((HALO, D), lambda i, f: (jnp.maximum(i * per - 1, 0), 0)),
                  pl.BlockSpec((tm, D), lambda i, f: (i, 0), pipeline_mode=pl.Buffered(1)),
                  row((1, D)),
                  pl.BlockSpec((D, tf), lambda i, f: (0, up(f))),
                  pl.BlockSpec((D, tf), lambda i, f: (0, up(f) + nf)),
                  pl.BlockSpec((FFN_CONV_K, tf), lambda i, f: (0, up(f))),
                  pl.BlockSpec((FFN_CONV_K, tf), lambda i, f: (0, up(f) + nf)),
                  pl.BlockSpec((1, tf), lambda i, f: (0, up(f))),
                  pl.BlockSpec((1, tf), lambda i, f: (0, up(f) + nf)),
                  pl.BlockSpec((tf, D), lambda i, f: (down(f), 0)),
                  row((1, D))],
        out_specs=pl.BlockSpec((tm, D), lambda i, f: (i, 0)),
        out_shape=jax.ShapeDtypeStruct((S, D), F32),
        scratch_shapes=[pltpu.VMEM((HALO + tm, D), BF16), pltpu.VMEM((2, tm, tf), BF16)],
        compiler_params=_params(("parallel", "arbitrary")),
        name="conv_ffn",
    )(x, x, g, w_up, w_up, dw, dw, dwb, dwb, w_down, final_g)


def kernel(x, norm1_g, w_in, conv_dw, conv_dw_b, conv_ln_g, conv_ln_b, w_conv_out, rel_bias,
           w_attn_out, w_out, norm2_g, w_up, ffn_dw, ffn_dw_b, w_down, final_g):
    B, S, D = x.shape
    assert D == D_MODEL and S % INPROJ_ROWS == 0 and INPROJ_ROWS % GROUP_KEYS == 0
    depth = w_in.shape[0]
    bias = _bias_tiles(rel_bias)
    row = lambda v: v.reshape(1, -1)
    q0 = 2 * D_CONV
    v0 = q0 + 2 * D_ATTN
    outs = []
    for b in range(B):
        xb = x.reshape(S, D) if B == 1 else x[b]
        for l in range(depth):
            w_l = w_in[l]
            wt_qv = jnp.concatenate([w_l[:, q0:q0 + D_ATTN], w_l[:, v0:v0 + D_ATTN]], axis=1).T.astype(BF16)
            glu, k, km, q_t, v_t, gates = _inproj_fused(xb, row(norm1_g[l]), w_l.astype(BF16), wt_qv)
            cs = _conv_branch(glu, conv_dw[l], row(conv_dw_b[l]), row(conv_ln_g[l]), row(conv_ln_b[l]))
            attn = _attention(k, km, q_t, v_t, bias, rel_bias)
            xb = _mix(cs, attn, gates, xb, w_conv_out[l].astype(BF16), w_attn_out[l].astype(BF16),
                      w_out[l].astype(BF16))
            xb = _ffn(xb, row(norm2_g[l]), w_up[l].astype(BF16), ffn_dw[l], row(ffn_dw_b[l]),
                      w_down[l].astype(BF16), row(final_g), final_norm=(l == depth - 1))
        outs.append(xb)
    return outs[0].reshape(1, S, D) if B == 1 else jnp.stack(outs, axis=0)
```

```python
import functools
import math

import jax
import jax.numpy as jnp
from jax import lax
from jax.experimental import pallas as pl
from jax.experimental.pallas import tpu as pltpu

F32 = jnp.float32
BF16 = jnp.bfloat16

D_MODEL = 2048
D_CONV = 1024
CONV_K = 31
N_HEADS = 8
HEAD_DIM = 128
D_ATTN = N_HEADS * HEAD_DIM
MOBA_BLOCK = 256
MOBA_TOPK = 3
N_BUCKETS = 32
REL_MAX_DIST = 2048
D_FF = 5632
FFN_CONV_K = 3
EPS = 1e-6
NEG = -1e30
LOG2E = math.log2(math.e)

N_BIAS_TILES = 8
BIAS_ROWS = 16
KV_GROUP = 2
GROUP_KEYS = KV_GROUP * MOBA_BLOCK
INPROJ_ROWS = 1024
SUM_ROWS = 16
V_ROWS = HEAD_DIM + SUM_ROWS
ATTN_HEADS = 2

V7X_VMEM_BYTES = 64 * 1024 * 1024
VMEM_LIMIT = V7X_VMEM_BYTES - 8 * 1024 * 1024

SUBLANES = 8
NORM_ROWS = 16
NORM_UNROLL = 4
HALO = 8
CONV_HALO = 32

W_TILE = 512

NT_DIMS = (((1,), (1,)), ((), ()))


def _params(sem):
    return pltpu.CompilerParams(dimension_semantics=sem, vmem_limit_bytes=VMEM_LIMIT)


def _rmsnorm_rows(x_ref, g_ref, dst_ref, n_rows, dst_off=0, out_dtype=BF16):
    def body(c, carry):
        r = pl.multiple_of(c * NORM_ROWS, NORM_ROWS)
        xv = x_ref[pl.ds(r, NORM_ROWS), :]
        ms = jnp.mean(xv * xv, axis=-1, keepdims=True)
        y = xv * lax.rsqrt(ms + EPS) * g_ref[...]
        dst_ref[pl.ds(dst_off + r, NORM_ROWS), :] = y.astype(out_dtype)
        return carry
    lax.fori_loop(0, n_rows // NORM_ROWS, body, 0, unroll=NORM_UNROLL)


def _tile_tails(a, tm, n):
    S, C = a.shape
    return a.reshape(S // tm, tm, C)[:, tm - n:, :]


def _inproj_kernel(x_ref, g_ref, w_ref, w2_ref, wt_ref, glu_ref, k_ref, km_ref, q_ref, v_ref, gate_ref,
                   h_ref, *, seg):
    j = pl.program_id(1)
    n_glu, n_k, n_q, n_v, _ = seg
    k0 = n_glu
    q0 = k0 + n_k
    v0 = q0 + n_q
    g0 = v0 + n_v

    @pl.when(j == 0)
    def _():
        _rmsnorm_rows(x_ref, g_ref, h_ref, x_ref.shape[0])

    @pl.when(j < k0)
    def _():
        h = h_ref[...]
        a = jnp.dot(h, w_ref[...], preferred_element_type=F32)
        b = jnp.dot(h, w2_ref[...], preferred_element_type=F32)
        glu_ref[...] = a * jax.nn.sigmoid(b)

    @pl.when(jnp.logical_and(j >= k0, j < q0))
    def _():
        acc = jnp.dot(h_ref[...], w_ref[...], preferred_element_type=F32)
        k_ref[...] = acc.astype(k_ref.dtype)
        for b in range(acc.shape[0] // MOBA_BLOCK):
            blk = acc[b * MOBA_BLOCK:(b + 1) * MOBA_BLOCK]
            km_ref[0, b:b + 1, :] = jnp.mean(blk, axis=0, keepdims=True)

    @pl.when(jnp.logical_and(j >= q0, j < v0))
    def _():
        acc = lax.dot_general(wt_ref[...], h_ref[...], NT_DIMS, preferred_element_type=F32)
        q_ref[0] = acc.astype(q_ref.dtype)

    @pl.when(jnp.logical_and(j >= v0, j < g0))
    def _():
        acc = lax.dot_general(wt_ref[...], h_ref[...], NT_DIMS, preferred_element_type=F32)
        for grp in range(v_ref.shape[0]):
            pos = slice(grp * GROUP_KEYS, (grp + 1) * GROUP_KEYS)
            for hh in range(acc.shape[0] // HEAD_DIM):
                r = hh * V_ROWS
                v_ref[grp, r:r + HEAD_DIM, :] = acc[hh * HEAD_DIM:(hh + 1) * HEAD_DIM, pos].astype(v_ref.dtype)
                v_ref[grp, r + HEAD_DIM:r + V_ROWS, :] = jnp.ones((SUM_ROWS, GROUP_KEYS), v_ref.dtype)

    @pl.when(j >= g0)
    def _():
        acc = jnp.dot(h_ref[...], w_ref[...], preferred_element_type=F32)
        gate_ref[...] = jax.nn.sigmoid(acc).astype(gate_ref.dtype)


def _inproj_fused(x, g, w_in, wt_qv, tn=W_TILE):
    S, D = x.shape
    tm = INPROJ_ROWS
    nm = S // tm
    groups = tm // GROUP_KEYS
    seg = (D_CONV // tn, D_ATTN // tn, D_ATTN // tn, D_ATTN // tn, 2 * D_MODEL // tn)
    n_glu, n_k, n_q, n_v, n_g = seg
    k0, q0 = n_glu, n_glu + n_k
    v0 = q0 + n_q
    g0 = v0 + n_v
    col_k = (2 * D_CONV + D_ATTN) // tn
    col_g = (2 * D_CONV + 3 * D_ATTN) // tn

    def w_col(j):
        return jnp.where(j < k0, j, jnp.where(j < q0, col_k + (j - k0),
                         jnp.where(j < g0, col_k + n_k - 1, col_g + (j - g0))))

    seg_tile = lambda j, start, n: jnp.clip(j - start, 0, n - 1)
    v_tile_rows = tn // HEAD_DIM * V_ROWS
    glu, k, km, q_t, v_t, gates = pl.pallas_call(
        functools.partial(_inproj_kernel, seg=seg),
        grid=(nm, sum(seg)),
        in_specs=[pl.BlockSpec((tm, D), lambda i, j: (i, 0)),
                  pl.BlockSpec((1, D), lambda i, j: (0, 0)),
                  pl.BlockSpec((D, tn), lambda i, j: (0, w_col(j))),
                  pl.BlockSpec((D, tn), lambda i, j: (0, D_CONV // tn + seg_tile(j, 0, n_glu))),
                  pl.BlockSpec((tn, D), lambda i, j: (seg_tile(j, q0, n_q + n_v), 0))],
        out_specs=[pl.BlockSpec((tm, tn), lambda i, j: (i, seg_tile(j, 0, n_glu))),
                   pl.BlockSpec((tm, tn), lambda i, j: (i, seg_tile(j, k0, n_k))),
                   pl.BlockSpec((1, tm // MOBA_BLOCK, tn), lambda i, j: (i, 0, seg_tile(j, k0, n_k))),
                   pl.BlockSpec((1, tn, tm), lambda i, j: (i, seg_tile(j, q0, n_q), 0)),
                   pl.BlockSpec((groups, v_tile_rows, GROUP_KEYS), lambda i, j: (i, seg_tile(j, v0, n_v), 0)),
                   pl.BlockSpec((tm, tn), lambda i, j: (i, seg_tile(j, g0, n_g)))],
        out_shape=[jax.ShapeDtypeStruct((S, D_CONV), F32),
                   jax.ShapeDtypeStruct((S, D_ATTN), BF16),
                   jax.ShapeDtypeStruct((nm, tm // MOBA_BLOCK, D_ATTN), F32),
                   jax.ShapeDtypeStruct((nm, D_ATTN, tm), BF16),
                   jax.ShapeDtypeStruct((S // GROUP_KEYS, N_HEADS * V_ROWS, GROUP_KEYS), BF16),
                   jax.ShapeDtypeStruct((S, 2 * D_MODEL), BF16)],
        scratch_shapes=[pltpu.VMEM((tm, D), BF16)],
        compiler_params=_params(("parallel", "arbitrary")),
        name="inproj",
    )(x, g, w_in, w_in, wt_qv)
    return glu, k, km.reshape(S // MOBA_BLOCK, D_ATTN), q_t, v_t, gates


def _conv_kernel(halo_ref, x_ref, w_ref, b_ref, lg_ref, lb_ref, o_ref, win_ref, y_ref,
                 *, rows, lanes):
    tm, C = x_ref.shape
    i = pl.program_id(0)
    halo = halo_ref[0]
    win_ref[0:CONV_HALO, :] = jnp.where(i > 0, halo, jnp.zeros_like(halo))
    win_ref[CONV_HALO:, :] = x_ref[...]
    first = CONV_HALO - (CONV_K - 1)

    for c0 in range(0, C, lanes):
        def body(rc, carry):
            r = pl.multiple_of(rc * rows, rows)
            acc = jnp.zeros((rows, lanes), F32) + b_ref[:, c0:c0 + lanes]
            win = win_ref[pl.ds(r, rows + CONV_HALO), c0:c0 + lanes]
            for sub in range(SUBLANES):
                shifted = win if sub == 0 else pltpu.roll(win, win.shape[0] - sub, 0)
                usable = win.shape[0] - (SUBLANES if sub else 0)
                for base in range(0, usable - rows + 1, SUBLANES):
                    k = base + sub - first
                    if 0 <= k < CONV_K:
                        acc = acc + shifted[base:base + rows] * w_ref[k:k + 1, c0:c0 + lanes]
            y_ref[pl.ds(r, rows), c0:c0 + lanes] = acc
            return carry
        lax.fori_loop(0, tm // rows, body, 0)

    def ln_body(rc, carry):
        r = pl.multiple_of(rc * NORM_ROWS, NORM_ROWS)
        v = y_ref[pl.ds(r, NORM_ROWS), :]
        mu = jnp.mean(v, axis=-1, keepdims=True)
        var = jnp.mean(jnp.square(v - mu), axis=-1, keepdims=True)
        z = (v - mu) * lax.rsqrt(var + EPS) * lg_ref[...] + lb_ref[...]
        o_ref[pl.ds(r, NORM_ROWS), :] = jax.nn.silu(z).astype(o_ref.dtype)
        return carry
    lax.fori_loop(0, tm // NORM_ROWS, ln_body, 0, unroll=NORM_UNROLL)


def _conv_branch(glu, w, b, lg, lb, tm=256):
    S, C = glu.shape
    tails = _tile_tails(glu, tm, CONV_HALO)
    return pl.pallas_call(
        functools.partial(_conv_kernel, rows=32, lanes=256),
        grid=(S // tm,),
        in_specs=[pl.BlockSpec((1, CONV_HALO, C), lambda i: (jnp.maximum(i - 1, 0), 0, 0)),
                  pl.BlockSpec((tm, C), lambda i: (i, 0)),
                  pl.BlockSpec((CONV_K, C), lambda i: (0, 0)),
                  pl.BlockSpec((1, C), lambda i: (0, 0)),
                  pl.BlockSpec((1, C), lambda i: (0, 0)),
                  pl.BlockSpec((1, C), lambda i: (0, 0))],
        out_specs=pl.BlockSpec((tm, C), lambda i: (i, 0)),
        out_shape=jax.ShapeDtypeStruct((S, C), BF16),
        scratch_shapes=[pltpu.VMEM((tm + CONV_HALO, C), F32), pltpu.VMEM((tm, C), F32)],
        compiler_params=_params(("parallel",)),
        name="conv_branch",
    )(tails, glu, w, b, lg, lb)


def _bias_kernel(rb_ref, o_ref):
    delta = pl.program_id(0)
    shape = (BIAS_ROWS, MOBA_BLOCK)

    def body(c, carry):
        c0 = pl.multiple_of(c * BIAS_ROWS, BIAS_ROWS)
        key = c0 + lax.broadcasted_iota(jnp.int32, shape, 0)
        d = delta * MOBA_BLOCK + lax.broadcasted_iota(jnp.int32, shape, 1) - key
        n = jnp.maximum(d, 0)
        max_exact = N_BUCKETS // 2
        nf = jnp.maximum(n, 1).astype(F32)
        large = max_exact + (jnp.log(nf / max_exact) / math.log(REL_MAX_DIST / max_exact)
                             * (N_BUCKETS - max_exact)).astype(jnp.int32)
        large = jnp.minimum(large, N_BUCKETS - 1)
        bucket = jnp.where(n < max_exact, n, large)
        vals = [jnp.zeros(shape, F32) for _ in range(N_HEADS)]
        for b in range(N_BUCKETS):
            hit = bucket == b
            vals = [jnp.where(hit, rb_ref[b, h], vals[h]) for h in range(N_HEADS)]
        for h in range(N_HEADS):
            val = jnp.where(delta == N_BIAS_TILES - 1, 0.0, vals[h] * LOG2E)
            o_ref[h, 0, pl.ds(c0, BIAS_ROWS), :] = jnp.where(d >= 0, val, NEG)
        return carry
    lax.fori_loop(0, MOBA_BLOCK // BIAS_ROWS, body, 0)


def _bias_tiles(rel_bias):
    return pl.pallas_call(
        _bias_kernel,
        grid=(N_BIAS_TILES,),
        in_specs=[pl.BlockSpec(memory_space=pltpu.SMEM)],
        out_specs=pl.BlockSpec((N_HEADS, 1, MOBA_BLOCK, MOBA_BLOCK), lambda d: (0, d, 0, 0)),
        out_shape=jax.ShapeDtypeStruct((N_HEADS, N_BIAS_TILES, MOBA_BLOCK, MOBA_BLOCK), F32),
        compiler_params=_params(("parallel",)),
        name="t5_bias_tiles",
    )(rel_bias)


def _attn_kernel(rb_ref, qt_ref, k_ref, vt_ref, km_ref, ind_ref, bias_ref, o_ref,
                 acc_ref, sa_ref, sb_ref, pa_ref, pb_ref):
    i = pl.program_id(1)
    nb = km_ref.shape[0]
    tq = qt_ref.shape[2]
    scale = HEAD_DIM ** -0.5
    heads = range(ATTN_HEADS)
    lanes = [slice(hh * HEAD_DIM, (hh + 1) * HEAD_DIM) for hh in heads]

    def widened_query(hh):
        qf = qt_ref[0, lanes[hh], :].astype(F32)
        gate = jnp.dot(km_ref[:, lanes[hh]], qf, precision=lax.Precision.HIGHEST, preferred_element_type=F32)
        row = lax.broadcasted_iota(jnp.int32, (nb, tq), 0)
        valid = row < i
        g = jnp.where(valid, gate, NEG)
        sel = jnp.zeros((nb, tq), jnp.bool_)
        for _ in range(MOBA_TOPK):
            top = jnp.max(g, axis=0, keepdims=True)
            idx = jnp.min(jnp.where(g == top, row, nb), axis=0, keepdims=True)
            hit = row == idx
            sel = jnp.logical_or(sel, hit)
            g = jnp.where(hit, -jnp.inf, g)
        picked = jnp.logical_and(sel, valid)
        far = jnp.logical_and(picked, row <= i - (N_BIAS_TILES - 1))
        far_bias = rb_ref[N_BUCKETS - 1, pl.program_id(0) * ATTN_HEADS + hh] * LOG2E
        pen = jnp.where(row == i, 0.0, jnp.where(picked, jnp.where(far, far_bias, 0.0), NEG))
        pen_hi = pen.astype(BF16)
        pen_lo = jnp.where(far, pen - pen_hi.astype(F32), 0.0).astype(BF16)
        return jnp.concatenate(
            [(qf * (scale * LOG2E)).astype(BF16), pen_hi, pen_lo, jnp.zeros((HEAD_DIM - 2 * nb, tq), BF16)],
            axis=0)

    q_aug = [widened_query(hh) for hh in heads]
    n_groups = i // KV_GROUP + 1

    def scores(grp, dst_ref):
        r0 = pl.multiple_of(grp * GROUP_KEYS, GROUP_KEYS)
        ind = ind_ref[pl.ds(r0, GROUP_KEYS), :]
        for hh in heads:
            k_aug = jnp.concatenate([k_ref[pl.ds(r0, GROUP_KEYS), lanes[hh]], ind], axis=1)
            dst_ref[hh] = jnp.dot(k_aug, q_aug[hh], preferred_element_type=F32)

    def softmax(grp, src_ref, dst_ref, m_prev):
        m_out, alpha_out = [], []
        for hh in heads:
            parts = []
            for u in range(KV_GROUP):
                delta = jnp.clip(i - (grp * KV_GROUP + u), 0, N_BIAS_TILES - 1)
                parts.append(src_ref[hh, u * MOBA_BLOCK:(u + 1) * MOBA_BLOCK, :] + bias_ref[hh, delta])
            s = jnp.concatenate(parts, axis=0)
            m_new = jnp.maximum(m_prev[hh], jnp.max(s, axis=0, keepdims=True))
            dst_ref[hh] = jnp.exp2(s - m_new).astype(BF16)
            m_out.append(m_new)
            alpha_out.append(jnp.exp2(m_prev[hh] - m_new))
        return tuple(m_out), tuple(alpha_out)

    def weighted_values(grp, live, src_ref, alpha):
        for hh in heads:
            pv = jnp.dot(vt_ref[grp, hh * V_ROWS:(hh + 1) * V_ROWS, :], src_ref[hh], preferred_element_type=F32)
            acc_ref[hh] = alpha[hh] * acc_ref[hh] + jnp.where(live, pv, 0.0)

    acc_ref[...] = jnp.zeros_like(acc_ref)
    pb_ref[...] = jnp.zeros_like(pb_ref)
    scores(n_groups - 1, sa_ref)

    def body(u, carry):
        m_prev, alpha_b, grp_b_prev = carry
        grp_a = n_groups - 1 - 2 * u
        grp_b = grp_a - 1
        weighted_values(jnp.maximum(grp_b_prev, 0), grp_b_prev >= 0, pb_ref, alpha_b)
        m_a, alpha_a = softmax(grp_a, sa_ref, pa_ref, m_prev)
        scores(jnp.maximum(grp_b, 0), sb_ref)
        weighted_values(grp_a, True, pa_ref, alpha_a)
        m_b, alpha_b = softmax(jnp.maximum(grp_b, 0), sb_ref, pb_ref, m_a)
        scores(jnp.maximum(grp_b - 1, 0), sa_ref)
        return m_b, alpha_b, grp_b

    m0 = tuple(jnp.full((1, tq), NEG, F32) for _ in heads)
    a0 = tuple(jnp.ones((1, tq), F32) for _ in heads)
    _, alpha_b, grp_b = lax.fori_loop(0, (n_groups + 1) // 2, body, (m0, a0, jnp.int32(-1)))
    weighted_values(jnp.maximum(grp_b, 0), grp_b >= 0, pb_ref, alpha_b)
    for hh in heads:
        acc = acc_ref[hh]
        o_ref[:, lanes[hh]] = (acc[:HEAD_DIM] / acc[HEAD_DIM:HEAD_DIM + 1]).T.astype(o_ref.dtype)


def _attention(k, km, q_t, v_t, bias, rel_bias):
    S = k.shape[0]
    nb = S // MOBA_BLOCK
    tq = MOBA_BLOCK
    per = INPROJ_ROWS // tq
    width = ATTN_HEADS * HEAD_DIM
    key_blk = jnp.arange(S, dtype=jnp.int32)[:, None] // MOBA_BLOCK
    lane = jnp.arange(HEAD_DIM, dtype=jnp.int32)[None, :]
    ind = jnp.logical_and(lane < 2 * nb, lane % nb == key_blk).astype(BF16)
    return pl.pallas_call(
        _attn_kernel,
        grid=(N_HEADS // ATTN_HEADS, S // tq),
        in_specs=[pl.BlockSpec(memory_space=pltpu.SMEM),
                  pl.BlockSpec((1, width, tq), lambda h, i: (i // per, h, i % per)),
                  pl.BlockSpec((S, width), lambda h, i: (0, h)),
                  pl.BlockSpec((S // GROUP_KEYS, ATTN_HEADS * V_ROWS, GROUP_KEYS), lambda h, i: (0, h, 0)),
                  pl.BlockSpec((nb, width), lambda h, i: (0, h)),
                  pl.BlockSpec((S, HEAD_DIM), lambda h, i: (0, 0)),
                  pl.BlockSpec((ATTN_HEADS, N_BIAS_TILES, MOBA_BLOCK, MOBA_BLOCK), lambda h, i: (h, 0, 0, 0))],
        out_specs=pl.BlockSpec((tq, width), lambda h, i: (i, h)),
        out_shape=jax.ShapeDtypeStruct((S, D_ATTN), BF16),
        scratch_shapes=[pltpu.VMEM((ATTN_HEADS, HEAD_DIM + SUM_ROWS, tq), F32),
                        pltpu.VMEM((ATTN_HEADS, GROUP_KEYS, tq), F32),
                        pltpu.VMEM((ATTN_HEADS, GROUP_KEYS, tq), F32),
                        pltpu.VMEM((ATTN_HEADS, GROUP_KEYS, tq), BF16),
                        pltpu.VMEM((ATTN_HEADS, GROUP_KEYS, tq), BF16)],
        compiler_params=_params(("parallel", "parallel")),
        name="moba_attention",
    )(rel_bias, q_t, k, v_t, km, ind, bias)


def _mix_kernel(cs_ref, at_ref, gate_ref, x_ref, wc_ref, wa_ref, wo_ref, o_ref):
    D = x_ref.shape[1]
    yc = jnp.dot(cs_ref[...], wc_ref[...], preferred_element_type=F32)
    ya = jnp.dot(at_ref[...], wa_ref[...], preferred_element_type=F32)
    merged = gate_ref[:, :D].astype(F32) * yc + gate_ref[:, D:].astype(F32) * ya
    o_ref[...] = x_ref[...] + jnp.dot(merged.astype(BF16), wo_ref[...], preferred_element_type=F32)


def _mix(cs, attn, gates, x, wc, wa, wo, tm=256):
    S, D = x.shape
    full = lambda shape: pl.BlockSpec(shape, lambda i: (0, 0))
    return pl.pallas_call(
        _mix_kernel,
        grid=(S // tm,),
        in_specs=[pl.BlockSpec((tm, D_CONV), lambda i: (i, 0)),
                  pl.BlockSpec((tm, D_ATTN), lambda i: (i, 0)),
                  pl.BlockSpec((tm, 2 * D), lambda i: (i, 0)),
                  pl.BlockSpec((tm, D), lambda i: (i, 0)),
                  full((D_CONV, D)), full((D_ATTN, D)), full((D, D))],
        out_specs=pl.BlockSpec((tm, D), lambda i: (i, 0)),
        out_shape=jax.ShapeDtypeStruct((S, D), F32),
        compiler_params=_params(("parallel",)),
        name="mix_out_proj",
    )(cs, attn, gates, x, wc, wa, wo)


def _ffn_kernel(xh_ref, x_ref, g_ref, wa_ref, wb_ref, dwa_ref, dwb_ref, ba_ref, bb_ref, wd_ref,
                fg_ref, o_ref, h_ref, gate_ref, *, final_norm, n_tiles, up_chunk):
    i = pl.program_id(0)
    f = pl.program_id(1)
    tm = x_ref.shape[0]
    tf = wa_ref.shape[1]
    slot = f % 2

    def up_stage():
        h = h_ref[...]
        for c0 in range(0, tf, up_chunk):
            cols = slice(c0, c0 + up_chunk)

            def up_conv(w_ref, dw_ref, b_ref):
                u = jnp.dot(h, w_ref[:, cols], preferred_element_type=F32)
                y = (u * dw_ref[2:3, cols] + pltpu.roll(u, 1, 0) * dw_ref[1:2, cols]
                     + pltpu.roll(u, 2, 0) * dw_ref[0:1, cols] + b_ref[:, cols])
                return y[HALO:]

            ua = up_conv(wa_ref, dwa_ref, ba_ref)
            ub = up_conv(wb_ref, dwb_ref, bb_ref)
            gate_ref[slot, :, cols] = (jax.nn.silu(ub) * ua).astype(BF16)

    def down_product():
        return jnp.dot(gate_ref[1 - slot], wd_ref[...], preferred_element_type=F32)

    @pl.when(f == 0)
    def _():
        xh = xh_ref[0]
        ms = jnp.mean(xh * xh, axis=-1, keepdims=True)
        hh = xh * lax.rsqrt(ms + EPS) * g_ref[...]
        h_ref[0:HALO, :] = jnp.where(i > 0, hh, jnp.zeros_like(hh)).astype(BF16)
        _rmsnorm_rows(x_ref, g_ref, h_ref, tm, dst_off=HALO)
        o_ref[...] = jnp.zeros_like(o_ref)
        up_stage()

    @pl.when(jnp.logical_and(f > 0, f < n_tiles))
    def _():
        o_ref[...] += down_product()
        up_stage()

    @pl.when(f == n_tiles)
    def _():
        y = x_ref[...] + o_ref[...] + down_product()
        if final_norm:
            ms = jnp.mean(y * y, axis=-1, keepdims=True)
            y = y * lax.rsqrt(ms + EPS) * fg_ref[...]
        o_ref[...] = y


def _ffn(x, g, w_up, dw, dwb, w_down, final_g, final_norm, tm=1024, tf=W_TILE):
    S, D = x.shape
    nf = D_FF // tf
    tails = _tile_tails(x, tm, HALO)
    row = lambda shape: pl.BlockSpec(shape, lambda i, f: (0, 0))
    up = lambda f: jnp.minimum(f, nf - 1)
    down = lambda f: jnp.maximum(f - 1, 0)
    return pl.pallas_call(
        functools.partial(_ffn_kernel, final_norm=final_norm, n_tiles=nf, up_chunk=tf // 2),
        grid=(S // tm, nf + 1),
        in_specs=[pl.BlockSpec((1, HALO, D), lambda i, f: (jnp.maximum(i - 1, 0), 0, 0)),
                  pl.BlockSpec((tm, D), lambda i, f: (i, 0), pipeline_mode=pl.Buffered(1)),
                  row((1, D)),
                  pl.BlockSpec((D, tf), lambda i, f: (0, up(f))),
                  pl.BlockSpec((D, tf), lambda i, f: (0, up(f) + nf)),
                  pl.BlockSpec((FFN_CONV_K, tf), lambda i, f: (0, up(f))),
                  pl.BlockSpec((FFN_CONV_K, tf), lambda i, f: (0, up(f) + nf)),
                  pl.BlockSpec((1, tf), lambda i, f: (0, up(f))),
                  pl.BlockSpec((1, tf), lambda i, f: (0, up(f) + nf)),
                  pl.BlockSpec((tf, D), lambda i, f: (down(f), 0)),
                  row((1, D))],
        out_specs=pl.BlockSpec((tm, D), lambda i, f: (i, 0)),
        out_shape=jax.ShapeDtypeStruct((S, D), F32),
        scratch_shapes=[pltpu.VMEM((HALO + tm, D), BF16), pltpu.VMEM((2, tm, tf), BF16)],
        compiler_params=_params(("parallel", "arbitrary")),
        name="conv_ffn",
    )(tails, x, g, w_up, w_up, dw, dw, dwb, dwb, w_down, final_g)


def kernel(x, norm1_g, w_in, conv_dw, conv_dw_b, conv_ln_g, conv_ln_b, w_conv_out, rel_bias,
           w_attn_out, w_out, norm2_g, w_up, ffn_dw, ffn_dw_b, w_down, final_g):
    B, S, D = x.shape
    assert D == D_MODEL and S % INPROJ_ROWS == 0 and INPROJ_ROWS % GROUP_KEYS == 0
    depth = w_in.shape[0]
    bias = _bias_tiles(rel_bias)
    row = lambda v: v.reshape(1, -1)
    q0 = 2 * D_CONV
    v0 = q0 + 2 * D_ATTN
    outs = []
    for b in range(B):
        xb = x.reshape(S, D) if B == 1 else x[b]
        for l in range(depth):
            w_l = w_in[l]
            wt_qv = jnp.concatenate([w_l[:, q0:q0 + D_ATTN], w_l[:, v0:v0 + D_ATTN]], axis=1).T.astype(BF16)
            glu, k, km, q_t, v_t, gates = _inproj_fused(xb, row(norm1_g[l]), w_l.astype(BF16), wt_qv)
            cs = _conv_branch(glu, conv_dw[l], row(conv_dw_b[l]), row(conv_ln_g[l]), row(conv_ln_b[l]))
            attn = _attention(k, km, q_t, v_t, bias, rel_bias)
            xb = _mix(cs, attn, gates, xb, w_conv_out[l].astype(BF16), w_attn_out[l].astype(BF16),
                      w_out[l].astype(BF16))
            xb = _ffn(xb, row(norm2_g[l]), w_up[l].astype(BF16), ffn_dw[l], row(ffn_dw_b[l]),
                      w_down[l].astype(BF16), row(final_g), final_norm=(l == depth - 1))
        outs.append(xb)
    return outs[0].reshape(1, S, D) if B == 1 else jnp.stack(outs, axis=0)
```

```python
import functools
import math

import jax
import jax.numpy as jnp
from jax import lax
from jax.experimental import pallas as pl
from jax.experimental.pallas import tpu as pltpu

F32 = jnp.float32
BF16 = jnp.bfloat16

D_MODEL = 2048
D_CONV = 1024
CONV_K = 31
N_HEADS = 8
HEAD_DIM = 128
D_ATTN = N_HEADS * HEAD_DIM
MOBA_BLOCK = 256
MOBA_TOPK = 3
N_BUCKETS = 32
REL_MAX_DIST = 2048
D_FF = 5632
FFN_CONV_K = 3
EPS = 1e-6
NEG = -1e30
LOG2E = math.log2(math.e)

N_BIAS_TILES = 8
BIAS_ROWS = 16
KV_GROUP = 2
GROUP_KEYS = KV_GROUP * MOBA_BLOCK
INPROJ_ROWS = 1024
SUM_ROWS = 16
V_ROWS = HEAD_DIM + SUM_ROWS
ATTN_HEADS = 2

V7X_VMEM_BYTES = 64 * 1024 * 1024
VMEM_LIMIT = V7X_VMEM_BYTES - 8 * 1024 * 1024

SUBLANES = 8
NORM_ROWS = 16
NORM_UNROLL = 4
HALO = 8
CONV_HALO = 32

W_TILE = 512

TN_DIMS = (((0,), (1,)), ((), ()))


def _params(sem):
    return pltpu.CompilerParams(dimension_semantics=sem, vmem_limit_bytes=VMEM_LIMIT)


def _rmsnorm_rows(x_ref, g_ref, dst_ref, n_rows, dst_off=0, out_dtype=BF16):
    def body(c, carry):
        r = pl.multiple_of(c * NORM_ROWS, NORM_ROWS)
        xv = x_ref[pl.ds(r, NORM_ROWS), :]
        ms = jnp.mean(xv * xv, axis=-1, keepdims=True)
        y = xv * lax.rsqrt(ms + EPS) * g_ref[...]
        dst_ref[pl.ds(dst_off + r, NORM_ROWS), :] = y.astype(out_dtype)
        return carry
    lax.fori_loop(0, n_rows // NORM_ROWS, body, 0, unroll=NORM_UNROLL)


def _inproj_kernel(x_ref, g_ref, w_ref, w2_ref, glu_ref, k_ref, km_ref, q_ref, v_ref, gate_ref,
                   h_ref, *, seg):
    j = pl.program_id(1)
    n_glu, n_k, n_q, n_v, _ = seg
    k0 = n_glu
    q0 = k0 + n_k
    v0 = q0 + n_q
    g0 = v0 + n_v

    @pl.when(j == 0)
    def _():
        _rmsnorm_rows(x_ref, g_ref, h_ref, x_ref.shape[0])

    @pl.when(j < k0)
    def _():
        h = h_ref[...]
        a = jnp.dot(h, w_ref[0], preferred_element_type=F32)
        b = jnp.dot(h, w2_ref[0], preferred_element_type=F32)
        glu_ref[...] = a * jax.nn.sigmoid(b)

    @pl.when(jnp.logical_and(j >= k0, j < q0))
    def _():
        acc = jnp.dot(h_ref[...], w_ref[0], preferred_element_type=F32)
        k_ref[...] = acc.astype(k_ref.dtype)
        for b in range(acc.shape[0] // MOBA_BLOCK):
            blk = acc[b * MOBA_BLOCK:(b + 1) * MOBA_BLOCK]
            km_ref[0, b:b + 1, :] = jnp.mean(blk, axis=0, keepdims=True)

    @pl.when(jnp.logical_and(j >= q0, j < v0))
    def _():
        acc = lax.dot_general(w_ref[0], h_ref[...], TN_DIMS, preferred_element_type=F32)
        q_ref[0] = acc.astype(q_ref.dtype)

    @pl.when(jnp.logical_and(j >= v0, j < g0))
    def _():
        acc = lax.dot_general(w_ref[0], h_ref[...], TN_DIMS, preferred_element_type=F32)
        for grp in range(v_ref.shape[0]):
            pos = slice(grp * GROUP_KEYS, (grp + 1) * GROUP_KEYS)
            for hh in range(acc.shape[0] // HEAD_DIM):
                r = hh * V_ROWS
                v_ref[grp, r:r + HEAD_DIM, :] = acc[hh * HEAD_DIM:(hh + 1) * HEAD_DIM, pos].astype(v_ref.dtype)
                v_ref[grp, r + HEAD_DIM:r + V_ROWS, :] = jnp.ones((SUM_ROWS, GROUP_KEYS), v_ref.dtype)

    @pl.when(j >= g0)
    def _():
        acc = jnp.dot(h_ref[...], w_ref[0], preferred_element_type=F32)
        gate_ref[...] = jax.nn.sigmoid(acc).astype(gate_ref.dtype)


def _inproj_fused(x, g, w_in, layer, tn=W_TILE):
    S, D = x.shape
    tm = INPROJ_ROWS
    nm = S // tm
    groups = tm // GROUP_KEYS
    seg = (D_CONV // tn, D_ATTN // tn, D_ATTN // tn, D_ATTN // tn, 2 * D_MODEL // tn)
    n_glu, n_k, n_q, n_v, n_g = seg
    k0, q0 = n_glu, n_glu + n_k
    v0 = q0 + n_q
    g0 = v0 + n_v
    col_q = 2 * D_CONV // tn
    col_k = col_q + n_q

    def w_col(j):
        return jnp.where(j < k0, j, jnp.where(j < q0, col_k + (j - k0),
                         jnp.where(j < v0, col_q + (j - q0), col_k + n_k + (j - v0))))

    seg_tile = lambda j, start, n: jnp.clip(j - start, 0, n - 1)
    v_tile_rows = tn // HEAD_DIM * V_ROWS
    glu, k, km, q_t, v_t, gates = pl.pallas_call(
        functools.partial(_inproj_kernel, seg=seg),
        grid=(nm, sum(seg)),
        in_specs=[pl.BlockSpec((tm, D), lambda i, j: (i, 0)),
                  pl.BlockSpec((1, D), lambda i, j: (0, 0)),
                  pl.BlockSpec((1, D, tn), lambda i, j: (layer, 0, w_col(j))),
                  pl.BlockSpec((1, D, tn), lambda i, j: (layer, 0, D_CONV // tn + seg_tile(j, 0, n_glu)))],
        out_specs=[pl.BlockSpec((tm, tn), lambda i, j: (i, seg_tile(j, 0, n_glu))),
                   pl.BlockSpec((tm, tn), lambda i, j: (i, seg_tile(j, k0, n_k))),
                   pl.BlockSpec((1, tm // MOBA_BLOCK, tn), lambda i, j: (i, 0, seg_tile(j, k0, n_k))),
                   pl.BlockSpec((1, tn, tm), lambda i, j: (i, seg_tile(j, q0, n_q), 0)),
                   pl.BlockSpec((groups, v_tile_rows, GROUP_KEYS), lambda i, j: (i, seg_tile(j, v0, n_v), 0)),
                   pl.BlockSpec((tm, tn), lambda i, j: (i, seg_tile(j, g0, n_g)))],
        out_shape=[jax.ShapeDtypeStruct((S, D_CONV), F32),
                   jax.ShapeDtypeStruct((S, D_ATTN), BF16),
                   jax.ShapeDtypeStruct((nm, tm // MOBA_BLOCK, D_ATTN), F32),
                   jax.ShapeDtypeStruct((nm, D_ATTN, tm), BF16),
                   jax.ShapeDtypeStruct((S // GROUP_KEYS, N_HEADS * V_ROWS, GROUP_KEYS), BF16),
                   jax.ShapeDtypeStruct((S, 2 * D_MODEL), BF16)],
        scratch_shapes=[pltpu.VMEM((tm, D), BF16)],
        compiler_params=_params(("parallel", "arbitrary")),
        name="inproj",
    )(x, g, w_in, w_in)
    return glu, k, km.reshape(S // MOBA_BLOCK, D_ATTN), q_t, v_t, gates


def _conv_kernel(halo_ref, x_ref, w_ref, b_ref, lg_ref, lb_ref, o_ref, win_ref, y_ref,
                 *, rows, lanes):
    tm, C = x_ref.shape
    i = pl.program_id(0)
    halo = halo_ref[...]
    win_ref[0:CONV_HALO, :] = jnp.where(i > 0, halo, jnp.zeros_like(halo))
    win_ref[CONV_HALO:, :] = x_ref[...]
    first = CONV_HALO - (CONV_K - 1)

    for c0 in range(0, C, lanes):
        def body(rc, carry):
            r = pl.multiple_of(rc * rows, rows)
            acc = jnp.zeros((rows, lanes), F32) + b_ref[:, c0:c0 + lanes]
            win = win_ref[pl.ds(r, rows + CONV_HALO), c0:c0 + lanes]
            for sub in range(SUBLANES):
                shifted = win if sub == 0 else pltpu.roll(win, win.shape[0] - sub, 0)
                usable = win.shape[0] - (SUBLANES if sub else 0)
                for base in range(0, usable - rows + 1, SUBLANES):
                    k = base + sub - first
                    if 0 <= k < CONV_K:
                        acc = acc + shifted[base:base + rows] * w_ref[k:k + 1, c0:c0 + lanes]
            y_ref[pl.ds(r, rows), c0:c0 + lanes] = acc
            return carry
        lax.fori_loop(0, tm // rows, body, 0)

    def ln_body(rc, carry):
        r = pl.multiple_of(rc * NORM_ROWS, NORM_ROWS)
        v = y_ref[pl.ds(r, NORM_ROWS), :]
        mu = jnp.mean(v, axis=-1, keepdims=True)
        var = jnp.mean(jnp.square(v - mu), axis=-1, keepdims=True)
        z = (v - mu) * lax.rsqrt(var + EPS) * lg_ref[...] + lb_ref[...]
        o_ref[pl.ds(r, NORM_ROWS), :] = jax.nn.silu(z).astype(o_ref.dtype)
        return carry
    lax.fori_loop(0, tm // NORM_ROWS, ln_body, 0, unroll=NORM_UNROLL)


def _conv_branch(glu, w, b, lg, lb, tm=256):
    S, C = glu.shape
    per = tm // CONV_HALO
    return pl.pallas_call(
        functools.partial(_conv_kernel, rows=32, lanes=256),
        grid=(S // tm,),
        in_specs=[pl.BlockSpec((CONV_HALO, C), lambda i: (jnp.maximum(i * per - 1, 0), 0)),
                  pl.BlockSpec((tm, C), lambda i: (i, 0)),
                  pl.BlockSpec((CONV_K, C), lambda i: (0, 0)),
                  pl.BlockSpec((1, C), lambda i: (0, 0)),
                  pl.BlockSpec((1, C), lambda i: (0, 0)),
                  pl.BlockSpec((1, C), lambda i: (0, 0))],
        out_specs=pl.BlockSpec((tm, C), lambda i: (i, 0)),
        out_shape=jax.ShapeDtypeStruct((S, C), BF16),
        scratch_shapes=[pltpu.VMEM((tm + CONV_HALO, C), F32), pltpu.VMEM((tm, C), F32)],
        compiler_params=_params(("parallel",)),
        name="conv_branch",
    )(glu, glu, w, b, lg, lb)


def _bias_kernel(rb_ref, o_ref):
    delta = pl.program_id(0)
    shape = (BIAS_ROWS, MOBA_BLOCK)

    def body(c, carry):
        c0 = pl.multiple_of(c * BIAS_ROWS, BIAS_ROWS)
        key = c0 + lax.broadcasted_iota(jnp.int32, shape, 0)
        d = delta * MOBA_BLOCK + lax.broadcasted_iota(jnp.int32, shape, 1) - key
        n = jnp.maximum(d, 0)
        max_exact = N_BUCKETS // 2
        nf = jnp.maximum(n, 1).astype(F32)
        large = max_exact + (jnp.log(nf / max_exact) / math.log(REL_MAX_DIST / max_exact)
                             * (N_BUCKETS - max_exact)).astype(jnp.int32)
        large = jnp.minimum(large, N_BUCKETS - 1)
        bucket = jnp.where(n < max_exact, n, large)
        vals = [jnp.zeros(shape, F32) for _ in range(N_HEADS)]
        for b in range(N_BUCKETS):
            hit = bucket == b
            vals = [jnp.where(hit, rb_ref[b, h], vals[h]) for h in range(N_HEADS)]
        for h in range(N_HEADS):
            val = jnp.where(delta == N_BIAS_TILES - 1, 0.0, vals[h] * LOG2E)
            o_ref[h, 0, pl.ds(c0, BIAS_ROWS), :] = jnp.where(d >= 0, val, NEG)
        return carry
    lax.fori_loop(0, MOBA_BLOCK // BIAS_ROWS, body, 0)


def _bias_tiles(rel_bias):
    return pl.pallas_call(
        _bias_kernel,
        grid=(N_BIAS_TILES,),
        in_specs=[pl.BlockSpec(memory_space=pltpu.SMEM)],
        out_specs=pl.BlockSpec((N_HEADS, 1, MOBA_BLOCK, MOBA_BLOCK), lambda d: (0, d, 0, 0)),
        out_shape=jax.ShapeDtypeStruct((N_HEADS, N_BIAS_TILES, MOBA_BLOCK, MOBA_BLOCK), F32),
        compiler_params=_params(("parallel",)),
        name="t5_bias_tiles",
    )(rel_bias)


def _attn_kernel(rb_ref, qt_ref, k_ref, vt_ref, km_ref, ind_ref, bias_ref, o_ref,
                 acc_ref, sa_ref, sb_ref, pa_ref, pb_ref):
    i = pl.program_id(1)
    nb = km_ref.shape[0]
    tq = qt_ref.shape[2]
    scale = HEAD_DIM ** -0.5
    heads = range(ATTN_HEADS)
    lanes = [slice(hh * HEAD_DIM, (hh + 1) * HEAD_DIM) for hh in heads]

    def widened_query(hh):
        qf = qt_ref[0, lanes[hh], :].astype(F32)
        gate = jnp.dot(km_ref[:, lanes[hh]], qf, precision=lax.Precision.HIGHEST, preferred_element_type=F32)
        row = lax.broadcasted_iota(jnp.int32, (nb, tq), 0)
        valid = row < i
        g = jnp.where(valid, gate, NEG)
        sel = jnp.zeros((nb, tq), jnp.bool_)
        for _ in range(MOBA_TOPK):
            top = jnp.max(g, axis=0, keepdims=True)
            idx = jnp.min(jnp.where(g == top, row, nb), axis=0, keepdims=True)
            hit = row == idx
            sel = jnp.logical_or(sel, hit)
            g = jnp.where(hit, -jnp.inf, g)
        picked = jnp.logical_and(sel, valid)
        far = jnp.logical_and(picked, row <= i - (N_BIAS_TILES - 1))
        far_bias = rb_ref[N_BUCKETS - 1, pl.program_id(0) * ATTN_HEADS + hh] * LOG2E
        pen = jnp.where(row == i, 0.0, jnp.where(picked, jnp.where(far, far_bias, 0.0), NEG))
        pen_hi = pen.astype(BF16)
        pen_lo = jnp.where(far, pen - pen_hi.astype(F32), 0.0).astype(BF16)
        return jnp.concatenate(
            [(qf * (scale * LOG2E)).astype(BF16), pen_hi, pen_lo, jnp.zeros((HEAD_DIM - 2 * nb, tq), BF16)],
            axis=0)

    q_aug = [widened_query(hh) for hh in heads]
    n_groups = i // KV_GROUP + 1

    def scores(grp, dst_ref):
        r0 = pl.multiple_of(grp * GROUP_KEYS, GROUP_KEYS)
        ind = ind_ref[pl.ds(r0, GROUP_KEYS), :]
        for hh in heads:
            k_aug = jnp.concatenate([k_ref[pl.ds(r0, GROUP_KEYS), lanes[hh]], ind], axis=1)
            dst_ref[hh] = jnp.dot(k_aug, q_aug[hh], preferred_element_type=F32)

    def softmax(grp, src_ref, dst_ref, m_prev):
        m_out, alpha_out = [], []
        for hh in heads:
            parts = []
            for u in range(KV_GROUP):
                delta = jnp.clip(i - (grp * KV_GROUP + u), 0, N_BIAS_TILES - 1)
                parts.append(src_ref[hh, u * MOBA_BLOCK:(u + 1) * MOBA_BLOCK, :] + bias_ref[hh, delta])
            s = jnp.concatenate(parts, axis=0)
            m_new = jnp.maximum(m_prev[hh], jnp.max(s, axis=0, keepdims=True))
            dst_ref[hh] = jnp.exp2(s - m_new).astype(BF16)
            m_out.append(m_new)
            alpha_out.append(jnp.exp2(m_prev[hh] - m_new))
        return tuple(m_out), tuple(alpha_out)

    def weighted_values(grp, live, src_ref, alpha):
        for hh in heads:
            pv = jnp.dot(vt_ref[grp, hh * V_ROWS:(hh + 1) * V_ROWS, :], src_ref[hh], preferred_element_type=F32)
            acc_ref[hh] = alpha[hh] * acc_ref[hh] + jnp.where(live, pv, 0.0)

    acc_ref[...] = jnp.zeros_like(acc_ref)
    pb_ref[...] = jnp.zeros_like(pb_ref)
    scores(n_groups - 1, sa_ref)

    def body(u, carry):
        m_prev, alpha_b, grp_b_prev = carry
        grp_a = n_groups - 1 - 2 * u
        grp_b = grp_a - 1
        weighted_values(jnp.maximum(grp_b_prev, 0), grp_b_prev >= 0, pb_ref, alpha_b)
        m_a, alpha_a = softmax(grp_a, sa_ref, pa_ref, m_prev)
        scores(jnp.maximum(grp_b, 0), sb_ref)
        weighted_values(grp_a, True, pa_ref, alpha_a)
        m_b, alpha_b = softmax(jnp.maximum(grp_b, 0), sb_ref, pb_ref, m_a)
        scores(jnp.maximum(grp_b - 1, 0), sa_ref)
        return m_b, alpha_b, grp_b

    m0 = tuple(jnp.full((1, tq), NEG, F32) for _ in heads)
    a0 = tuple(jnp.ones((1, tq), F32) for _ in heads)
    _, alpha_b, grp_b = lax.fori_loop(0, (n_groups + 1) // 2, body, (m0, a0, jnp.int32(-1)))
    weighted_values(jnp.maximum(grp_b, 0), grp_b >= 0, pb_ref, alpha_b)
    for hh in heads:
        acc = acc_ref[hh]
        o_ref[:, lanes[hh]] = (acc[:HEAD_DIM] / acc[HEAD_DIM:HEAD_DIM + 1]).T.astype(o_ref.dtype)


def _attention(k, km, q_t, v_t, bias, rel_bias):
    S = k.shape[0]
    nb = S // MOBA_BLOCK
    tq = MOBA_BLOCK
    per = INPROJ_ROWS // tq
    width = ATTN_HEADS * HEAD_DIM
    key_blk = jnp.arange(S, dtype=jnp.int32)[:, None] // MOBA_BLOCK
    lane = jnp.arange(HEAD_DIM, dtype=jnp.int32)[None, :]
    ind = jnp.logical_and(lane < 2 * nb, lane % nb == key_blk).astype(BF16)
    return pl.pallas_call(
        _attn_kernel,
        grid=(N_HEADS // ATTN_HEADS, S // tq),
        in_specs=[pl.BlockSpec(memory_space=pltpu.SMEM),
                  pl.BlockSpec((1, width, tq), lambda h, i: (i // per, h, i % per)),
                  pl.BlockSpec((S, width), lambda h, i: (0, h)),
                  pl.BlockSpec((S // GROUP_KEYS, ATTN_HEADS * V_ROWS, GROUP_KEYS), lambda h, i: (0, h, 0)),
                  pl.BlockSpec((nb, width), lambda h, i: (0, h)),
                  pl.BlockSpec((S, HEAD_DIM), lambda h, i: (0, 0)),
                  pl.BlockSpec((ATTN_HEADS, N_BIAS_TILES, MOBA_BLOCK, MOBA_BLOCK), lambda h, i: (h, 0, 0, 0))],
        out_specs=pl.BlockSpec((tq, width), lambda h, i: (i, h)),
        out_shape=jax.ShapeDtypeStruct((S, D_ATTN), BF16),
        scratch_shapes=[pltpu.VMEM((ATTN_HEADS, HEAD_DIM + SUM_ROWS, tq), F32),
                        pltpu.VMEM((ATTN_HEADS, GROUP_KEYS, tq), F32),
                        pltpu.VMEM((ATTN_HEADS, GROUP_KEYS, tq), F32),
                        pltpu.VMEM((ATTN_HEADS, GROUP_KEYS, tq), BF16),
                        pltpu.VMEM((ATTN_HEADS, GROUP_KEYS, tq), BF16)],
        compiler_params=_params(("parallel", "parallel")),
        name="moba_attention",
    )(rel_bias, q_t, k, v_t, km, ind, bias)


def _mix_kernel(cs_ref, at_ref, gate_ref, x_ref, wc_ref, wa_ref, wo_ref, o_ref):
    D = x_ref.shape[1]
    yc = jnp.dot(cs_ref[...], wc_ref[...], preferred_element_type=F32)
    ya = jnp.dot(at_ref[...], wa_ref[...], preferred_element_type=F32)
    merged = gate_ref[:, :D].astype(F32) * yc + gate_ref[:, D:].astype(F32) * ya
    o_ref[...] = x_ref[...] + jnp.dot(merged.astype(BF16), wo_ref[...], preferred_element_type=F32)


def _mix(cs, attn, gates, x, wc, wa, wo, tm=256):
    S, D = x.shape
    full = lambda shape: pl.BlockSpec(shape, lambda i: (0, 0))
    return pl.pallas_call(
        _mix_kernel,
        grid=(S // tm,),
        in_specs=[pl.BlockSpec((tm, D_CONV), lambda i: (i, 0)),
                  pl.BlockSpec((tm, D_ATTN), lambda i: (i, 0)),
                  pl.BlockSpec((tm, 2 * D), lambda i: (i, 0)),
                  pl.BlockSpec((tm, D), lambda i: (i, 0)),
                  full((D_CONV, D)), full((D_ATTN, D)), full((D, D))],
        out_specs=pl.BlockSpec((tm, D), lambda i: (i, 0)),
        out_shape=jax.ShapeDtypeStruct((S, D), F32),
        compiler_params=_params(("parallel",)),
        name="mix_out_proj",
    )(cs, attn, gates, x, wc, wa, wo)


def _ffn_kernel(xh_ref, x_ref, g_ref, wa_ref, wb_ref, dwa_ref, dwb_ref, ba_ref, bb_ref, wd_ref,
                fg_ref, o_ref, h_ref, gate_ref, *, final_norm, n_tiles, up_chunk):
    i = pl.program_id(0)
    f = pl.program_id(1)
    tm = x_ref.shape[0]
    tf = wa_ref.shape[1]
    slot = f % 2

    def up_stage():
        h = h_ref[...]
        for c0 in range(0, tf, up_chunk):
            cols = slice(c0, c0 + up_chunk)

            def up_conv(w_ref, dw_ref, b_ref):
                u = jnp.dot(h, w_ref[:, cols], preferred_element_type=F32)
                y = (u * dw_ref[2:3, cols] + pltpu.roll(u, 1, 0) * dw_ref[1:2, cols]
                     + pltpu.roll(u, 2, 0) * dw_ref[0:1, cols] + b_ref[:, cols])
                return y[HALO:]

            ua = up_conv(wa_ref, dwa_ref, ba_ref)
            ub = up_conv(wb_ref, dwb_ref, bb_ref)
            gate_ref[slot, :, cols] = (jax.nn.silu(ub) * ua).astype(BF16)

    def down_product():
        return jnp.dot(gate_ref[1 - slot], wd_ref[...], preferred_element_type=F32)

    @pl.when(f == 0)
    def _():
        xh = xh_ref[...]
        ms = jnp.mean(xh * xh, axis=-1, keepdims=True)
        hh = xh * lax.rsqrt(ms + EPS) * g_ref[...]
        h_ref[0:HALO, :] = jnp.where(i > 0, hh, jnp.zeros_like(hh)).astype(BF16)
        _rmsnorm_rows(x_ref, g_ref, h_ref, tm, dst_off=HALO)
        o_ref[...] = jnp.zeros_like(o_ref)
        up_stage()

    @pl.when(jnp.logical_and(f > 0, f < n_tiles))
    def _():
        o_ref[...] += down_product()
        up_stage()

    @pl.when(f == n_tiles)
    def _():
        y = x_ref[...] + o_ref[...] + down_product()
        if final_norm:
            ms = jnp.mean(y * y, axis=-1, keepdims=True)
            y = y * lax.rsqrt(ms + EPS) * fg_ref[...]
        o_ref[...] = y


def _ffn(x, g, w_up, dw, dwb, w_down, final_g, final_norm, tm=1024, tf=W_TILE):
    S, D = x.shape
    nf = D_FF // tf
    per = tm // HALO
    row = lambda shape: pl.BlockSpec(shape, lambda i, f: (0, 0))
    up = lambda f: jnp.minimum(f, nf - 1)
    down = lambda f: jnp.maximum(f - 1, 0)
    return pl.pallas_call(
        functools.partial(_ffn_kernel, final_norm=final_norm, n_tiles=nf, up_chunk=tf // 2),
        grid=(S // tm, nf + 1),
        in_specs=[pl.BlockSpec((HALO, D), lambda i, f: (jnp.maximum(i * per - 1, 0), 0)),
                  pl.BlockSpec((tm, D), lambda i, f: (i, 0), pipeline_mode=pl.Buffered(1)),
                  row((1, D)),
                  pl.BlockSpec((D, tf), lambda i, f: (0, up(f))),
                  pl.BlockSpec((D, tf), lambda i, f: (0, up(f) + nf)),
                  pl.BlockSpec((FFN_CONV_K, tf), lambda i, f: (0, up(f))),
                  pl.BlockSpec((FFN_CONV_K, tf), lambda i, f: (0, up(f) + nf)),
                  pl.BlockSpec((1, tf), lambda i, f: (0, up(f))),
                  pl.BlockSpec((1, tf), lambda i, f: (0, up(f) + nf)),
                  pl.BlockSpec((tf, D), lambda i, f: (down(f), 0)),
                  row((1, D))],
        out_specs=pl.BlockSpec((tm, D), lambda i, f: (i, 0)),
        out_shape=jax.ShapeDtypeStruct((S, D), F32),
        scratch_shapes=[pltpu.VMEM((HALO + tm, D), BF16), pltpu.VMEM((2, tm, tf), BF16)],
        compiler_params=_params(("parallel", "arbitrary")),
        name="conv_ffn",
    )(x, x, g, w_up, w_up, dw, dw, dwb, dwb, w_down, final_g)


def kernel(x, norm1_g, w_in, conv_dw, conv_dw_b, conv_ln_g, conv_ln_b, w_conv_out, rel_bias,
           w_attn_out, w_out, norm2_g, w_up, ffn_dw, ffn_dw_b, w_down, final_g):
    B, S, D = x.shape
    assert D == D_MODEL and S % INPROJ_ROWS == 0 and INPROJ_ROWS % GROUP_KEYS == 0
    depth = w_in.shape[0]
    bias = _bias_tiles(rel_bias)
    row = lambda v: v.reshape(1, -1)
    w_in_bf = w_in.astype(BF16)
    w_conv_bf, w_attn_bf, w_out_bf = (w.astype(BF16) for w in (w_conv_out, w_attn_out, w_out))
    w_up_bf, w_down_bf = w_up.astype(BF16), w_down.astype(BF16)
    outs = []
    for b in range(B):
        xb = x.reshape(S, D) if B == 1 else x[b]
        for l in range(depth):
            glu, k, km, q_t, v_t, gates = _inproj_fused(xb, row(norm1_g[l]), w_in_bf, l)
            cs = _conv_branch(glu, conv_dw[l], row(conv_dw_b[l]), row(conv_ln_g[l]), row(conv_ln_b[l]))
            attn = _attention(k, km, q_t, v_t, bias, rel_bias)
            xb = _mix(cs, attn, gates, xb, w_conv_bf[l], w_attn_bf[l], w_out_bf[l])
            xb = _ffn(xb, row(norm2_g[l]), w_up_bf[l], ffn_dw[l], row(ffn_dw_b[l]),
                      w_down_bf[l], row(final_g), final_norm=(l == depth - 1))
        outs.append(xb)
    return outs[0].reshape(1, S, D) if B == 1 else jnp.stack(outs, axis=0)
```

```python
import functools
import math

import jax
import jax.numpy as jnp
from jax import lax
from jax.experimental import pallas as pl
from jax.experimental.pallas import tpu as pltpu

F32 = jnp.float32
BF16 = jnp.bfloat16

D_MODEL = 2048
D_CONV = 1024
CONV_K = 31
N_HEADS = 8
HEAD_DIM = 128
D_ATTN = N_HEADS * HEAD_DIM
MOBA_BLOCK = 256
MOBA_TOPK = 3
N_BUCKETS = 32
REL_MAX_DIST = 2048
D_FF = 5632
FFN_CONV_K = 3
EPS = 1e-6
NEG = -1e30
LOG2E = math.log2(math.e)

N_BIAS_TILES = 8
BIAS_ROWS = 16
KV_GROUP = 2
GROUP_KEYS = KV_GROUP * MOBA_BLOCK
INPROJ_ROWS = 1024
SUM_ROWS = 16
V_ROWS = HEAD_DIM + SUM_ROWS
ATTN_HEADS = 2

V7X_VMEM_BYTES = 64 * 1024 * 1024
VMEM_LIMIT = V7X_VMEM_BYTES - 8 * 1024 * 1024

SUBLANES = 8
NORM_ROWS = 16
NORM_UNROLL = 4
HALO = 8
CONV_HALO = 32

W_TILE = 512

TN_DIMS = (((0,), (1,)), ((), ()))


def _params(sem):
    return pltpu.CompilerParams(dimension_semantics=sem, vmem_limit_bytes=VMEM_LIMIT)


def _rmsnorm_rows(x_ref, g_ref, dst_ref, n_rows, dst_off=0, out_dtype=BF16):
    def body(c, carry):
        r = pl.multiple_of(c * NORM_ROWS, NORM_ROWS)
        xv = x_ref[pl.ds(r, NORM_ROWS), :]
        ms = jnp.mean(xv * xv, axis=-1, keepdims=True)
        y = xv * lax.rsqrt(ms + EPS) * g_ref[...]
        dst_ref[pl.ds(dst_off + r, NORM_ROWS), :] = y.astype(out_dtype)
        return carry
    lax.fori_loop(0, n_rows // NORM_ROWS, body, 0, unroll=NORM_UNROLL)


def _inproj_kernel(x_ref, g_ref, w_ref, w2_ref, glu_ref, k_ref, km_ref, q_ref, v_ref, gate_ref,
                   h_ref, *, seg):
    j = pl.program_id(1)
    n_glu, n_k, n_q, n_v, _ = seg
    k0 = n_glu
    q0 = k0 + n_k
    v0 = q0 + n_q
    g0 = v0 + n_v

    @pl.when(j == 0)
    def _():
        _rmsnorm_rows(x_ref, g_ref, h_ref, x_ref.shape[0])

    @pl.when(j < k0)
    def _():
        h = h_ref[...]
        a = jnp.dot(h, w_ref[0], preferred_element_type=F32)
        b = jnp.dot(h, w2_ref[0], preferred_element_type=F32)
        glu_ref[...] = a * jax.nn.sigmoid(b)

    @pl.when(jnp.logical_and(j >= k0, j < q0))
    def _():
        acc = jnp.dot(h_ref[...], w_ref[0], preferred_element_type=F32)
        k_ref[...] = acc.astype(k_ref.dtype)
        for b in range(acc.shape[0] // MOBA_BLOCK):
            blk = acc[b * MOBA_BLOCK:(b + 1) * MOBA_BLOCK]
            km_ref[0, b:b + 1, :] = jnp.mean(blk, axis=0, keepdims=True)

    @pl.when(jnp.logical_and(j >= q0, j < v0))
    def _():
        acc = lax.dot_general(w_ref[0], h_ref[...], TN_DIMS, preferred_element_type=F32)
        q_ref[0] = acc.astype(q_ref.dtype)

    @pl.when(jnp.logical_and(j >= v0, j < g0))
    def _():
        acc = lax.dot_general(w_ref[0], h_ref[...], TN_DIMS, preferred_element_type=F32)
        for grp in range(v_ref.shape[0]):
            pos = slice(grp * GROUP_KEYS, (grp + 1) * GROUP_KEYS)
            for hh in range(acc.shape[0] // HEAD_DIM):
                r = hh * V_ROWS
                v_ref[grp, r:r + HEAD_DIM, :] = acc[hh * HEAD_DIM:(hh + 1) * HEAD_DIM, pos].astype(v_ref.dtype)
                v_ref[grp, r + HEAD_DIM:r + V_ROWS, :] = jnp.ones((SUM_ROWS, GROUP_KEYS), v_ref.dtype)

    @pl.when(j >= g0)
    def _():
        acc = jnp.dot(h_ref[...], w_ref[0], preferred_element_type=F32)
        gate_ref[...] = jax.nn.sigmoid(acc).astype(gate_ref.dtype)


def _inproj_fused(x, g, w_in, layer, tn=W_TILE):
    S, D = x.shape
    tm = INPROJ_ROWS
    nm = S // tm
    groups = tm // GROUP_KEYS
    seg = (D_CONV // tn, D_ATTN // tn, D_ATTN // tn, D_ATTN // tn, 2 * D_MODEL // tn)
    n_glu, n_k, n_q, n_v, n_g = seg
    k0, q0 = n_glu, n_glu + n_k
    v0 = q0 + n_q
    g0 = v0 + n_v
    col_q = 2 * D_CONV // tn
    col_k = col_q + n_q

    def w_col(j):
        return jnp.where(j < k0, j, jnp.where(j < q0, col_k + (j - k0),
                         jnp.where(j < v0, col_q + (j - q0), col_k + n_k + (j - v0))))

    seg_tile = lambda j, start, n: jnp.clip(j - start, 0, n - 1)
    v_tile_rows = tn // HEAD_DIM * V_ROWS
    glu, k, km, q_t, v_t, gates = pl.pallas_call(
        functools.partial(_inproj_kernel, seg=seg),
        grid=(nm, sum(seg)),
        in_specs=[pl.BlockSpec((tm, D), lambda i, j: (i, 0)),
                  pl.BlockSpec((1, D), lambda i, j: (0, 0)),
                  pl.BlockSpec((1, D, tn), lambda i, j: (layer, 0, w_col(j))),
                  pl.BlockSpec((1, D, tn), lambda i, j: (layer, 0, D_CONV // tn + seg_tile(j, 0, n_glu)))],
        out_specs=[pl.BlockSpec((tm, tn), lambda i, j: (i, seg_tile(j, 0, n_glu))),
                   pl.BlockSpec((tm, tn), lambda i, j: (i, seg_tile(j, k0, n_k))),
                   pl.BlockSpec((1, tm // MOBA_BLOCK, tn), lambda i, j: (i, 0, seg_tile(j, k0, n_k))),
                   pl.BlockSpec((1, tn, tm), lambda i, j: (i, seg_tile(j, q0, n_q), 0)),
                   pl.BlockSpec((groups, v_tile_rows, GROUP_KEYS), lambda i, j: (i, seg_tile(j, v0, n_v), 0)),
                   pl.BlockSpec((tm, tn), lambda i, j: (i, seg_tile(j, g0, n_g)))],
        out_shape=[jax.ShapeDtypeStruct((S, D_CONV), F32),
                   jax.ShapeDtypeStruct((S, D_ATTN), BF16),
                   jax.ShapeDtypeStruct((nm, tm // MOBA_BLOCK, D_ATTN), F32),
                   jax.ShapeDtypeStruct((nm, D_ATTN, tm), BF16),
                   jax.ShapeDtypeStruct((S // GROUP_KEYS, N_HEADS * V_ROWS, GROUP_KEYS), BF16),
                   jax.ShapeDtypeStruct((S, 2 * D_MODEL), BF16)],
        scratch_shapes=[pltpu.VMEM((tm, D), BF16)],
        compiler_params=_params(("parallel", "arbitrary")),
        name="inproj",
    )(x, g, w_in, w_in)
    return glu, k, km.reshape(S // MOBA_BLOCK, D_ATTN), q_t, v_t, gates


def _conv_kernel(halo_ref, x_ref, w_ref, b_ref, lg_ref, lb_ref, o_ref, win_ref, y_ref,
                 *, rows, lanes):
    tm, C = x_ref.shape
    i = pl.program_id(0)
    halo = halo_ref[...]
    win_ref[0:CONV_HALO, :] = jnp.where(i > 0, halo, jnp.zeros_like(halo))
    win_ref[CONV_HALO:, :] = x_ref[...]
    first = CONV_HALO - (CONV_K - 1)

    for c0 in range(0, C, lanes):
        def body(rc, carry):
            r = pl.multiple_of(rc * rows, rows)
            acc = jnp.zeros((rows, lanes), F32) + b_ref[:, c0:c0 + lanes]
            win = win_ref[pl.ds(r, rows + CONV_HALO), c0:c0 + lanes]
            for sub in range(SUBLANES):
                shifted = win if sub == 0 else pltpu.roll(win, win.shape[0] - sub, 0)
                usable = win.shape[0] - (SUBLANES if sub else 0)
                for base in range(0, usable - rows + 1, SUBLANES):
                    k = base + sub - first
                    if 0 <= k < CONV_K:
                        acc = acc + shifted[base:base + rows] * w_ref[k:k + 1, c0:c0 + lanes]
            y_ref[pl.ds(r, rows), c0:c0 + lanes] = acc
            return carry
        lax.fori_loop(0, tm // rows, body, 0)

    def ln_body(rc, carry):
        r = pl.multiple_of(rc * NORM_ROWS, NORM_ROWS)
        v = y_ref[pl.ds(r, NORM_ROWS), :]
        mu = jnp.mean(v, axis=-1, keepdims=True)
        var = jnp.mean(jnp.square(v - mu), axis=-1, keepdims=True)
        z = (v - mu) * lax.rsqrt(var + EPS) * lg_ref[...] + lb_ref[...]
        o_ref[pl.ds(r, NORM_ROWS), :] = jax.nn.silu(z).astype(o_ref.dtype)
        return carry
    lax.fori_loop(0, tm // NORM_ROWS, ln_body, 0, unroll=NORM_UNROLL)


def _conv_branch(glu, w, b, lg, lb, tm=256):
    S, C = glu.shape
    per = tm // CONV_HALO
    return pl.pallas_call(
        functools.partial(_conv_kernel, rows=32, lanes=256),
        grid=(S // tm,),
        in_specs=[pl.BlockSpec((CONV_HALO, C), lambda i: (jnp.maximum(i * per - 1, 0), 0)),
                  pl.BlockSpec((tm, C), lambda i: (i, 0)),
                  pl.BlockSpec((CONV_K, C), lambda i: (0, 0)),
                  pl.BlockSpec((1, C), lambda i: (0, 0)),
                  pl.BlockSpec((1, C), lambda i: (0, 0)),
                  pl.BlockSpec((1, C), lambda i: (0, 0))],
        out_specs=pl.BlockSpec((tm, C), lambda i: (i, 0)),
        out_shape=jax.ShapeDtypeStruct((S, C), BF16),
        scratch_shapes=[pltpu.VMEM((tm + CONV_HALO, C), F32), pltpu.VMEM((tm, C), F32)],
        compiler_params=_params(("parallel",)),
        name="conv_branch",
    )(glu, glu, w, b, lg, lb)


def _bias_kernel(rb_ref, o_ref):
    delta = pl.program_id(0)
    shape = (BIAS_ROWS, MOBA_BLOCK)

    def body(c, carry):
        c0 = pl.multiple_of(c * BIAS_ROWS, BIAS_ROWS)
        key = c0 + lax.broadcasted_iota(jnp.int32, shape, 0)
        d = delta * MOBA_BLOCK + lax.broadcasted_iota(jnp.int32, shape, 1) - key
        n = jnp.maximum(d, 0)
        max_exact = N_BUCKETS // 2
        nf = jnp.maximum(n, 1).astype(F32)
        large = max_exact + (jnp.log(nf / max_exact) / math.log(REL_MAX_DIST / max_exact)
                             * (N_BUCKETS - max_exact)).astype(jnp.int32)
        large = jnp.minimum(large, N_BUCKETS - 1)
        bucket = jnp.where(n < max_exact, n, large)
        vals = [jnp.zeros(shape, F32) for _ in range(N_HEADS)]
        for b in range(N_BUCKETS):
            hit = bucket == b
            vals = [jnp.where(hit, rb_ref[b, h], vals[h]) for h in range(N_HEADS)]
        for h in range(N_HEADS):
            val = jnp.where(delta == N_BIAS_TILES - 1, 0.0, vals[h] * LOG2E)
            o_ref[h, 0, pl.ds(c0, BIAS_ROWS), :] = jnp.where(d >= 0, val, NEG)
        return carry
    lax.fori_loop(0, MOBA_BLOCK // BIAS_ROWS, body, 0)


def _bias_tiles(rel_bias):
    return pl.pallas_call(
        _bias_kernel,
        grid=(N_BIAS_TILES,),
        in_specs=[pl.BlockSpec(memory_space=pltpu.SMEM)],
        out_specs=pl.BlockSpec((N_HEADS, 1, MOBA_BLOCK, MOBA_BLOCK), lambda d: (0, d, 0, 0)),
        out_shape=jax.ShapeDtypeStruct((N_HEADS, N_BIAS_TILES, MOBA_BLOCK, MOBA_BLOCK), F32),
        compiler_params=_params(("parallel",)),
        name="t5_bias_tiles",
    )(rel_bias)


def _attn_kernel(rb_ref, qt_ref, k_ref, vt_ref, km_ref, ind_ref, bias_ref, o_ref,
                 acc_ref, sa_ref, sb_ref, pa_ref, pb_ref):
    i = pl.program_id(1)
    nb = km_ref.shape[0]
    tq = qt_ref.shape[2]
    scale = HEAD_DIM ** -0.5
    heads = range(ATTN_HEADS)
    lanes = [slice(hh * HEAD_DIM, (hh + 1) * HEAD_DIM) for hh in heads]

    def widened_query(hh):
        qf = qt_ref[0, lanes[hh], :].astype(F32)
        gate = jnp.dot(km_ref[:, lanes[hh]], qf, precision=lax.Precision.HIGHEST, preferred_element_type=F32)
        row = lax.broadcasted_iota(jnp.int32, (nb, tq), 0)
        valid = row < i
        g = jnp.where(valid, gate, NEG)
        sel = jnp.zeros((nb, tq), jnp.bool_)
        for _ in range(MOBA_TOPK):
            top = jnp.max(g, axis=0, keepdims=True)
            idx = jnp.min(jnp.where(g == top, row, nb), axis=0, keepdims=True)
            hit = row == idx
            sel = jnp.logical_or(sel, hit)
            g = jnp.where(hit, -jnp.inf, g)
        picked = jnp.logical_and(sel, valid)
        far = jnp.logical_and(picked, row <= i - (N_BIAS_TILES - 1))
        far_bias = rb_ref[N_BUCKETS - 1, pl.program_id(0) * ATTN_HEADS + hh] * LOG2E
        pen = jnp.where(row == i, 0.0, jnp.where(picked, jnp.where(far, far_bias, 0.0), NEG))
        pen_hi = pen.astype(BF16)
        pen_lo = jnp.where(far, pen - pen_hi.astype(F32), 0.0).astype(BF16)
        return jnp.concatenate(
            [(qf * (scale * LOG2E)).astype(BF16), pen_hi, pen_lo, jnp.zeros((HEAD_DIM - 2 * nb, tq), BF16)],
            axis=0)

    q_aug = [widened_query(hh) for hh in heads]
    n_groups = i // KV_GROUP + 1

    def scores(grp, dst_ref):
        r0 = pl.multiple_of(grp * GROUP_KEYS, GROUP_KEYS)
        ind = ind_ref[pl.ds(r0, GROUP_KEYS), :]
        for hh in heads:
            k_aug = jnp.concatenate([k_ref[pl.ds(r0, GROUP_KEYS), lanes[hh]], ind], axis=1)
            dst_ref[hh] = jnp.dot(k_aug, q_aug[hh], preferred_element_type=F32)

    def softmax(grp, src_ref, dst_ref, m_prev):
        m_out, alpha_out = [], []
        for hh in heads:
            parts = []
            for u in range(KV_GROUP):
                delta = jnp.clip(i - (grp * KV_GROUP + u), 0, N_BIAS_TILES - 1)
                parts.append(src_ref[hh, u * MOBA_BLOCK:(u + 1) * MOBA_BLOCK, :] + bias_ref[hh, delta])
            s = jnp.concatenate(parts, axis=0)
            m_new = jnp.maximum(m_prev[hh], jnp.max(s, axis=0, keepdims=True))
            dst_ref[hh] = jnp.exp2(s - m_new).astype(BF16)
            m_out.append(m_new)
            alpha_out.append(jnp.exp2(m_prev[hh] - m_new))
        return tuple(m_out), tuple(alpha_out)

    def weighted_values(grp, live, src_ref, alpha):
        for hh in heads:
            pv = jnp.dot(vt_ref[grp, hh * V_ROWS:(hh + 1) * V_ROWS, :], src_ref[hh], preferred_element_type=F32)
            acc_ref[hh] = alpha[hh] * acc_ref[hh] + jnp.where(live, pv, 0.0)

    acc_ref[...] = jnp.zeros_like(acc_ref)
    pb_ref[...] = jnp.zeros_like(pb_ref)
    scores(n_groups - 1, sa_ref)

    def body(u, carry):
        m_prev, alpha_b, grp_b_prev = carry
        grp_a = n_groups - 1 - 2 * u
        grp_b = grp_a - 1
        weighted_values(jnp.maximum(grp_b_prev, 0), grp_b_prev >= 0, pb_ref, alpha_b)
        m_a, alpha_a = softmax(grp_a, sa_ref, pa_ref, m_prev)
        scores(jnp.maximum(grp_b, 0), sb_ref)
        weighted_values(grp_a, True, pa_ref, alpha_a)
        m_b, alpha_b = softmax(jnp.maximum(grp_b, 0), sb_ref, pb_ref, m_a)
        scores(jnp.maximum(grp_b - 1, 0), sa_ref)
        return m_b, alpha_b, grp_b

    m0 = tuple(jnp.full((1, tq), NEG, F32) for _ in heads)
    a0 = tuple(jnp.ones((1, tq), F32) for _ in heads)
    _, alpha_b, grp_b = lax.fori_loop(0, (n_groups + 1) // 2, body, (m0, a0, jnp.int32(-1)))
    weighted_values(jnp.maximum(grp_b, 0), grp_b >= 0, pb_ref, alpha_b)
    for hh in heads:
        acc = acc_ref[hh]
        o_ref[:, lanes[hh]] = (acc[:HEAD_DIM] / acc[HEAD_DIM:HEAD_DIM + 1]).T.astype(o_ref.dtype)


def _attention(k, km, q_t, v_t, bias, rel_bias):
    S = k.shape[0]
    nb = S // MOBA_BLOCK
    tq = MOBA_BLOCK
    per = INPROJ_ROWS // tq
    width = ATTN_HEADS * HEAD_DIM
    key_blk = jnp.arange(S, dtype=jnp.int32)[:, None] // MOBA_BLOCK
    lane = jnp.arange(HEAD_DIM, dtype=jnp.int32)[None, :]
    ind = jnp.logical_and(lane < 2 * nb, lane % nb == key_blk).astype(BF16)
    return pl.pallas_call(
        _attn_kernel,
        grid=(N_HEADS // ATTN_HEADS, S // tq),
        in_specs=[pl.BlockSpec(memory_space=pltpu.SMEM),
                  pl.BlockSpec((1, width, tq), lambda h, i: (i // per, h, i % per)),
                  pl.BlockSpec((S, width), lambda h, i: (0, h)),
                  pl.BlockSpec((S // GROUP_KEYS, ATTN_HEADS * V_ROWS, GROUP_KEYS), lambda h, i: (0, h, 0)),
                  pl.BlockSpec((nb, width), lambda h, i: (0, h)),
                  pl.BlockSpec((S, HEAD_DIM), lambda h, i: (0, 0)),
                  pl.BlockSpec((ATTN_HEADS, N_BIAS_TILES, MOBA_BLOCK, MOBA_BLOCK), lambda h, i: (h, 0, 0, 0))],
        out_specs=pl.BlockSpec((tq, width), lambda h, i: (i, h)),
        out_shape=jax.ShapeDtypeStruct((S, D_ATTN), BF16),
        scratch_shapes=[pltpu.VMEM((ATTN_HEADS, HEAD_DIM + SUM_ROWS, tq), F32),
                        pltpu.VMEM((ATTN_HEADS, GROUP_KEYS, tq), F32),
                        pltpu.VMEM((ATTN_HEADS, GROUP_KEYS, tq), F32),
                        pltpu.VMEM((ATTN_HEADS, GROUP_KEYS, tq), BF16),
                        pltpu.VMEM((ATTN_HEADS, GROUP_KEYS, tq), BF16)],
        compiler_params=_params(("parallel", "parallel")),
        name="moba_attention",
    )(rel_bias, q_t, k, v_t, km, ind, bias)


def _mix_kernel(cs_ref, at_ref, gate_ref, x_ref, wc_ref, wa_ref, wo_ref, o_ref):
    D = x_ref.shape[1]
    yc = jnp.dot(cs_ref[...], wc_ref[0], preferred_element_type=F32)
    ya = jnp.dot(at_ref[...], wa_ref[0], preferred_element_type=F32)
    merged = gate_ref[:, :D].astype(F32) * yc + gate_ref[:, D:].astype(F32) * ya
    o_ref[...] = x_ref[...] + jnp.dot(merged.astype(BF16), wo_ref[0], preferred_element_type=F32)


def _mix(cs, attn, gates, x, wc, wa, wo, layer, tm=256):
    S, D = x.shape
    full = lambda shape: pl.BlockSpec(shape, lambda i: (layer, 0, 0))
    return pl.pallas_call(
        _mix_kernel,
        grid=(S // tm,),
        in_specs=[pl.BlockSpec((tm, D_CONV), lambda i: (i, 0)),
                  pl.BlockSpec((tm, D_ATTN), lambda i: (i, 0)),
                  pl.BlockSpec((tm, 2 * D), lambda i: (i, 0)),
                  pl.BlockSpec((tm, D), lambda i: (i, 0)),
                  full((1, D_CONV, D)), full((1, D_ATTN, D)), full((1, D, D))],
        out_specs=pl.BlockSpec((tm, D), lambda i: (i, 0)),
        out_shape=jax.ShapeDtypeStruct((S, D), F32),
        compiler_params=_params(("parallel",)),
        name="mix_out_proj",
    )(cs, attn, gates, x, wc, wa, wo)


def _ffn_kernel(xh_ref, x_ref, g_ref, wa_ref, wb_ref, dwa_ref, dwb_ref, ba_ref, bb_ref, wd_ref,
                fg_ref, o_ref, h_ref, gate_ref, *, final_norm, n_tiles, up_chunk):
    i = pl.program_id(0)
    f = pl.program_id(1)
    tm = x_ref.shape[0]
    tf = wa_ref.shape[2]
    slot = f % 2

    def up_stage():
        h = h_ref[...]
        for c0 in range(0, tf, up_chunk):
            cols = slice(c0, c0 + up_chunk)

            def up_conv(w_ref, dw_ref, b_ref):
                u = jnp.dot(h, w_ref[0, :, cols], preferred_element_type=F32)
                y = (u * dw_ref[2:3, cols] + pltpu.roll(u, 1, 0) * dw_ref[1:2, cols]
                     + pltpu.roll(u, 2, 0) * dw_ref[0:1, cols] + b_ref[:, cols])
                return y[HALO:]

            ua = up_conv(wa_ref, dwa_ref, ba_ref)
            ub = up_conv(wb_ref, dwb_ref, bb_ref)
            gate_ref[slot, :, cols] = (jax.nn.silu(ub) * ua).astype(BF16)

    def down_product():
        return jnp.dot(gate_ref[1 - slot], wd_ref[0], preferred_element_type=F32)

    @pl.when(f == 0)
    def _():
        xh = xh_ref[...]
        ms = jnp.mean(xh * xh, axis=-1, keepdims=True)
        hh = xh * lax.rsqrt(ms + EPS) * g_ref[...]
        h_ref[0:HALO, :] = jnp.where(i > 0, hh, jnp.zeros_like(hh)).astype(BF16)
        _rmsnorm_rows(x_ref, g_ref, h_ref, tm, dst_off=HALO)
        o_ref[...] = jnp.zeros_like(o_ref)
        up_stage()

    @pl.when(jnp.logical_and(f > 0, f < n_tiles))
    def _():
        o_ref[...] += down_product()
        up_stage()

    @pl.when(f == n_tiles)
    def _():
        y = x_ref[...] + o_ref[...] + down_product()
        if final_norm:
            ms = jnp.mean(y * y, axis=-1, keepdims=True)
            y = y * lax.rsqrt(ms + EPS) * fg_ref[...]
        o_ref[...] = y


def _ffn(x, g, w_up, dw, dwb, w_down, layer, final_g, final_norm, tm=1024, tf=W_TILE):
    S, D = x.shape
    nf = D_FF // tf
    per = tm // HALO
    row = lambda shape: pl.BlockSpec(shape, lambda i, f: (0, 0))
    up = lambda f: jnp.minimum(f, nf - 1)
    down = lambda f: jnp.maximum(f - 1, 0)
    return pl.pallas_call(
        functools.partial(_ffn_kernel, final_norm=final_norm, n_tiles=nf, up_chunk=tf // 2),
        grid=(S // tm, nf + 1),
        in_specs=[pl.BlockSpec((HALO, D), lambda i, f: (jnp.maximum(i * per - 1, 0), 0)),
                  pl.BlockSpec((tm, D), lambda i, f: (i, 0), pipeline_mode=pl.Buffered(1)),
                  row((1, D)),
                  pl.BlockSpec((1, D, tf), lambda i, f: (layer, 0, up(f))),
                  pl.BlockSpec((1, D, tf), lambda i, f: (layer, 0, up(f) + nf)),
                  pl.BlockSpec((FFN_CONV_K, tf), lambda i, f: (0, up(f))),
                  pl.BlockSpec((FFN_CONV_K, tf), lambda i, f: (0, up(f) + nf)),
                  pl.BlockSpec((1, tf), lambda i, f: (0, up(f))),
                  pl.BlockSpec((1, tf), lambda i, f: (0, up(f) + nf)),
                  pl.BlockSpec((1, tf, D), lambda i, f: (layer, down(f), 0)),
                  row((1, D))],
        out_specs=pl.BlockSpec((tm, D), lambda i, f: (i, 0)),
        out_shape=jax.ShapeDtypeStruct((S, D), F32),
        scratch_shapes=[pltpu.VMEM((HALO + tm, D), BF16), pltpu.VMEM((2, tm, tf), BF16)],
        compiler_params=_params(("parallel", "arbitrary")),
        name="conv_ffn",
    )(x, x, g, w_up, w_up, dw, dw, dwb, dwb, w_down, final_g)


def kernel(x, norm1_g, w_in, conv_dw, conv_dw_b, conv_ln_g, conv_ln_b, w_conv_out, rel_bias,
           w_attn_out, w_out, norm2_g, w_up, ffn_dw, ffn_dw_b, w_down, final_g):
    B, S, D = x.shape
    assert D == D_MODEL and S % INPROJ_ROWS == 0 and INPROJ_ROWS % GROUP_KEYS == 0
    depth = w_in.shape[0]
    bias = _bias_tiles(rel_bias)
    row = lambda v: v.reshape(1, -1)
    w_in_bf = w_in.astype(BF16)
    w_conv_bf, w_attn_bf, w_out_bf = (w.astype(BF16) for w in (w_conv_out, w_attn_out, w_out))
    w_up_bf, w_down_bf = w_up.astype(BF16), w_down.astype(BF16)
    outs = []
    for b in range(B):
        xb = x.reshape(S, D) if B == 1 else x[b]
        for l in range(depth):
            glu, k, km, q_t, v_t, gates = _inproj_fused(xb, row(norm1_g[l]), w_in_bf, l)
            cs = _conv_branch(glu, conv_dw[l], row(conv_dw_b[l]), row(conv_ln_g[l]), row(conv_ln_b[l]))
            attn = _attention(k, km, q_t, v_t, bias, rel_bias)
            xb = _mix(cs, attn, gates, xb, w_conv_bf, w_attn_bf, w_out_bf, l)
            xb = _ffn(xb, row(norm2_g[l]), w_up_bf, ffn_dw[l], row(ffn_dw_b[l]),
                      w_down_bf, l, row(final_g), final_norm=(l == depth - 1))
        outs.append(xb)
    return outs[0].reshape(1, S, D) if B == 1 else jnp.stack(outs, axis=0)
```

```python
import functools
import math

import jax
import jax.numpy as jnp
from jax import lax
from jax.experimental import pallas as pl
from jax.experimental.pallas import tpu as pltpu

F32 = jnp.float32
BF16 = jnp.bfloat16

D_MODEL = 2048
D_CONV = 1024
CONV_K = 31
N_HEADS = 8
HEAD_DIM = 128
D_ATTN = N_HEADS * HEAD_DIM
MOBA_BLOCK = 256
MOBA_TOPK = 3
N_BUCKETS = 32
REL_MAX_DIST = 2048
D_FF = 5632
FFN_CONV_K = 3
EPS = 1e-6
NEG = -1e30
LOG2E = math.log2(math.e)

N_BIAS_TILES = 8
BIAS_ROWS = 16
KV_GROUP = 2
GROUP_KEYS = KV_GROUP * MOBA_BLOCK
INPROJ_ROWS = 1024
SUM_ROWS = 16
V_ROWS = HEAD_DIM + SUM_ROWS
ATTN_HEADS = 2

V7X_VMEM_BYTES = 64 * 1024 * 1024
VMEM_LIMIT = V7X_VMEM_BYTES - 8 * 1024 * 1024

SUBLANES = 8
NORM_ROWS = 16
NORM_UNROLL = 4
HALO = 8
CONV_HALO = 32

W_TILE = 512

TN_DIMS = (((0,), (1,)), ((), ()))


def _params(sem):
    return pltpu.CompilerParams(dimension_semantics=sem, vmem_limit_bytes=VMEM_LIMIT)


def _rmsnorm_rows(x_ref, g_ref, dst_ref, n_rows, dst_off=0, out_dtype=BF16):
    def body(c, carry):
        r = pl.multiple_of(c * NORM_ROWS, NORM_ROWS)
        xv = x_ref[pl.ds(r, NORM_ROWS), :]
        ms = jnp.mean(xv * xv, axis=-1, keepdims=True)
        y = xv * lax.rsqrt(ms + EPS) * g_ref[...]
        dst_ref[pl.ds(dst_off + r, NORM_ROWS), :] = y.astype(out_dtype)
        return carry
    lax.fori_loop(0, n_rows // NORM_ROWS, body, 0, unroll=NORM_UNROLL)


def _inproj_kernel(x_ref, g_ref, w_ref, w2_ref, glu_ref, k_ref, km_ref, q_ref, v_ref, gate_ref,
                   h_ref, *, seg):
    j = pl.program_id(1)
    n_glu, n_k, n_q, n_v, _ = seg
    k0 = n_glu
    q0 = k0 + n_k
    v0 = q0 + n_q
    g0 = v0 + n_v

    @pl.when(j == 0)
    def _():
        _rmsnorm_rows(x_ref, g_ref, h_ref, x_ref.shape[0])

    @pl.when(j < k0)
    def _():
        h = h_ref[...]
        a = jnp.dot(h, w_ref[0], preferred_element_type=F32)
        b = jnp.dot(h, w2_ref[0], preferred_element_type=F32)
        glu_ref[...] = a * jax.nn.sigmoid(b)

    @pl.when(jnp.logical_and(j >= k0, j < q0))
    def _():
        acc = jnp.dot(h_ref[...], w_ref[0], preferred_element_type=F32)
        k_ref[...] = acc.astype(k_ref.dtype)
        for b in range(acc.shape[0] // MOBA_BLOCK):
            blk = acc[b * MOBA_BLOCK:(b + 1) * MOBA_BLOCK]
            km_ref[0, b:b + 1, :] = jnp.mean(blk, axis=0, keepdims=True)

    @pl.when(jnp.logical_and(j >= q0, j < v0))
    def _():
        acc = lax.dot_general(w_ref[0], h_ref[...], TN_DIMS, preferred_element_type=F32)
        q_ref[0] = acc.astype(q_ref.dtype)

    @pl.when(jnp.logical_and(j >= v0, j < g0))
    def _():
        acc = lax.dot_general(w_ref[0], h_ref[...], TN_DIMS, preferred_element_type=F32)
        for grp in range(v_ref.shape[0]):
            pos = slice(grp * GROUP_KEYS, (grp + 1) * GROUP_KEYS)
            for hh in range(acc.shape[0] // HEAD_DIM):
                r = hh * V_ROWS
                v_ref[grp, r:r + HEAD_DIM, :] = acc[hh * HEAD_DIM:(hh + 1) * HEAD_DIM, pos].astype(v_ref.dtype)
                v_ref[grp, r + HEAD_DIM:r + V_ROWS, :] = jnp.ones((SUM_ROWS, GROUP_KEYS), v_ref.dtype)

    @pl.when(j >= g0)
    def _():
        acc = jnp.dot(h_ref[...], w_ref[0], preferred_element_type=F32)
        gate_ref[...] = jax.nn.sigmoid(acc).astype(gate_ref.dtype)


def _inproj_fused(x, g, w_in, layer, tn=W_TILE):
    S, D = x.shape
    tm = INPROJ_ROWS
    nm = S // tm
    groups = tm // GROUP_KEYS
    seg = (D_CONV // tn, D_ATTN // tn, D_ATTN // tn, D_ATTN // tn, 2 * D_MODEL // tn)
    n_glu, n_k, n_q, n_v, n_g = seg
    k0, q0 = n_glu, n_glu + n_k
    v0 = q0 + n_q
    g0 = v0 + n_v
    col_q = 2 * D_CONV // tn
    col_k = col_q + n_q

    def w_col(j):
        return jnp.where(j < k0, j, jnp.where(j < q0, col_k + (j - k0),
                         jnp.where(j < v0, col_q + (j - q0), col_k + n_k + (j - v0))))

    seg_tile = lambda j, start, n: jnp.clip(j - start, 0, n - 1)
    v_tile_rows = tn // HEAD_DIM * V_ROWS
    glu, k, km, q_t, v_t, gates = pl.pallas_call(
        functools.partial(_inproj_kernel, seg=seg),
        grid=(nm, sum(seg)),
        in_specs=[pl.BlockSpec((tm, D), lambda i, j: (i, 0)),
                  pl.BlockSpec((1, D), lambda i, j: (0, 0)),
                  pl.BlockSpec((1, D, tn), lambda i, j: (layer, 0, w_col(j))),
                  pl.BlockSpec((1, D, tn), lambda i, j: (layer, 0, D_CONV // tn + seg_tile(j, 0, n_glu)))],
        out_specs=[pl.BlockSpec((tm, tn), lambda i, j: (i, seg_tile(j, 0, n_glu))),
                   pl.BlockSpec((tm, tn), lambda i, j: (i, seg_tile(j, k0, n_k))),
                   pl.BlockSpec((1, tm // MOBA_BLOCK, tn), lambda i, j: (i, 0, seg_tile(j, k0, n_k))),
                   pl.BlockSpec((1, tn, tm), lambda i, j: (i, seg_tile(j, q0, n_q), 0)),
                   pl.BlockSpec((groups, v_tile_rows, GROUP_KEYS), lambda i, j: (i, seg_tile(j, v0, n_v), 0)),
                   pl.BlockSpec((tm, tn), lambda i, j: (i, seg_tile(j, g0, n_g)))],
        out_shape=[jax.ShapeDtypeStruct((S, D_CONV), F32),
                   jax.ShapeDtypeStruct((S, D_ATTN), BF16),
                   jax.ShapeDtypeStruct((nm, tm // MOBA_BLOCK, D_ATTN), F32),
                   jax.ShapeDtypeStruct((nm, D_ATTN, tm), BF16),
                   jax.ShapeDtypeStruct((S // GROUP_KEYS, N_HEADS * V_ROWS, GROUP_KEYS), BF16),
                   jax.ShapeDtypeStruct((S, 2 * D_MODEL), BF16)],
        scratch_shapes=[pltpu.VMEM((tm, D), BF16)],
        compiler_params=_params(("parallel", "arbitrary")),
        name="inproj",
    )(x, g, w_in, w_in)
    return glu, k, km.reshape(S // MOBA_BLOCK, D_ATTN), q_t, v_t, gates


def _conv_kernel(halo_ref, x_ref, w_ref, b_ref, lg_ref, lb_ref, o_ref, win_ref, y_ref,
                 *, rows, lanes):
    tm, C = x_ref.shape
    i = pl.program_id(0)
    halo = halo_ref[...]
    win_ref[0:CONV_HALO, :] = jnp.where(i > 0, halo, jnp.zeros_like(halo))
    win_ref[CONV_HALO:, :] = x_ref[...]
    first = CONV_HALO - (CONV_K - 1)

    for c0 in range(0, C, lanes):
        def body(rc, carry):
            r = pl.multiple_of(rc * rows, rows)
            acc = jnp.zeros((rows, lanes), F32) + b_ref[:, c0:c0 + lanes]
            win = win_ref[pl.ds(r, rows + CONV_HALO), c0:c0 + lanes]
            for sub in range(SUBLANES):
                shifted = win if sub == 0 else pltpu.roll(win, win.shape[0] - sub, 0)
                usable = win.shape[0] - (SUBLANES if sub else 0)
                for base in range(0, usable - rows + 1, SUBLANES):
                    k = base + sub - first
                    if 0 <= k < CONV_K:
                        acc = acc + shifted[base:base + rows] * w_ref[k:k + 1, c0:c0 + lanes]
            y_ref[pl.ds(r, rows), c0:c0 + lanes] = acc
            return carry
        lax.fori_loop(0, tm // rows, body, 0)

    def ln_body(rc, carry):
        r = pl.multiple_of(rc * NORM_ROWS, NORM_ROWS)
        v = y_ref[pl.ds(r, NORM_ROWS), :]
        mu = jnp.mean(v, axis=-1, keepdims=True)
        var = jnp.mean(jnp.square(v - mu), axis=-1, keepdims=True)
        z = (v - mu) * lax.rsqrt(var + EPS) * lg_ref[...] + lb_ref[...]
        o_ref[pl.ds(r, NORM_ROWS), :] = jax.nn.silu(z).astype(o_ref.dtype)
        return carry
    lax.fori_loop(0, tm // NORM_ROWS, ln_body, 0, unroll=NORM_UNROLL)


def _conv_branch(glu, w, b, lg, lb, tm=256):
    S, C = glu.shape
    per = tm // CONV_HALO
    return pl.pallas_call(
        functools.partial(_conv_kernel, rows=32, lanes=256),
        grid=(S // tm,),
        in_specs=[pl.BlockSpec((CONV_HALO, C), lambda i: (jnp.maximum(i * per - 1, 0), 0)),
                  pl.BlockSpec((tm, C), lambda i: (i, 0)),
                  pl.BlockSpec((CONV_K, C), lambda i: (0, 0)),
                  pl.BlockSpec((1, C), lambda i: (0, 0)),
                  pl.BlockSpec((1, C), lambda i: (0, 0)),
                  pl.BlockSpec((1, C), lambda i: (0, 0))],
        out_specs=pl.BlockSpec((tm, C), lambda i: (i, 0)),
        out_shape=jax.ShapeDtypeStruct((S, C), BF16),
        scratch_shapes=[pltpu.VMEM((tm + CONV_HALO, C), F32), pltpu.VMEM((tm, C), F32)],
        compiler_params=_params(("parallel",)),
        name="conv_branch",
    )(glu, glu, w, b, lg, lb)


def _bias_kernel(rb_ref, o_ref):
    delta = pl.program_id(0)
    shape = (BIAS_ROWS, MOBA_BLOCK)

    def body(c, carry):
        c0 = pl.multiple_of(c * BIAS_ROWS, BIAS_ROWS)
        key = c0 + lax.broadcasted_iota(jnp.int32, shape, 0)
        d = delta * MOBA_BLOCK + lax.broadcasted_iota(jnp.int32, shape, 1) - key
        n = jnp.maximum(d, 0)
        max_exact = N_BUCKETS // 2
        nf = jnp.maximum(n, 1).astype(F32)
        large = max_exact + (jnp.log(nf / max_exact) / math.log(REL_MAX_DIST / max_exact)
                             * (N_BUCKETS - max_exact)).astype(jnp.int32)
        large = jnp.minimum(large, N_BUCKETS - 1)
        bucket = jnp.where(n < max_exact, n, large)
        vals = [jnp.zeros(shape, F32) for _ in range(N_HEADS)]
        for b in range(N_BUCKETS):
            hit = bucket == b
            vals = [jnp.where(hit, rb_ref[b, h], vals[h]) for h in range(N_HEADS)]
        for h in range(N_HEADS):
            val = jnp.where(delta == N_BIAS_TILES - 1, 0.0, vals[h] * LOG2E)
            o_ref[h, 0, pl.ds(c0, BIAS_ROWS), :] = jnp.where(d >= 0, val, NEG)
        return carry
    lax.fori_loop(0, MOBA_BLOCK // BIAS_ROWS, body, 0)


def _bias_tiles(rel_bias):
    return pl.pallas_call(
        _bias_kernel,
        grid=(N_BIAS_TILES,),
        in_specs=[pl.BlockSpec(memory_space=pltpu.SMEM)],
        out_specs=pl.BlockSpec((N_HEADS, 1, MOBA_BLOCK, MOBA_BLOCK), lambda d: (0, d, 0, 0)),
        out_shape=jax.ShapeDtypeStruct((N_HEADS, N_BIAS_TILES, MOBA_BLOCK, MOBA_BLOCK), F32),
        compiler_params=_params(("parallel",)),
        name="t5_bias_tiles",
    )(rel_bias)


def _attn_kernel(rb_ref, qt_ref, k_ref, vt_ref, km_ref, ind_ref, bias_ref, o_ref,
                 acc_ref, sa_ref, sb_ref, pa_ref, pb_ref):
    i = pl.program_id(1)
    nb = km_ref.shape[0]
    tq = qt_ref.shape[2]
    scale = HEAD_DIM ** -0.5
    heads = range(ATTN_HEADS)
    lanes = [slice(hh * HEAD_DIM, (hh + 1) * HEAD_DIM) for hh in heads]

    def widened_query(hh):
        qf = qt_ref[0, lanes[hh], :].astype(F32)
        gate = jnp.dot(km_ref[:, lanes[hh]], qf, precision=lax.Precision.HIGHEST, preferred_element_type=F32)
        row = lax.broadcasted_iota(jnp.int32, (nb, tq), 0)
        valid = row < i
        g = jnp.where(valid, gate, NEG)
        sel = jnp.zeros((nb, tq), jnp.bool_)
        for _ in range(MOBA_TOPK):
            top = jnp.max(g, axis=0, keepdims=True)
            idx = jnp.min(jnp.where(g == top, row, nb), axis=0, keepdims=True)
            hit = row == idx
            sel = jnp.logical_or(sel, hit)
            g = jnp.where(hit, -jnp.inf, g)
        picked = jnp.logical_and(sel, valid)
        far = jnp.logical_and(picked, row <= i - (N_BIAS_TILES - 1))
        far_bias = rb_ref[N_BUCKETS - 1, pl.program_id(0) * ATTN_HEADS + hh] * LOG2E
        pen = jnp.where(row == i, 0.0, jnp.where(picked, jnp.where(far, far_bias, 0.0), NEG))
        pen_hi = pen.astype(BF16)
        pen_lo = jnp.where(far, pen - pen_hi.astype(F32), 0.0).astype(BF16)
        return jnp.concatenate(
            [(qf * (scale * LOG2E)).astype(BF16), pen_hi, pen_lo, jnp.zeros((HEAD_DIM - 2 * nb, tq), BF16)],
            axis=0)

    q_aug = [widened_query(hh) for hh in heads]
    n_groups = i // KV_GROUP + 1

    def scores(grp, dst_ref):
        r0 = pl.multiple_of(grp * GROUP_KEYS, GROUP_KEYS)
        ind = ind_ref[pl.ds(r0, GROUP_KEYS), :]
        for hh in heads:
            k_aug = jnp.concatenate([k_ref[pl.ds(r0, GROUP_KEYS), lanes[hh]], ind], axis=1)
            dst_ref[hh] = jnp.dot(k_aug, q_aug[hh], preferred_element_type=F32)

    def softmax(grp, src_ref, dst_ref, m_prev):
        m_out, alpha_out = [], []
        for hh in heads:
            parts = []
            for u in range(KV_GROUP):
                delta = jnp.clip(i - (grp * KV_GROUP + u), 0, N_BIAS_TILES - 1)
                parts.append(src_ref[hh, u * MOBA_BLOCK:(u + 1) * MOBA_BLOCK, :] + bias_ref[hh, delta])
            s = jnp.concatenate(parts, axis=0)
            m_new = jnp.maximum(m_prev[hh], jnp.max(s, axis=0, keepdims=True))
            dst_ref[hh] = jnp.exp2(s - m_new).astype(BF16)
            m_out.append(m_new)
            alpha_out.append(jnp.exp2(m_prev[hh] - m_new))
        return tuple(m_out), tuple(alpha_out)

    def weighted_values(grp, live, src_ref, alpha):
        for hh in heads:
            pv = jnp.dot(vt_ref[grp, hh * V_ROWS:(hh + 1) * V_ROWS, :], src_ref[hh], preferred_element_type=F32)
            acc_ref[hh] = alpha[hh] * acc_ref[hh] + jnp.where(live, pv, 0.0)

    acc_ref[...] = jnp.zeros_like(acc_ref)
    pb_ref[...] = jnp.zeros_like(pb_ref)
    scores(n_groups - 1, sa_ref)

    def body(u, carry):
        m_prev, alpha_b, grp_b_prev = carry
        grp_a = n_groups - 1 - 2 * u
        grp_b = grp_a - 1
        weighted_values(jnp.maximum(grp_b_prev, 0), grp_b_prev >= 0, pb_ref, alpha_b)
        m_a, alpha_a = softmax(grp_a, sa_ref, pa_ref, m_prev)
        scores(jnp.maximum(grp_b, 0), sb_ref)
        weighted_values(grp_a, True, pa_ref, alpha_a)
        m_b, alpha_b = softmax(jnp.maximum(grp_b, 0), sb_ref, pb_ref, m_a)
        scores(jnp.maximum(grp_b - 1, 0), sa_ref)
        return m_b, alpha_b, grp_b

    m0 = tuple(jnp.full((1, tq), NEG, F32) for _ in heads)
    a0 = tuple(jnp.ones((1, tq), F32) for _ in heads)
    _, alpha_b, grp_b = lax.fori_loop(0, (n_groups + 1) // 2, body, (m0, a0, jnp.int32(-1)))
    weighted_values(jnp.maximum(grp_b, 0), grp_b >= 0, pb_ref, alpha_b)
    for hh in heads:
        acc = acc_ref[hh]
        o_ref[:, lanes[hh]] = (acc[:HEAD_DIM] / acc[HEAD_DIM:HEAD_DIM + 1]).T.astype(o_ref.dtype)


def _attention(k, km, q_t, v_t, bias, rel_bias):
    S = k.shape[0]
    nb = S // MOBA_BLOCK
    tq = MOBA_BLOCK
    per = INPROJ_ROWS // tq
    width = ATTN_HEADS * HEAD_DIM
    key_blk = jnp.arange(S, dtype=jnp.int32)[:, None] // MOBA_BLOCK
    lane = jnp.arange(HEAD_DIM, dtype=jnp.int32)[None, :]
    ind = jnp.logical_and(lane < 2 * nb, lane % nb == key_blk).astype(BF16)
    return pl.pallas_call(
        _attn_kernel,
        grid=(N_HEADS // ATTN_HEADS, S // tq),
        in_specs=[pl.BlockSpec(memory_space=pltpu.SMEM),
                  pl.BlockSpec((1, width, tq), lambda h, i: (i // per, h, i % per)),
                  pl.BlockSpec((S, width), lambda h, i: (0, h)),
                  pl.BlockSpec((S // GROUP_KEYS, ATTN_HEADS * V_ROWS, GROUP_KEYS), lambda h, i: (0, h, 0)),
                  pl.BlockSpec((nb, width), lambda h, i: (0, h)),
                  pl.BlockSpec((S, HEAD_DIM), lambda h, i: (0, 0)),
                  pl.BlockSpec((ATTN_HEADS, N_BIAS_TILES, MOBA_BLOCK, MOBA_BLOCK), lambda h, i: (h, 0, 0, 0))],
        out_specs=pl.BlockSpec((tq, width), lambda h, i: (i, h)),
        out_shape=jax.ShapeDtypeStruct((S, D_ATTN), BF16),
        scratch_shapes=[pltpu.VMEM((ATTN_HEADS, HEAD_DIM + SUM_ROWS, tq), F32),
                        pltpu.VMEM((ATTN_HEADS, GROUP_KEYS, tq), F32),
                        pltpu.VMEM((ATTN_HEADS, GROUP_KEYS, tq), F32),
                        pltpu.VMEM((ATTN_HEADS, GROUP_KEYS, tq), BF16),
                        pltpu.VMEM((ATTN_HEADS, GROUP_KEYS, tq), BF16)],
        compiler_params=_params(("parallel", "parallel")),
        name="moba_attention",
    )(rel_bias, q_t, k, v_t, km, ind, bias)


def _mix_kernel(cs_ref, at_ref, gate_ref, x_ref, wc_ref, wa_ref, wo_ref, o_ref):
    D = x_ref.shape[1]
    yc = jnp.dot(cs_ref[...], wc_ref[0], preferred_element_type=F32)
    ya = jnp.dot(at_ref[...], wa_ref[0], preferred_element_type=F32)
    merged = gate_ref[:, :D].astype(F32) * yc + gate_ref[:, D:].astype(F32) * ya
    o_ref[...] = x_ref[...] + jnp.dot(merged.astype(BF16), wo_ref[0], preferred_element_type=F32)


def _mix(cs, attn, gates, x, wc, wa, wo, layer, tm=512):
    S, D = x.shape
    full = lambda shape: pl.BlockSpec(shape, lambda i: (layer, 0, 0), pipeline_mode=pl.Buffered(1))
    return pl.pallas_call(
        _mix_kernel,
        grid=(S // tm,),
        in_specs=[pl.BlockSpec((tm, D_CONV), lambda i: (i, 0)),
                  pl.BlockSpec((tm, D_ATTN), lambda i: (i, 0)),
                  pl.BlockSpec((tm, 2 * D), lambda i: (i, 0)),
                  pl.BlockSpec((tm, D), lambda i: (i, 0)),
                  full((1, D_CONV, D)), full((1, D_ATTN, D)), full((1, D, D))],
        out_specs=pl.BlockSpec((tm, D), lambda i: (i, 0)),
        out_shape=jax.ShapeDtypeStruct((S, D), F32),
        compiler_params=_params(("parallel",)),
        name="mix_out_proj",
    )(cs, attn, gates, x, wc, wa, wo)


def _ffn_kernel(xh_ref, x_ref, g_ref, wa_ref, wb_ref, dw_ref, b_ref, wd_ref,
                fg_ref, o_ref, h_ref, gate_ref, *, final_norm, n_tiles, up_chunk):
    i = pl.program_id(0)
    f = pl.program_id(1)
    tm = x_ref.shape[0]
    tf = wa_ref.shape[2]
    slot = f % 2

    def up_stage():
        h = h_ref[...]
        tile = jnp.minimum(f, n_tiles - 1)
        for c0 in range(0, tf, up_chunk):
            cols = slice(c0, c0 + up_chunk)

            def up_conv(w_ref, t):
                u = jnp.dot(h, w_ref[0, :, cols], preferred_element_type=F32)
                y = (u * dw_ref[t, 2:3, cols] + pltpu.roll(u, 1, 0) * dw_ref[t, 1:2, cols]
                     + pltpu.roll(u, 2, 0) * dw_ref[t, 0:1, cols] + b_ref[t, :, cols])
                return y[HALO:]

            ua = up_conv(wa_ref, tile)
            ub = up_conv(wb_ref, tile + n_tiles)
            gate_ref[slot, :, cols] = (jax.nn.silu(ub) * ua).astype(BF16)

    def down_product():
        return jnp.dot(gate_ref[1 - slot], wd_ref[0], preferred_element_type=F32)

    @pl.when(f == 0)
    def _():
        xh = xh_ref[...]
        ms = jnp.mean(xh * xh, axis=-1, keepdims=True)
        hh = xh * lax.rsqrt(ms + EPS) * g_ref[...]
        h_ref[0:HALO, :] = jnp.where(i > 0, hh, jnp.zeros_like(hh)).astype(BF16)
        _rmsnorm_rows(x_ref, g_ref, h_ref, tm, dst_off=HALO)
        o_ref[...] = jnp.zeros_like(o_ref)
        up_stage()

    @pl.when(jnp.logical_and(f > 0, f < n_tiles))
    def _():
        o_ref[...] += down_product()
        up_stage()

    @pl.when(f == n_tiles)
    def _():
        y = x_ref[...] + o_ref[...] + down_product()
        if final_norm:
            ms = jnp.mean(y * y, axis=-1, keepdims=True)
            y = y * lax.rsqrt(ms + EPS) * fg_ref[...]
        o_ref[...] = y


def _ffn(x, g, w_up, dw, dwb, w_down, layer, final_g, final_norm, tm=1024, tf=W_TILE):
    S, D = x.shape
    nf = D_FF // tf
    per = tm // HALO
    dw_tiles = dw.reshape(FFN_CONV_K, 2 * nf, tf).transpose(1, 0, 2)
    b_tiles = dwb.reshape(2 * nf, 1, tf)
    row = lambda shape: pl.BlockSpec(shape, lambda i, f: (0, 0))
    up = lambda f: jnp.minimum(f, nf - 1)
    down = lambda f: jnp.maximum(f - 1, 0)
    return pl.pallas_call(
        functools.partial(_ffn_kernel, final_norm=final_norm, n_tiles=nf, up_chunk=tf // 2),
        grid=(S // tm, nf + 1),
        in_specs=[pl.BlockSpec((HALO, D), lambda i, f: (jnp.maximum(i * per - 1, 0), 0)),
                  pl.BlockSpec((tm, D), lambda i, f: (i, 0), pipeline_mode=pl.Buffered(1)),
                  row((1, D)),
                  pl.BlockSpec((1, D, tf), lambda i, f: (layer, 0, up(f))),
                  pl.BlockSpec((1, D, tf), lambda i, f: (layer, 0, up(f) + nf)),
                  pl.BlockSpec((2 * nf, FFN_CONV_K, tf), lambda i, f: (0, 0, 0)),
                  pl.BlockSpec((2 * nf, 1, tf), lambda i, f: (0, 0, 0)),
                  pl.BlockSpec((1, tf, D), lambda i, f: (layer, down(f), 0)),
                  row((1, D))],
        out_specs=pl.BlockSpec((tm, D), lambda i, f: (i, 0)),
        out_shape=jax.ShapeDtypeStruct((S, D), F32),
        scratch_shapes=[pltpu.VMEM((HALO + tm, D), BF16), pltpu.VMEM((2, tm, tf), BF16)],
        compiler_params=_params(("parallel", "arbitrary")),
        name="conv_ffn",
    )(x, x, g, w_up, w_up, dw_tiles, b_tiles, w_down, final_g)


def kernel(x, norm1_g, w_in, conv_dw, conv_dw_b, conv_ln_g, conv_ln_b, w_conv_out, rel_bias,
           w_attn_out, w_out, norm2_g, w_up, ffn_dw, ffn_dw_b, w_down, final_g):
    B, S, D = x.shape
    assert D == D_MODEL and S % INPROJ_ROWS == 0 and INPROJ_ROWS % GROUP_KEYS == 0
    depth = w_in.shape[0]
    bias = _bias_tiles(rel_bias)
    row = lambda v: v.reshape(1, -1)
    w_in_bf = w_in.astype(BF16)
    w_conv_bf, w_attn_bf, w_out_bf = (w.astype(BF16) for w in (w_conv_out, w_attn_out, w_out))
    w_up_bf, w_down_bf = w_up.astype(BF16), w_down.astype(BF16)
    outs = []
    for b in range(B):
        xb = x.reshape(S, D) if B == 1 else x[b]
        for l in range(depth):
            glu, k, km, q_t, v_t, gates = _inproj_fused(xb, row(norm1_g[l]), w_in_bf, l)
            cs = _conv_branch(glu, conv_dw[l], row(conv_dw_b[l]), row(conv_ln_g[l]), row(conv_ln_b[l]))
            attn = _attention(k, km, q_t, v_t, bias, rel_bias)
            xb = _mix(cs, attn, gates, xb, w_conv_bf, w_attn_bf, w_out_bf, l)
            xb = _ffn(xb, row(norm2_g[l]), w_up_bf, ffn_dw[l], row(ffn_dw_b[l]),
                      w_down_bf, l, row(final_g), final_norm=(l == depth - 1))
        outs.append(xb)
    return outs[0].reshape(1, S, D) if B == 1 else jnp.stack(outs, axis=0)
```

```python
import functools
import math

import jax
import jax.numpy as jnp
from jax import lax
from jax.experimental import pallas as pl
from jax.experimental.pallas import tpu as pltpu

F32 = jnp.float32
BF16 = jnp.bfloat16

D_MODEL = 2048
D_CONV = 1024
CONV_K = 31
N_HEADS = 8
HEAD_DIM = 128
D_ATTN = N_HEADS * HEAD_DIM
MOBA_BLOCK = 256
MOBA_TOPK = 3
N_BUCKETS = 32
REL_MAX_DIST = 2048
D_FF = 5632
FFN_CONV_K = 3
EPS = 1e-6
NEG = -1e30
LOG2E = math.log2(math.e)

N_BIAS_TILES = 8
BIAS_ROWS = 16
KV_GROUP = 2
GROUP_KEYS = KV_GROUP * MOBA_BLOCK
INPROJ_ROWS = 1024
SUM_ROWS = 16
V_ROWS = HEAD_DIM + SUM_ROWS
ATTN_HEADS = 2

V7X_VMEM_BYTES = 64 * 1024 * 1024
VMEM_LIMIT = V7X_VMEM_BYTES - 8 * 1024 * 1024

SUBLANES = 8
NORM_ROWS = 16
NORM_UNROLL = 8
HALO = 8
CONV_HALO = 32

W_TILE = 512

TN_DIMS = (((0,), (1,)), ((), ()))


def _params(sem):
    return pltpu.CompilerParams(dimension_semantics=sem, vmem_limit_bytes=VMEM_LIMIT)


def _rmsnorm_rows(x_ref, g_ref, dst_ref, n_rows, dst_off=0, out_dtype=BF16):
    def body(c, carry):
        r = pl.multiple_of(c * NORM_ROWS, NORM_ROWS)
        xv = x_ref[pl.ds(r, NORM_ROWS), :]
        ms = jnp.mean(xv * xv, axis=-1, keepdims=True)
        y = xv * lax.rsqrt(ms + EPS) * g_ref[...]
        dst_ref[pl.ds(dst_off + r, NORM_ROWS), :] = y.astype(out_dtype)
        return carry
    lax.fori_loop(0, n_rows // NORM_ROWS, body, 0, unroll=NORM_UNROLL)


def _inproj_kernel(x_ref, g_ref, w_ref, w2_ref, glu_ref, k_ref, km_ref, q_ref, v_ref, gate_ref,
                   h_ref, *, seg):
    j = pl.program_id(1)
    n_glu, n_k, n_q, n_v, _ = seg
    k0 = n_glu
    q0 = k0 + n_k
    v0 = q0 + n_q
    g0 = v0 + n_v

    @pl.when(j == 0)
    def _():
        _rmsnorm_rows(x_ref, g_ref, h_ref, x_ref.shape[0])

    @pl.when(j < k0)
    def _():
        h = h_ref[...]
        a = jnp.dot(h, w_ref[0], preferred_element_type=F32)
        b = jnp.dot(h, w2_ref[0], preferred_element_type=F32)
        glu_ref[...] = a * jax.nn.sigmoid(b)

    @pl.when(jnp.logical_and(j >= k0, j < q0))
    def _():
        acc = jnp.dot(h_ref[...], w_ref[0], preferred_element_type=F32)
        k_ref[...] = acc.astype(k_ref.dtype)
        for b in range(acc.shape[0] // MOBA_BLOCK):
            blk = acc[b * MOBA_BLOCK:(b + 1) * MOBA_BLOCK]
            km_ref[0, b:b + 1, :] = jnp.mean(blk, axis=0, keepdims=True)

    @pl.when(jnp.logical_and(j >= q0, j < v0))
    def _():
        acc = lax.dot_general(w_ref[0], h_ref[...], TN_DIMS, preferred_element_type=F32)
        q_ref[0] = acc.astype(q_ref.dtype)

    @pl.when(jnp.logical_and(j >= v0, j < g0))
    def _():
        acc = lax.dot_general(w_ref[0], h_ref[...], TN_DIMS, preferred_element_type=F32)
        for grp in range(v_ref.shape[0]):
            pos = slice(grp * GROUP_KEYS, (grp + 1) * GROUP_KEYS)
            for hh in range(acc.shape[0] // HEAD_DIM):
                r = hh * V_ROWS
                v_ref[grp, r:r + HEAD_DIM, :] = acc[hh * HEAD_DIM:(hh + 1) * HEAD_DIM, pos].astype(v_ref.dtype)
                v_ref[grp, r + HEAD_DIM:r + V_ROWS, :] = jnp.ones((SUM_ROWS, GROUP_KEYS), v_ref.dtype)

    @pl.when(j >= g0)
    def _():
        acc = jnp.dot(h_ref[...], w_ref[0], preferred_element_type=F32)
        gate_ref[...] = jax.nn.sigmoid(acc).astype(gate_ref.dtype)


def _inproj_fused(x, g, w_in, layer, tn=W_TILE):
    S, D = x.shape
    tm = INPROJ_ROWS
    nm = S // tm
    groups = tm // GROUP_KEYS
    seg = (D_CONV // tn, D_ATTN // tn, D_ATTN // tn, D_ATTN // tn, 2 * D_MODEL // tn)
    n_glu, n_k, n_q, n_v, n_g = seg
    k0, q0 = n_glu, n_glu + n_k
    v0 = q0 + n_q
    g0 = v0 + n_v
    col_q = 2 * D_CONV // tn
    col_k = col_q + n_q

    def w_col(j):
        return jnp.where(j < k0, j, jnp.where(j < q0, col_k + (j - k0),
                         jnp.where(j < v0, col_q + (j - q0), col_k + n_k + (j - v0))))

    seg_tile = lambda j, start, n: jnp.clip(j - start, 0, n - 1)
    v_tile_rows = tn // HEAD_DIM * V_ROWS
    glu, k, km, q_t, v_t, gates = pl.pallas_call(
        functools.partial(_inproj_kernel, seg=seg),
        grid=(nm, sum(seg)),
        in_specs=[pl.BlockSpec((tm, D), lambda i, j: (i, 0)),
                  pl.BlockSpec((1, D), lambda i, j: (0, 0)),
                  pl.BlockSpec((1, D, tn), lambda i, j: (layer, 0, w_col(j))),
                  pl.BlockSpec((1, D, tn), lambda i, j: (layer, 0, D_CONV // tn + seg_tile(j, 0, n_glu)))],
        out_specs=[pl.BlockSpec((tm, tn), lambda i, j: (i, seg_tile(j, 0, n_glu))),
                   pl.BlockSpec((tm, tn), lambda i, j: (i, seg_tile(j, k0, n_k))),
                   pl.BlockSpec((1, tm // MOBA_BLOCK, tn), lambda i, j: (i, 0, seg_tile(j, k0, n_k))),
                   pl.BlockSpec((1, tn, tm), lambda i, j: (i, seg_tile(j, q0, n_q), 0)),
                   pl.BlockSpec((groups, v_tile_rows, GROUP_KEYS), lambda i, j: (i, seg_tile(j, v0, n_v), 0)),
                   pl.BlockSpec((tm, tn), lambda i, j: (i, seg_tile(j, g0, n_g)))],
        out_shape=[jax.ShapeDtypeStruct((S, D_CONV), F32),
                   jax.ShapeDtypeStruct((S, D_ATTN), BF16),
                   jax.ShapeDtypeStruct((nm, tm // MOBA_BLOCK, D_ATTN), F32),
                   jax.ShapeDtypeStruct((nm, D_ATTN, tm), BF16),
                   jax.ShapeDtypeStruct((S // GROUP_KEYS, N_HEADS * V_ROWS, GROUP_KEYS), BF16),
                   jax.ShapeDtypeStruct((S, 2 * D_MODEL), BF16)],
        scratch_shapes=[pltpu.VMEM((tm, D), BF16)],
        compiler_params=_params(("parallel", "arbitrary")),
        name="inproj",
    )(x, g, w_in, w_in)
    return glu, k, km.reshape(S // MOBA_BLOCK, D_ATTN), q_t, v_t, gates


def _conv_kernel(halo_ref, x_ref, w_ref, b_ref, lg_ref, lb_ref, o_ref, win_ref, y_ref,
                 *, rows, lanes):
    tm, C = x_ref.shape
    i = pl.program_id(0)
    halo = halo_ref[...]
    win_ref[0:CONV_HALO, :] = jnp.where(i > 0, halo, jnp.zeros_like(halo))
    win_ref[CONV_HALO:, :] = x_ref[...]
    first = CONV_HALO - (CONV_K - 1)

    for c0 in range(0, C, lanes):
        def body(rc, carry):
            r = pl.multiple_of(rc * rows, rows)
            acc = jnp.zeros((rows, lanes), F32) + b_ref[:, c0:c0 + lanes]
            win = win_ref[pl.ds(r, rows + CONV_HALO), c0:c0 + lanes]
            for sub in range(SUBLANES):
                shifted = win if sub == 0 else pltpu.roll(win, win.shape[0] - sub, 0)
                usable = win.shape[0] - (SUBLANES if sub else 0)
                for base in range(0, usable - rows + 1, SUBLANES):
                    k = base + sub - first
                    if 0 <= k < CONV_K:
                        acc = acc + shifted[base:base + rows] * w_ref[k:k + 1, c0:c0 + lanes]
            y_ref[pl.ds(r, rows), c0:c0 + lanes] = acc
            return carry
        lax.fori_loop(0, tm // rows, body, 0)

    def ln_body(rc, carry):
        r = pl.multiple_of(rc * NORM_ROWS, NORM_ROWS)
        v = y_ref[pl.ds(r, NORM_ROWS), :]
        mu = jnp.mean(v, axis=-1, keepdims=True)
        var = jnp.mean(jnp.square(v - mu), axis=-1, keepdims=True)
        z = (v - mu) * lax.rsqrt(var + EPS) * lg_ref[...] + lb_ref[...]
        o_ref[pl.ds(r, NORM_ROWS), :] = jax.nn.silu(z).astype(o_ref.dtype)
        return carry
    lax.fori_loop(0, tm // NORM_ROWS, ln_body, 0, unroll=NORM_UNROLL)


def _conv_branch(glu, w, b, lg, lb, tm=256):
    S, C = glu.shape
    per = tm // CONV_HALO
    return pl.pallas_call(
        functools.partial(_conv_kernel, rows=128, lanes=128),
        grid=(S // tm,),
        in_specs=[pl.BlockSpec((CONV_HALO, C), lambda i: (jnp.maximum(i * per - 1, 0), 0)),
                  pl.BlockSpec((tm, C), lambda i: (i, 0)),
                  pl.BlockSpec((CONV_K, C), lambda i: (0, 0)),
                  pl.BlockSpec((1, C), lambda i: (0, 0)),
                  pl.BlockSpec((1, C), lambda i: (0, 0)),
                  pl.BlockSpec((1, C), lambda i: (0, 0))],
        out_specs=pl.BlockSpec((tm, C), lambda i: (i, 0)),
        out_shape=jax.ShapeDtypeStruct((S, C), BF16),
        scratch_shapes=[pltpu.VMEM((tm + CONV_HALO, C), F32), pltpu.VMEM((tm, C), F32)],
        compiler_params=_params(("parallel",)),
        name="conv_branch",
    )(glu, glu, w, b, lg, lb)


def _bias_kernel(rb_ref, o_ref):
    delta = pl.program_id(0)
    shape = (BIAS_ROWS, MOBA_BLOCK)

    def body(c, carry):
        c0 = pl.multiple_of(c * BIAS_ROWS, BIAS_ROWS)
        key = c0 + lax.broadcasted_iota(jnp.int32, shape, 0)
        d = delta * MOBA_BLOCK + lax.broadcasted_iota(jnp.int32, shape, 1) - key
        n = jnp.maximum(d, 0)
        max_exact = N_BUCKETS // 2
        nf = jnp.maximum(n, 1).astype(F32)
        large = max_exact + (jnp.log(nf / max_exact) / math.log(REL_MAX_DIST / max_exact)
                             * (N_BUCKETS - max_exact)).astype(jnp.int32)
        large = jnp.minimum(large, N_BUCKETS - 1)
        bucket = jnp.where(n < max_exact, n, large)
        vals = [jnp.zeros(shape, F32) for _ in range(N_HEADS)]
        for b in range(N_BUCKETS):
            hit = bucket == b
            vals = [jnp.where(hit, rb_ref[b, h], vals[h]) for h in range(N_HEADS)]
        for h in range(N_HEADS):
            val = jnp.where(delta == N_BIAS_TILES - 1, 0.0, vals[h] * LOG2E)
            o_ref[h, 0, pl.ds(c0, BIAS_ROWS), :] = jnp.where(d >= 0, val, NEG)
        return carry
    lax.fori_loop(0, MOBA_BLOCK // BIAS_ROWS, body, 0)


def _bias_tiles(rel_bias):
    return pl.pallas_call(
        _bias_kernel,
        grid=(N_BIAS_TILES,),
        in_specs=[pl.BlockSpec(memory_space=pltpu.SMEM)],
        out_specs=pl.BlockSpec((N_HEADS, 1, MOBA_BLOCK, MOBA_BLOCK), lambda d: (0, d, 0, 0)),
        out_shape=jax.ShapeDtypeStruct((N_HEADS, N_BIAS_TILES, MOBA_BLOCK, MOBA_BLOCK), F32),
        compiler_params=_params(("parallel",)),
        name="t5_bias_tiles",
    )(rel_bias)


def _attn_kernel(rb_ref, qt_ref, k_ref, vt_ref, km_ref, ind_ref, bias_ref, o_ref,
                 acc_ref, sa_ref, sb_ref, pa_ref, pb_ref):
    i = pl.program_id(1)
    nb = km_ref.shape[0]
    tq = qt_ref.shape[2]
    scale = HEAD_DIM ** -0.5
    heads = range(ATTN_HEADS)
    lanes = [slice(hh * HEAD_DIM, (hh + 1) * HEAD_DIM) for hh in heads]

    def widened_query(hh):
        qf = qt_ref[0, lanes[hh], :].astype(F32)
        gate = jnp.dot(km_ref[:, lanes[hh]], qf, precision=lax.Precision.HIGHEST, preferred_element_type=F32)
        row = lax.broadcasted_iota(jnp.int32, (nb, tq), 0)
        valid = row < i
        g = jnp.where(valid, gate, NEG)
        sel = jnp.zeros((nb, tq), jnp.bool_)
        for _ in range(MOBA_TOPK):
            top = jnp.max(g, axis=0, keepdims=True)
            idx = jnp.min(jnp.where(g == top, row, nb), axis=0, keepdims=True)
            hit = row == idx
            sel = jnp.logical_or(sel, hit)
            g = jnp.where(hit, -jnp.inf, g)
        picked = jnp.logical_and(sel, valid)
        far = jnp.logical_and(picked, row <= i - (N_BIAS_TILES - 1))
        far_bias = rb_ref[N_BUCKETS - 1, pl.program_id(0) * ATTN_HEADS + hh] * LOG2E
        pen = jnp.where(row == i, 0.0, jnp.where(picked, jnp.where(far, far_bias, 0.0), NEG))
        pen_hi = pen.astype(BF16)
        pen_lo = jnp.where(far, pen - pen_hi.astype(F32), 0.0).astype(BF16)
        return jnp.concatenate(
            [(qf * (scale * LOG2E)).astype(BF16), pen_hi, pen_lo, jnp.zeros((HEAD_DIM - 2 * nb, tq), BF16)],
            axis=0)

    q_aug = [widened_query(hh) for hh in heads]
    n_groups = i // KV_GROUP + 1

    def scores(grp, dst_ref):
        r0 = pl.multiple_of(grp * GROUP_KEYS, GROUP_KEYS)
        ind = ind_ref[pl.ds(r0, GROUP_KEYS), :]
        for hh in heads:
            k_aug = jnp.concatenate([k_ref[pl.ds(r0, GROUP_KEYS), lanes[hh]], ind], axis=1)
            dst_ref[hh] = jnp.dot(k_aug, q_aug[hh], preferred_element_type=F32)

    def softmax(grp, src_ref, dst_ref, m_prev):
        m_out, alpha_out = [], []
        for hh in heads:
            parts = []
            for u in range(KV_GROUP):
                delta = jnp.clip(i - (grp * KV_GROUP + u), 0, N_BIAS_TILES - 1)
                parts.append(src_ref[hh, u * MOBA_BLOCK:(u + 1) * MOBA_BLOCK, :] + bias_ref[hh, delta])
            s = jnp.concatenate(parts, axis=0)
            m_new = jnp.maximum(m_prev[hh], jnp.max(s, axis=0, keepdims=True))
            dst_ref[hh] = jnp.exp2(s - m_new).astype(BF16)
            m_out.append(m_new)
            alpha_out.append(jnp.exp2(m_prev[hh] - m_new))
        return tuple(m_out), tuple(alpha_out)

    def weighted_values(grp, live, src_ref, alpha):
        for hh in heads:
            pv = jnp.dot(vt_ref[grp, hh * V_ROWS:(hh + 1) * V_ROWS, :], src_ref[hh], preferred_element_type=F32)
            acc_ref[hh] = alpha[hh] * acc_ref[hh] + jnp.where(live, pv, 0.0)

    acc_ref[...] = jnp.zeros_like(acc_ref)
    pb_ref[...] = jnp.zeros_like(pb_ref)
    scores(n_groups - 1, sa_ref)

    def body(u, carry):
        m_prev, alpha_b, grp_b_prev = carry
        grp_a = n_groups - 1 - 2 * u
        grp_b = grp_a - 1
        weighted_values(jnp.maximum(grp_b_prev, 0), grp_b_prev >= 0, pb_ref, alpha_b)
        m_a, alpha_a = softmax(grp_a, sa_ref, pa_ref, m_prev)
        scores(jnp.maximum(grp_b, 0), sb_ref)
        weighted_values(grp_a, True, pa_ref, alpha_a)
        m_b, alpha_b = softmax(jnp.maximum(grp_b, 0), sb_ref, pb_ref, m_a)
        scores(jnp.maximum(grp_b - 1, 0), sa_ref)
        return m_b, alpha_b, grp_b

    m0 = tuple(jnp.full((1, tq), NEG, F32) for _ in heads)
    a0 = tuple(jnp.ones((1, tq), F32) for _ in heads)
    _, alpha_b, grp_b = lax.fori_loop(0, (n_groups + 1) // 2, body, (m0, a0, jnp.int32(-1)))
    weighted_values(jnp.maximum(grp_b, 0), grp_b >= 0, pb_ref, alpha_b)
    for hh in heads:
        acc = acc_ref[hh]
        o_ref[:, lanes[hh]] = (acc[:HEAD_DIM] / acc[HEAD_DIM:HEAD_DIM + 1]).T.astype(o_ref.dtype)


def _attention(k, km, q_t, v_t, bias, rel_bias):
    S = k.shape[0]
    nb = S // MOBA_BLOCK
    tq = MOBA_BLOCK
    per = INPROJ_ROWS // tq
    width = ATTN_HEADS * HEAD_DIM
    key_blk = jnp.arange(S, dtype=jnp.int32)[:, None] // MOBA_BLOCK
    lane = jnp.arange(HEAD_DIM, dtype=jnp.int32)[None, :]
    ind = jnp.logical_and(lane < 2 * nb, lane % nb == key_blk).astype(BF16)
    return pl.pallas_call(
        _attn_kernel,
        grid=(N_HEADS // ATTN_HEADS, S // tq),
        in_specs=[pl.BlockSpec(memory_space=pltpu.SMEM),
                  pl.BlockSpec((1, width, tq), lambda h, i: (i // per, h, i % per)),
                  pl.BlockSpec((S, width), lambda h, i: (0, h)),
                  pl.BlockSpec((S // GROUP_KEYS, ATTN_HEADS * V_ROWS, GROUP_KEYS), lambda h, i: (0, h, 0)),
                  pl.BlockSpec((nb, width), lambda h, i: (0, h)),
                  pl.BlockSpec((S, HEAD_DIM), lambda h, i: (0, 0)),
                  pl.BlockSpec((ATTN_HEADS, N_BIAS_TILES, MOBA_BLOCK, MOBA_BLOCK), lambda h, i: (h, 0, 0, 0))],
        out_specs=pl.BlockSpec((tq, width), lambda h, i: (i, h)),
        out_shape=jax.ShapeDtypeStruct((S, D_ATTN), BF16),
        scratch_shapes=[pltpu.VMEM((ATTN_HEADS, HEAD_DIM + SUM_ROWS, tq), F32),
                        pltpu.VMEM((ATTN_HEADS, GROUP_KEYS, tq), F32),
                        pltpu.VMEM((ATTN_HEADS, GROUP_KEYS, tq), F32),
                        pltpu.VMEM((ATTN_HEADS, GROUP_KEYS, tq), BF16),
                        pltpu.VMEM((ATTN_HEADS, GROUP_KEYS, tq), BF16)],
        compiler_params=_params(("parallel", "parallel")),
        name="moba_attention",
    )(rel_bias, q_t, k, v_t, km, ind, bias)


def _mix_kernel(cs_ref, at_ref, gate_ref, x_ref, wc_ref, wa_ref, wo_ref, o_ref):
    D = x_ref.shape[1]
    yc = jnp.dot(cs_ref[...], wc_ref[0], preferred_element_type=F32)
    ya = jnp.dot(at_ref[...], wa_ref[0], preferred_element_type=F32)
    merged = gate_ref[:, :D].astype(F32) * yc + gate_ref[:, D:].astype(F32) * ya
    o_ref[...] = x_ref[...] + jnp.dot(merged.astype(BF16), wo_ref[0], preferred_element_type=F32)


def _mix(cs, attn, gates, x, wc, wa, wo, layer, tm=512):
    S, D = x.shape
    full = lambda shape: pl.BlockSpec(shape, lambda i: (layer, 0, 0), pipeline_mode=pl.Buffered(1))
    return pl.pallas_call(
        _mix_kernel,
        grid=(S // tm,),
        in_specs=[pl.BlockSpec((tm, D_CONV), lambda i: (i, 0)),
                  pl.BlockSpec((tm, D_ATTN), lambda i: (i, 0)),
                  pl.BlockSpec((tm, 2 * D), lambda i: (i, 0)),
                  pl.BlockSpec((tm, D), lambda i: (i, 0)),
                  full((1, D_CONV, D)), full((1, D_ATTN, D)), full((1, D, D))],
        out_specs=pl.BlockSpec((tm, D), lambda i: (i, 0)),
        out_shape=jax.ShapeDtypeStruct((S, D), F32),
        compiler_params=_params(("parallel",)),
        name="mix_out_proj",
    )(cs, attn, gates, x, wc, wa, wo)


def _ffn_kernel(xh_ref, x_ref, g_ref, wa_ref, wb_ref, dw_ref, b_ref, wd_ref,
                fg_ref, o_ref, h_ref, gate_ref, *, final_norm, n_tiles, up_chunk):
    i = pl.program_id(0)
    f = pl.program_id(1)
    tm = x_ref.shape[0]
    tf = wa_ref.shape[2]
    slot = f % 2

    def up_stage():
        h = h_ref[...]
        tile = jnp.minimum(f, n_tiles - 1)
        for c0 in range(0, tf, up_chunk):
            cols = slice(c0, c0 + up_chunk)

            def up_conv(w_ref, t):
                u = jnp.dot(h, w_ref[0, :, cols], preferred_element_type=F32)
                y = (u * dw_ref[t, 2:3, cols] + pltpu.roll(u, 1, 0) * dw_ref[t, 1:2, cols]
                     + pltpu.roll(u, 2, 0) * dw_ref[t, 0:1, cols] + b_ref[t, :, cols])
                return y[HALO:]

            ua = up_conv(wa_ref, tile)
            ub = up_conv(wb_ref, tile + n_tiles)
            gate_ref[slot, :, cols] = (jax.nn.silu(ub) * ua).astype(BF16)

    def down_product():
        return jnp.dot(gate_ref[1 - slot], wd_ref[0], preferred_element_type=F32)

    @pl.when(f == 0)
    def _():
        xh = xh_ref[...]
        ms = jnp.mean(xh * xh, axis=-1, keepdims=True)
        hh = xh * lax.rsqrt(ms + EPS) * g_ref[...]
        h_ref[0:HALO, :] = jnp.where(i > 0, hh, jnp.zeros_like(hh)).astype(BF16)
        _rmsnorm_rows(x_ref, g_ref, h_ref, tm, dst_off=HALO)
        o_ref[...] = jnp.zeros_like(o_ref)
        up_stage()

    @pl.when(jnp.logical_and(f > 0, f < n_tiles))
    def _():
        o_ref[...] += down_product()
        up_stage()

    @pl.when(f == n_tiles)
    def _():
        y = x_ref[...] + o_ref[...] + down_product()
        if final_norm:
            ms = jnp.mean(y * y, axis=-1, keepdims=True)
            y = y * lax.rsqrt(ms + EPS) * fg_ref[...]
        o_ref[...] = y


def _ffn(x, g, w_up, dw, dwb, w_down, layer, final_g, final_norm, tm=1024, tf=W_TILE):
    S, D = x.shape
    nf = D_FF // tf
    per = tm // HALO
    dw_tiles = dw.reshape(FFN_CONV_K, 2 * nf, tf).transpose(1, 0, 2)
    b_tiles = dwb.reshape(2 * nf, 1, tf)
    row = lambda shape: pl.BlockSpec(shape, lambda i, f: (0, 0))
    up = lambda f: jnp.minimum(f, nf - 1)
    down = lambda f: jnp.maximum(f - 1, 0)
    return pl.pallas_call(
        functools.partial(_ffn_kernel, final_norm=final_norm, n_tiles=nf, up_chunk=tf // 2),
        grid=(S // tm, nf + 1),
        in_specs=[pl.BlockSpec((HALO, D), lambda i, f: (jnp.maximum(i * per - 1, 0), 0)),
                  pl.BlockSpec((tm, D), lambda i, f: (i, 0), pipeline_mode=pl.Buffered(1)),
                  row((1, D)),
                  pl.BlockSpec((1, D, tf), lambda i, f: (layer, 0, up(f))),
                  pl.BlockSpec((1, D, tf), lambda i, f: (layer, 0, up(f) + nf)),
                  pl.BlockSpec((2 * nf, FFN_CONV_K, tf), lambda i, f: (0, 0, 0)),
                  pl.BlockSpec((2 * nf, 1, tf), lambda i, f: (0, 0, 0)),
                  pl.BlockSpec((1, tf, D), lambda i, f: (layer, down(f), 0)),
                  row((1, D))],
        out_specs=pl.BlockSpec((tm, D), lambda i, f: (i, 0)),
        out_shape=jax.ShapeDtypeStruct((S, D), F32),
        scratch_shapes=[pltpu.VMEM((HALO + tm, D), BF16), pltpu.VMEM((2, tm, tf), BF16)],
        compiler_params=_params(("parallel", "arbitrary")),
        name="conv_ffn",
    )(x, x, g, w_up, w_up, dw_tiles, b_tiles, w_down, final_g)


def kernel(x, norm1_g, w_in, conv_dw, conv_dw_b, conv_ln_g, conv_ln_b, w_conv_out, rel_bias,
           w_attn_out, w_out, norm2_g, w_up, ffn_dw, ffn_dw_b, w_down, final_g):
    B, S, D = x.shape
    assert D == D_MODEL and S % INPROJ_ROWS == 0 and INPROJ_ROWS % GROUP_KEYS == 0
    depth = w_in.shape[0]
    bias = _bias_tiles(rel_bias)
    row = lambda v: v.reshape(1, -1)
    w_in_bf = w_in.astype(BF16)
    w_conv_bf, w_attn_bf, w_out_bf = (w.astype(BF16) for w in (w_conv_out, w_attn_out, w_out))
    w_up_bf, w_down_bf = w_up.astype(BF16), w_down.astype(BF16)
    outs = []
    for b in range(B):
        xb = x.reshape(S, D) if B == 1 else x[b]
        for l in range(depth):
            glu, k, km, q_t, v_t, gates = _inproj_fused(xb, row(norm1_g[l]), w_in_bf, l)
            cs = _conv_branch(glu, conv_dw[l], row(conv_dw_b[l]), row(conv_ln_g[l]), row(conv_ln_b[l]))
            attn = _attention(k, km, q_t, v_t, bias, rel_bias)
            xb = _mix(cs, attn, gates, xb, w_conv_bf, w_attn_bf, w_out_bf, l)
            xb = _ffn(xb, row(norm2_g[l]), w_up_bf, ffn_dw[l], row(ffn_dw_b[l]),
                      w_down_bf, l, row(final_g), final_norm=(l == depth - 1))
        outs.append(xb)
    return outs[0].reshape(1, S, D) if B == 1 else jnp.stack(outs, axis=0)
```

```python
import functools
import math

import jax
import jax.numpy as jnp
from jax import lax
from jax.experimental import pallas as pl
from jax.experimental.pallas import tpu as pltpu

F32 = jnp.float32
BF16 = jnp.bfloat16

D_MODEL = 2048
D_CONV = 1024
CONV_K = 31
N_HEADS = 8
HEAD_DIM = 128
D_ATTN = N_HEADS * HEAD_DIM
MOBA_BLOCK = 256
MOBA_TOPK = 3
N_BUCKETS = 32
REL_MAX_DIST = 2048
D_FF = 5632
FFN_CONV_K = 3
EPS = 1e-6
NEG = -1e30
LOG2E = math.log2(math.e)

N_BIAS_TILES = 8
BIAS_ROWS = 16
KV_GROUP = 2
GROUP_KEYS = KV_GROUP * MOBA_BLOCK
INPROJ_ROWS = 1024
SUM_ROWS = 16
V_ROWS = HEAD_DIM + SUM_ROWS
ATTN_HEADS = 2

V7X_VMEM_BYTES = 64 * 1024 * 1024
VMEM_LIMIT = V7X_VMEM_BYTES - 8 * 1024 * 1024

SUBLANES = 8
NORM_ROWS = 16
NORM_UNROLL = 8
HALO = 8
CONV_HALO = 32

W_TILE = 512

TN_DIMS = (((0,), (1,)), ((), ()))


def _params(sem):
    return pltpu.CompilerParams(dimension_semantics=sem, vmem_limit_bytes=VMEM_LIMIT)


def _rmsnorm_rows(x_ref, g_ref, dst_ref, n_rows, dst_off=0, out_dtype=BF16):
    def body(c, carry):
        r = pl.multiple_of(c * NORM_ROWS, NORM_ROWS)
        xv = x_ref[pl.ds(r, NORM_ROWS), :]
        ms = jnp.mean(xv * xv, axis=-1, keepdims=True)
        y = xv * lax.rsqrt(ms + EPS) * g_ref[...]
        dst_ref[pl.ds(dst_off + r, NORM_ROWS), :] = y.astype(out_dtype)
        return carry
    lax.fori_loop(0, n_rows // NORM_ROWS, body, 0, unroll=NORM_UNROLL)


def _inproj_kernel(x_ref, g_ref, w_ref, w2_ref, glu_ref, k_ref, km_ref, q_ref, v_ref, gate_ref,
                   h_ref, *, seg):
    j = pl.program_id(1)
    n_glu, n_k, n_q, n_v, _ = seg
    k0 = n_glu
    q0 = k0 + n_k
    v0 = q0 + n_q
    g0 = v0 + n_v

    @pl.when(j == 0)
    def _():
        _rmsnorm_rows(x_ref, g_ref, h_ref, x_ref.shape[0])

    @pl.when(j < k0)
    def _():
        h = h_ref[...]
        a = jnp.dot(h, w_ref[0], preferred_element_type=F32)
        b = jnp.dot(h, w2_ref[0], preferred_element_type=F32)
        glu_ref[...] = a * jax.nn.sigmoid(b)

    @pl.when(jnp.logical_and(j >= k0, j < q0))
    def _():
        acc = jnp.dot(h_ref[...], w_ref[0], preferred_element_type=F32)
        k_ref[...] = acc.astype(k_ref.dtype)
        for b in range(acc.shape[0] // MOBA_BLOCK):
            blk = acc[b * MOBA_BLOCK:(b + 1) * MOBA_BLOCK]
            km_ref[0, b:b + 1, :] = jnp.mean(blk, axis=0, keepdims=True)

    @pl.when(jnp.logical_and(j >= q0, j < v0))
    def _():
        acc = lax.dot_general(w_ref[0], h_ref[...], TN_DIMS, preferred_element_type=F32)
        q_ref[0] = acc.astype(q_ref.dtype)

    @pl.when(jnp.logical_and(j >= v0, j < g0))
    def _():
        acc = lax.dot_general(w_ref[0], h_ref[...], TN_DIMS, preferred_element_type=F32)
        for grp in range(v_ref.shape[0]):
            pos = slice(grp * GROUP_KEYS, (grp + 1) * GROUP_KEYS)
            for hh in range(acc.shape[0] // HEAD_DIM):
                r = hh * V_ROWS
                v_ref[grp, r:r + HEAD_DIM, :] = acc[hh * HEAD_DIM:(hh + 1) * HEAD_DIM, pos].astype(v_ref.dtype)
                v_ref[grp, r + HEAD_DIM:r + V_ROWS, :] = jnp.ones((SUM_ROWS, GROUP_KEYS), v_ref.dtype)

    @pl.when(j >= g0)
    def _():
        acc = jnp.dot(h_ref[...], w_ref[0], preferred_element_type=F32)
        gate_ref[...] = jax.nn.sigmoid(acc).astype(gate_ref.dtype)


def _inproj_fused(x, g, w_in, layer, tn=W_TILE):
    S, D = x.shape
    tm = INPROJ_ROWS
    nm = S // tm
    groups = tm // GROUP_KEYS
    seg = (D_CONV // tn, D_ATTN // tn, D_ATTN // tn, D_ATTN // tn, 2 * D_MODEL // tn)
    n_glu, n_k, n_q, n_v, n_g = seg
    k0, q0 = n_glu, n_glu + n_k
    v0 = q0 + n_q
    g0 = v0 + n_v
    col_q = 2 * D_CONV // tn
    col_k = col_q + n_q

    def w_col(j):
        return jnp.where(j < k0, j, jnp.where(j < q0, col_k + (j - k0),
                         jnp.where(j < v0, col_q + (j - q0), col_k + n_k + (j - v0))))

    seg_tile = lambda j, start, n: jnp.clip(j - start, 0, n - 1)
    v_tile_rows = tn // HEAD_DIM * V_ROWS
    glu, k, km, q_t, v_t, gates = pl.pallas_call(
        functools.partial(_inproj_kernel, seg=seg),
        grid=(nm, sum(seg)),
        in_specs=[pl.BlockSpec((tm, D), lambda i, j: (i, 0)),
                  pl.BlockSpec((1, D), lambda i, j: (0, 0)),
                  pl.BlockSpec((1, D, tn), lambda i, j: (layer, 0, w_col(j))),
                  pl.BlockSpec((1, D, tn), lambda i, j: (layer, 0, D_CONV // tn + seg_tile(j, 0, n_glu)))],
        out_specs=[pl.BlockSpec((tm, tn), lambda i, j: (i, seg_tile(j, 0, n_glu))),
                   pl.BlockSpec((tm, tn), lambda i, j: (i, seg_tile(j, k0, n_k))),
                   pl.BlockSpec((1, tm // MOBA_BLOCK, tn), lambda i, j: (i, 0, seg_tile(j, k0, n_k))),
                   pl.BlockSpec((1, tn, tm), lambda i, j: (i, seg_tile(j, q0, n_q), 0)),
                   pl.BlockSpec((groups, v_tile_rows, GROUP_KEYS), lambda i, j: (i, seg_tile(j, v0, n_v), 0)),
                   pl.BlockSpec((tm, tn), lambda i, j: (i, seg_tile(j, g0, n_g)))],
        out_shape=[jax.ShapeDtypeStruct((S, D_CONV), F32),
                   jax.ShapeDtypeStruct((S, D_ATTN), BF16),
                   jax.ShapeDtypeStruct((nm, tm // MOBA_BLOCK, D_ATTN), F32),
                   jax.ShapeDtypeStruct((nm, D_ATTN, tm), BF16),
                   jax.ShapeDtypeStruct((S // GROUP_KEYS, N_HEADS * V_ROWS, GROUP_KEYS), BF16),
                   jax.ShapeDtypeStruct((S, 2 * D_MODEL), BF16)],
        scratch_shapes=[pltpu.VMEM((tm, D), BF16)],
        compiler_params=_params(("parallel", "arbitrary")),
        name="inproj",
    )(x, g, w_in, w_in)
    return glu, k, km.reshape(S // MOBA_BLOCK, D_ATTN), q_t, v_t, gates


def _conv_kernel(halo_ref, x_ref, w_ref, b_ref, lg_ref, lb_ref, o_ref, win_ref, y_ref,
                 *, rows, lanes):
    tm, C = x_ref.shape
    i = pl.program_id(0)
    halo = halo_ref[...]
    win_ref[0:CONV_HALO, :] = jnp.where(i > 0, halo, jnp.zeros_like(halo))
    win_ref[CONV_HALO:, :] = x_ref[...]
    first = CONV_HALO - (CONV_K - 1)

    for c0 in range(0, C, lanes):
        def body(rc, carry):
            r = pl.multiple_of(rc * rows, rows)
            acc = jnp.zeros((rows, lanes), F32) + b_ref[:, c0:c0 + lanes]
            win = win_ref[pl.ds(r, rows + CONV_HALO), c0:c0 + lanes]
            for sub in range(SUBLANES):
                shifted = win if sub == 0 else pltpu.roll(win, win.shape[0] - sub, 0)
                usable = win.shape[0] - (SUBLANES if sub else 0)
                for base in range(0, usable - rows + 1, SUBLANES):
                    k = base + sub - first
                    if 0 <= k < CONV_K:
                        acc = acc + shifted[base:base + rows] * w_ref[k:k + 1, c0:c0 + lanes]
            y_ref[pl.ds(r, rows), c0:c0 + lanes] = acc
            return carry
        lax.fori_loop(0, tm // rows, body, 0)

    def ln_body(rc, carry):
        r = pl.multiple_of(rc * NORM_ROWS, NORM_ROWS)
        v = y_ref[pl.ds(r, NORM_ROWS), :]
        mu = jnp.mean(v, axis=-1, keepdims=True)
        var = jnp.mean(jnp.square(v - mu), axis=-1, keepdims=True)
        z = (v - mu) * lax.rsqrt(var + EPS) * lg_ref[...] + lb_ref[...]
        o_ref[pl.ds(r, NORM_ROWS), :] = jax.nn.silu(z).astype(o_ref.dtype)
        return carry
    lax.fori_loop(0, tm // NORM_ROWS, ln_body, 0, unroll=NORM_UNROLL)


def _conv_branch(glu, w, b, lg, lb, tm=256):
    S, C = glu.shape
    per = tm // CONV_HALO
    return pl.pallas_call(
        functools.partial(_conv_kernel, rows=128, lanes=128),
        grid=(S // tm,),
        in_specs=[pl.BlockSpec((CONV_HALO, C), lambda i: (jnp.maximum(i * per - 1, 0), 0)),
                  pl.BlockSpec((tm, C), lambda i: (i, 0)),
                  pl.BlockSpec((CONV_K, C), lambda i: (0, 0)),
                  pl.BlockSpec((1, C), lambda i: (0, 0)),
                  pl.BlockSpec((1, C), lambda i: (0, 0)),
                  pl.BlockSpec((1, C), lambda i: (0, 0))],
        out_specs=pl.BlockSpec((tm, C), lambda i: (i, 0)),
        out_shape=jax.ShapeDtypeStruct((S, C), BF16),
        scratch_shapes=[pltpu.VMEM((tm + CONV_HALO, C), F32), pltpu.VMEM((tm, C), F32)],
        compiler_params=_params(("parallel",)),
        name="conv_branch",
    )(glu, glu, w, b, lg, lb)


def _bias_kernel(rb_ref, o_ref):
    delta = pl.program_id(0)
    shape = (BIAS_ROWS, MOBA_BLOCK)

    def body(c, carry):
        c0 = pl.multiple_of(c * BIAS_ROWS, BIAS_ROWS)
        key = c0 + lax.broadcasted_iota(jnp.int32, shape, 0)
        d = delta * MOBA_BLOCK + lax.broadcasted_iota(jnp.int32, shape, 1) - key
        n = jnp.maximum(d, 0)
        max_exact = N_BUCKETS // 2
        nf = jnp.maximum(n, 1).astype(F32)
        large = max_exact + (jnp.log(nf / max_exact) / math.log(REL_MAX_DIST / max_exact)
                             * (N_BUCKETS - max_exact)).astype(jnp.int32)
        large = jnp.minimum(large, N_BUCKETS - 1)
        bucket = jnp.where(n < max_exact, n, large)
        vals = [jnp.zeros(shape, F32) for _ in range(N_HEADS)]
        for b in range(N_BUCKETS):
            hit = bucket == b
            vals = [jnp.where(hit, rb_ref[b, h], vals[h]) for h in range(N_HEADS)]
        for h in range(N_HEADS):
            val = jnp.where(delta == N_BIAS_TILES - 1, 0.0, vals[h] * LOG2E)
            o_ref[h, 0, pl.ds(c0, BIAS_ROWS), :] = jnp.where(d >= 0, val, NEG)
        return carry
    lax.fori_loop(0, MOBA_BLOCK // BIAS_ROWS, body, 0)


def _bias_tiles(rel_bias):
    return pl.pallas_call(
        _bias_kernel,
        grid=(N_BIAS_TILES,),
        in_specs=[pl.BlockSpec(memory_space=pltpu.SMEM)],
        out_specs=pl.BlockSpec((N_HEADS, 1, MOBA_BLOCK, MOBA_BLOCK), lambda d: (0, d, 0, 0)),
        out_shape=jax.ShapeDtypeStruct((N_HEADS, N_BIAS_TILES, MOBA_BLOCK, MOBA_BLOCK), F32),
        compiler_params=_params(("parallel",)),
        name="t5_bias_tiles",
    )(rel_bias)


def _attn_kernel(rb_ref, qt_ref, k_ref, vt_ref, km_ref, ind_ref, bias_ref, o_ref,
                 acc_ref, sa_ref, sb_ref, pa_ref, pb_ref):
    i = pl.program_id(1)
    nb = km_ref.shape[0]
    tq = qt_ref.shape[2]
    scale = HEAD_DIM ** -0.5
    heads = range(ATTN_HEADS)
    lanes = [slice(hh * HEAD_DIM, (hh + 1) * HEAD_DIM) for hh in heads]

    def widened_query(hh):
        qf = qt_ref[0, lanes[hh], :].astype(F32)
        gate = jnp.dot(km_ref[:, lanes[hh]], qf, precision=lax.Precision.HIGHEST, preferred_element_type=F32)
        row = lax.broadcasted_iota(jnp.int32, (nb, tq), 0)
        valid = row < i
        g = jnp.where(valid, gate, NEG)
        sel = jnp.zeros((nb, tq), jnp.bool_)
        for _ in range(MOBA_TOPK):
            top = jnp.max(g, axis=0, keepdims=True)
            idx = jnp.min(jnp.where(g == top, row, nb), axis=0, keepdims=True)
            hit = row == idx
            sel = jnp.logical_or(sel, hit)
            g = jnp.where(hit, -jnp.inf, g)
        picked = jnp.logical_and(sel, valid)
        far = jnp.logical_and(picked, row <= i - (N_BIAS_TILES - 1))
        far_bias = rb_ref[N_BUCKETS - 1, pl.program_id(0) * ATTN_HEADS + hh] * LOG2E
        pen = jnp.where(row == i, 0.0, jnp.where(picked, jnp.where(far, far_bias, 0.0), NEG))
        pen_hi = pen.astype(BF16)
        pen_lo = jnp.where(far, pen - pen_hi.astype(F32), 0.0).astype(BF16)
        return jnp.concatenate(
            [(qf * (scale * LOG2E)).astype(BF16), pen_hi, pen_lo, jnp.zeros((HEAD_DIM - 2 * nb, tq), BF16)],
            axis=0)

    q_aug = [widened_query(hh) for hh in heads]
    n_groups = i // KV_GROUP + 1

    def scores(grp, dst_ref):
        r0 = pl.multiple_of(grp * GROUP_KEYS, GROUP_KEYS)
        ind = ind_ref[pl.ds(r0, GROUP_KEYS), :]
        for hh in heads:
            k_aug = jnp.concatenate([k_ref[pl.ds(r0, GROUP_KEYS), lanes[hh]], ind], axis=1)
            dst_ref[hh] = jnp.dot(k_aug, q_aug[hh], preferred_element_type=F32)

    def softmax(grp, src_ref, dst_ref, m_prev):
        m_out, alpha_out = [], []
        for hh in heads:
            parts = []
            for u in range(KV_GROUP):
                delta = jnp.clip(i - (grp * KV_GROUP + u), 0, N_BIAS_TILES - 1)
                parts.append(src_ref[hh, u * MOBA_BLOCK:(u + 1) * MOBA_BLOCK, :] + bias_ref[hh, delta])
            s = jnp.concatenate(parts, axis=0)
            m_new = jnp.maximum(m_prev[hh], jnp.max(s, axis=0, keepdims=True))
            dst_ref[hh] = jnp.exp2(s - m_new).astype(BF16)
            m_out.append(m_new)
            alpha_out.append(jnp.exp2(m_prev[hh] - m_new))
        return tuple(m_out), tuple(alpha_out)

    def weighted_values(grp, live, src_ref, alpha):
        for hh in heads:
            pv = jnp.dot(vt_ref[grp, hh * V_ROWS:(hh + 1) * V_ROWS, :], src_ref[hh], preferred_element_type=F32)
            acc_ref[hh] = alpha[hh] * acc_ref[hh] + jnp.where(live, pv, 0.0)

    acc_ref[...] = jnp.zeros_like(acc_ref)
    pb_ref[...] = jnp.zeros_like(pb_ref)
    scores(n_groups - 1, sa_ref)

    def body(u, carry):
        m_prev, alpha_b, grp_b_prev = carry
        grp_a = n_groups - 1 - 2 * u
        grp_b = grp_a - 1
        weighted_values(jnp.maximum(grp_b_prev, 0), grp_b_prev >= 0, pb_ref, alpha_b)
        m_a, alpha_a = softmax(grp_a, sa_ref, pa_ref, m_prev)
        scores(jnp.maximum(grp_b, 0), sb_ref)
        weighted_values(grp_a, True, pa_ref, alpha_a)
        m_b, alpha_b = softmax(jnp.maximum(grp_b, 0), sb_ref, pb_ref, m_a)
        scores(jnp.maximum(grp_b - 1, 0), sa_ref)
        return m_b, alpha_b, grp_b

    m0 = tuple(jnp.full((1, tq), NEG, F32) for _ in heads)
    a0 = tuple(jnp.ones((1, tq), F32) for _ in heads)
    _, alpha_b, grp_b = lax.fori_loop(0, (n_groups + 1) // 2, body, (m0, a0, jnp.int32(-1)))
    weighted_values(jnp.maximum(grp_b, 0), grp_b >= 0, pb_ref, alpha_b)
    for hh in heads:
        acc = acc_ref[hh]
        o_ref[:, lanes[hh]] = (acc[:HEAD_DIM] / acc[HEAD_DIM:HEAD_DIM + 1]).T.astype(o_ref.dtype)


def _attention(k, km, q_t, v_t, bias, rel_bias):
    S = k.shape[0]
    nb = S // MOBA_BLOCK
    tq = MOBA_BLOCK
    per = INPROJ_ROWS // tq
    width = ATTN_HEADS * HEAD_DIM
    key_blk = jnp.arange(S, dtype=jnp.int32)[:, None] // MOBA_BLOCK
    lane = jnp.arange(HEAD_DIM, dtype=jnp.int32)[None, :]
    ind = jnp.logical_and(lane < 2 * nb, lane % nb == key_blk).astype(BF16)
    return pl.pallas_call(
        _attn_kernel,
        grid=(N_HEADS // ATTN_HEADS, S // tq),
        in_specs=[pl.BlockSpec(memory_space=pltpu.SMEM),
                  pl.BlockSpec((1, width, tq), lambda h, i: (i // per, h, i % per)),
                  pl.BlockSpec((S, width), lambda h, i: (0, h)),
                  pl.BlockSpec((S // GROUP_KEYS, ATTN_HEADS * V_ROWS, GROUP_KEYS), lambda h, i: (0, h, 0)),
                  pl.BlockSpec((nb, width), lambda h, i: (0, h)),
                  pl.BlockSpec((S, HEAD_DIM), lambda h, i: (0, 0)),
                  pl.BlockSpec((ATTN_HEADS, N_BIAS_TILES, MOBA_BLOCK, MOBA_BLOCK), lambda h, i: (h, 0, 0, 0))],
        out_specs=pl.BlockSpec((tq, width), lambda h, i: (i, h)),
        out_shape=jax.ShapeDtypeStruct((S, D_ATTN), BF16),
        scratch_shapes=[pltpu.VMEM((ATTN_HEADS, HEAD_DIM + SUM_ROWS, tq), F32),
                        pltpu.VMEM((ATTN_HEADS, GROUP_KEYS, tq), F32),
                        pltpu.VMEM((ATTN_HEADS, GROUP_KEYS, tq), F32),
                        pltpu.VMEM((ATTN_HEADS, GROUP_KEYS, tq), BF16),
                        pltpu.VMEM((ATTN_HEADS, GROUP_KEYS, tq), BF16)],
        compiler_params=_params(("parallel", "parallel")),
        name="moba_attention",
    )(rel_bias, q_t, k, v_t, km, ind, bias)


def _mix_kernel(cs_ref, at_ref, gate_ref, x_ref, wc_ref, wa_ref, wo_ref, o_ref):
    D = x_ref.shape[1]
    yc = jnp.dot(cs_ref[...], wc_ref[0], preferred_element_type=F32)
    ya = jnp.dot(at_ref[...], wa_ref[0], preferred_element_type=F32)
    merged = gate_ref[:, :D].astype(F32) * yc + gate_ref[:, D:].astype(F32) * ya
    o_ref[...] = x_ref[...] + jnp.dot(merged.astype(BF16), wo_ref[0], preferred_element_type=F32)


def _mix(cs, attn, gates, x, wc, wa, wo, layer, tm=512):
    S, D = x.shape
    full = lambda shape: pl.BlockSpec(shape, lambda i: (layer, 0, 0), pipeline_mode=pl.Buffered(1))
    return pl.pallas_call(
        _mix_kernel,
        grid=(S // tm,),
        in_specs=[pl.BlockSpec((tm, D_CONV), lambda i: (i, 0)),
                  pl.BlockSpec((tm, D_ATTN), lambda i: (i, 0)),
                  pl.BlockSpec((tm, 2 * D), lambda i: (i, 0)),
                  pl.BlockSpec((tm, D), lambda i: (i, 0)),
                  full((1, D_CONV, D)), full((1, D_ATTN, D)), full((1, D, D))],
        out_specs=pl.BlockSpec((tm, D), lambda i: (i, 0)),
        out_shape=jax.ShapeDtypeStruct((S, D), F32),
        compiler_params=_params(("parallel",)),
        name="mix_out_proj",
    )(cs, attn, gates, x, wc, wa, wo)


def _ffn_kernel(xh_ref, x_ref, g_ref, wa_ref, wb_ref, dw_ref, b_ref, wd_ref,
                fg_ref, o_ref, h_ref, gate_ref, *, final_norm, n_tiles, up_chunk):
    i = pl.program_id(0)
    f = pl.program_id(1)
    tm = x_ref.shape[0]
    tf = wa_ref.shape[2]
    slot = f % 2

    def up_stage():
        h = h_ref[...]
        tile = jnp.minimum(f, n_tiles - 1)
        for c0 in range(0, tf, up_chunk):
            cols = slice(c0, c0 + up_chunk)

            def up_conv(w_ref, t):
                u = jnp.dot(h, w_ref[0, :, cols], preferred_element_type=F32)
                y = (u * dw_ref[t, 2:3, cols] + pltpu.roll(u, 1, 0) * dw_ref[t, 1:2, cols]
                     + pltpu.roll(u, 2, 0) * dw_ref[t, 0:1, cols] + b_ref[t, :, cols])
                return y[HALO:]

            ua = up_conv(wa_ref, tile)
            ub = up_conv(wb_ref, tile + n_tiles)
            gate_ref[slot, :, cols] = (jax.nn.silu(ub) * ua).astype(BF16)

    def down_product():
        return jnp.dot(gate_ref[1 - slot], wd_ref[0], preferred_element_type=F32)

    @pl.when(f == 0)
    def _():
        xh = xh_ref[...]
        ms = jnp.mean(xh * xh, axis=-1, keepdims=True)
        hh = xh * lax.rsqrt(ms + EPS) * g_ref[...]
        h_ref[0:HALO, :] = jnp.where(i > 0, hh, jnp.zeros_like(hh)).astype(BF16)
        _rmsnorm_rows(x_ref, g_ref, h_ref, tm, dst_off=HALO)
        o_ref[...] = jnp.zeros_like(o_ref)
        up_stage()

    @pl.when(jnp.logical_and(f > 0, f < n_tiles))
    def _():
        o_ref[...] += down_product()
        up_stage()

    @pl.when(f == n_tiles)
    def _():
        y = x_ref[...] + o_ref[...] + down_product()
        if final_norm:
            ms = jnp.mean(y * y, axis=-1, keepdims=True)
            y = y * lax.rsqrt(ms + EPS) * fg_ref[...]
        o_ref[...] = y


def _ffn(x, g, w_up, dw, dwb, w_down, layer, final_g, final_norm, tm=1024, tf=W_TILE):
    S, D = x.shape
    nf = D_FF // tf
    per = tm // HALO
    dw_tiles = dw.reshape(FFN_CONV_K, 2 * nf, tf).transpose(1, 0, 2)
    b_tiles = dwb.reshape(2 * nf, 1, tf)
    row = lambda shape: pl.BlockSpec(shape, lambda i, f: (0, 0))
    up = lambda f: jnp.minimum(f, nf - 1)
    down = lambda f: jnp.maximum(f - 1, 0)
    return pl.pallas_call(
        functools.partial(_ffn_kernel, final_norm=final_norm, n_tiles=nf, up_chunk=tf),
        grid=(S // tm, nf + 1),
        in_specs=[pl.BlockSpec((HALO, D), lambda i, f: (jnp.maximum(i * per - 1, 0), 0)),
                  pl.BlockSpec((tm, D), lambda i, f: (i, 0), pipeline_mode=pl.Buffered(1)),
                  row((1, D)),
                  pl.BlockSpec((1, D, tf), lambda i, f: (layer, 0, up(f))),
                  pl.BlockSpec((1, D, tf), lambda i, f: (layer, 0, up(f) + nf)),
                  pl.BlockSpec((2 * nf, FFN_CONV_K, tf), lambda i, f: (0, 0, 0)),
                  pl.BlockSpec((2 * nf, 1, tf), lambda i, f: (0, 0, 0)),
                  pl.BlockSpec((1, tf, D), lambda i, f: (layer, down(f), 0)),
                  row((1, D))],
        out_specs=pl.BlockSpec((tm, D), lambda i, f: (i, 0)),
        out_shape=jax.ShapeDtypeStruct((S, D), F32),
        scratch_shapes=[pltpu.VMEM((HALO + tm, D), BF16), pltpu.VMEM((2, tm, tf), BF16)],
        compiler_params=_params(("parallel", "arbitrary")),
        name="conv_ffn",
    )(x, x, g, w_up, w_up, dw_tiles, b_tiles, w_down, final_g)


def kernel(x, norm1_g, w_in, conv_dw, conv_dw_b, conv_ln_g, conv_ln_b, w_conv_out, rel_bias,
           w_attn_out, w_out, norm2_g, w_up, ffn_dw, ffn_dw_b, w_down, final_g):
    B, S, D = x.shape
    assert D == D_MODEL and S % INPROJ_ROWS == 0 and INPROJ_ROWS % GROUP_KEYS == 0
    depth = w_in.shape[0]
    bias = _bias_tiles(rel_bias)
    row = lambda v: v.reshape(1, -1)
    w_in_bf = w_in.astype(BF16)
    w_conv_bf, w_attn_bf, w_out_bf = (w.astype(BF16) for w in (w_conv_out, w_attn_out, w_out))
    w_up_bf, w_down_bf = w_up.astype(BF16), w_down.astype(BF16)
    outs = []
    for b in range(B):
        xb = x.reshape(S, D) if B == 1 else x[b]
        for l in range(depth):
            glu, k, km, q_t, v_t, gates = _inproj_fused(xb, row(norm1_g[l]), w_in_bf, l)
            cs = _conv_branch(glu, conv_dw[l], row(conv_dw_b[l]), row(conv_ln_g[l]), row(conv_ln_b[l]))
            attn = _attention(k, km, q_t, v_t, bias, rel_bias)
            xb = _mix(cs, attn, gates, xb, w_conv_bf, w_attn_bf, w_out_bf, l)
            xb = _ffn(xb, row(norm2_g[l]), w_up_bf, ffn_dw[l], row(ffn_dw_b[l]),
                      w_down_bf, l, row(final_g), final_norm=(l == depth - 1))
        outs.append(xb)
    return outs[0].reshape(1, S, D) if B == 1 else jnp.stack(outs, axis=0)
```

```python
import functools
import math

import jax
import jax.numpy as jnp
from jax import lax
from jax.experimental import pallas as pl
from jax.experimental.pallas import tpu as pltpu

F32 = jnp.float32
BF16 = jnp.bfloat16

D_MODEL = 2048
D_CONV = 1024
CONV_K = 31
N_HEADS = 8
HEAD_DIM = 128
D_ATTN = N_HEADS * HEAD_DIM
MOBA_BLOCK = 256
MOBA_TOPK = 3
N_BUCKETS = 32
REL_MAX_DIST = 2048
D_FF = 5632
FFN_CONV_K = 3
EPS = 1e-6
NEG = -1e30
LOG2E = math.log2(math.e)

N_BIAS_TILES = 8
BIAS_ROWS = 16
KV_GROUP = 2
GROUP_KEYS = KV_GROUP * MOBA_BLOCK
INPROJ_ROWS = 1024
SUM_ROWS = 16
V_ROWS = HEAD_DIM + SUM_ROWS
ATTN_HEADS = 2

V7X_VMEM_BYTES = 64 * 1024 * 1024
VMEM_LIMIT = V7X_VMEM_BYTES - 8 * 1024 * 1024

SUBLANES = 8
NORM_ROWS = 16
NORM_UNROLL = 8
HALO = 8
CONV_HALO = 32

W_TILE = 512

TN_DIMS = (((0,), (1,)), ((), ()))


def _params(sem):
    return pltpu.CompilerParams(dimension_semantics=sem, vmem_limit_bytes=VMEM_LIMIT)


def _rmsnorm_rows(x_ref, g_ref, dst_ref, n_rows, dst_off=0, out_dtype=BF16):
    def body(c, carry):
        r = pl.multiple_of(c * NORM_ROWS, NORM_ROWS)
        xv = x_ref[pl.ds(r, NORM_ROWS), :]
        ms = jnp.mean(xv * xv, axis=-1, keepdims=True)
        y = xv * lax.rsqrt(ms + EPS) * g_ref[...]
        dst_ref[pl.ds(dst_off + r, NORM_ROWS), :] = y.astype(out_dtype)
        return carry
    lax.fori_loop(0, n_rows // NORM_ROWS, body, 0, unroll=NORM_UNROLL)


def _inproj_kernel(x_ref, g_ref, w_ref, w2_ref, glu_ref, k_ref, km_ref, q_ref, v_ref, gate_ref,
                   h_ref, *, seg):
    j = pl.program_id(1)
    n_glu, n_k, n_q, n_v, _ = seg
    k0 = n_glu
    q0 = k0 + n_k
    v0 = q0 + n_q
    g0 = v0 + n_v

    @pl.when(j == 0)
    def _():
        _rmsnorm_rows(x_ref, g_ref, h_ref, x_ref.shape[0])

    @pl.when(j < k0)
    def _():
        h = h_ref[...]
        a = jnp.dot(h, w_ref[0], preferred_element_type=F32)
        b = jnp.dot(h, w2_ref[0], preferred_element_type=F32)
        glu_ref[...] = a * jax.nn.sigmoid(b)

    @pl.when(jnp.logical_and(j >= k0, j < q0))
    def _():
        acc = jnp.dot(h_ref[...], w_ref[0], preferred_element_type=F32)
        k_ref[...] = acc.astype(k_ref.dtype)
        for b in range(acc.shape[0] // MOBA_BLOCK):
            blk = acc[b * MOBA_BLOCK:(b + 1) * MOBA_BLOCK]
            km_ref[0, b:b + 1, :] = jnp.mean(blk, axis=0, keepdims=True)

    @pl.when(jnp.logical_and(j >= q0, j < v0))
    def _():
        acc = lax.dot_general(w_ref[0], h_ref[...], TN_DIMS, preferred_element_type=F32)
        q_ref[0] = acc.astype(q_ref.dtype)

    @pl.when(jnp.logical_and(j >= v0, j < g0))
    def _():
        acc = lax.dot_general(w_ref[0], h_ref[...], TN_DIMS, preferred_element_type=F32)
        for grp in range(v_ref.shape[0]):
            pos = slice(grp * GROUP_KEYS, (grp + 1) * GROUP_KEYS)
            for hh in range(acc.shape[0] // HEAD_DIM):
                r = hh * V_ROWS
                v_ref[grp, r:r + HEAD_DIM, :] = acc[hh * HEAD_DIM:(hh + 1) * HEAD_DIM, pos].astype(v_ref.dtype)
                v_ref[grp, r + HEAD_DIM:r + V_ROWS, :] = jnp.ones((SUM_ROWS, GROUP_KEYS), v_ref.dtype)

    @pl.when(j >= g0)
    def _():
        acc = jnp.dot(h_ref[...], w_ref[0], preferred_element_type=F32)
        gate_ref[...] = jax.nn.sigmoid(acc).astype(gate_ref.dtype)


def _inproj_fused(x, g, w_in, layer, tn=W_TILE):
    S, D = x.shape
    tm = INPROJ_ROWS
    nm = S // tm
    groups = tm // GROUP_KEYS
    seg = (D_CONV // tn, D_ATTN // tn, D_ATTN // tn, D_ATTN // tn, 2 * D_MODEL // tn)
    n_glu, n_k, n_q, n_v, n_g = seg
    k0, q0 = n_glu, n_glu + n_k
    v0 = q0 + n_q
    g0 = v0 + n_v
    col_q = 2 * D_CONV // tn
    col_k = col_q + n_q

    def w_col(j):
        return jnp.where(j < k0, j, jnp.where(j < q0, col_k + (j - k0),
                         jnp.where(j < v0, col_q + (j - q0), col_k + n_k + (j - v0))))

    seg_tile = lambda j, start, n: jnp.clip(j - start, 0, n - 1)
    v_tile_rows = tn // HEAD_DIM * V_ROWS
    glu, k, km, q_t, v_t, gates = pl.pallas_call(
        functools.partial(_inproj_kernel, seg=seg),
        grid=(nm, sum(seg)),
        in_specs=[pl.BlockSpec((tm, D), lambda i, j: (i, 0)),
                  pl.BlockSpec((1, D), lambda i, j: (0, 0)),
                  pl.BlockSpec((1, D, tn), lambda i, j: (layer, 0, w_col(j))),
                  pl.BlockSpec((1, D, tn), lambda i, j: (layer, 0, D_CONV // tn + seg_tile(j, 0, n_glu)))],
        out_specs=[pl.BlockSpec((tm, tn), lambda i, j: (i, seg_tile(j, 0, n_glu))),
                   pl.BlockSpec((tm, tn), lambda i, j: (i, seg_tile(j, k0, n_k))),
                   pl.BlockSpec((1, tm // MOBA_BLOCK, tn), lambda i, j: (i, 0, seg_tile(j, k0, n_k))),
                   pl.BlockSpec((1, tn, tm), lambda i, j: (i, seg_tile(j, q0, n_q), 0)),
                   pl.BlockSpec((groups, v_tile_rows, GROUP_KEYS), lambda i, j: (i, seg_tile(j, v0, n_v), 0)),
                   pl.BlockSpec((tm, tn), lambda i, j: (i, seg_tile(j, g0, n_g)))],
        out_shape=[jax.ShapeDtypeStruct((S, D_CONV), F32),
                   jax.ShapeDtypeStruct((S, D_ATTN), BF16),
                   jax.ShapeDtypeStruct((nm, tm // MOBA_BLOCK, D_ATTN), F32),
                   jax.ShapeDtypeStruct((nm, D_ATTN, tm), BF16),
                   jax.ShapeDtypeStruct((S // GROUP_KEYS, N_HEADS * V_ROWS, GROUP_KEYS), BF16),
                   jax.ShapeDtypeStruct((S, 2 * D_MODEL), BF16)],
        scratch_shapes=[pltpu.VMEM((tm, D), BF16)],
        compiler_params=_params(("parallel", "arbitrary")),
        name="inproj",
    )(x, g, w_in, w_in)
    return glu, k, km.reshape(S // MOBA_BLOCK, D_ATTN), q_t, v_t, gates


def _conv_kernel(halo_ref, x_ref, w_ref, b_ref, lg_ref, lb_ref, o_ref, win_ref, y_ref,
                 *, rows, lanes):
    tm, C = x_ref.shape
    i = pl.program_id(0)
    halo = halo_ref[...]
    win_ref[0:CONV_HALO, :] = jnp.where(i > 0, halo, jnp.zeros_like(halo))
    win_ref[CONV_HALO:, :] = x_ref[...]
    first = CONV_HALO - (CONV_K - 1)

    for c0 in range(0, C, lanes):
        def body(rc, carry):
            r = pl.multiple_of(rc * rows, rows)
            acc = jnp.zeros((rows, lanes), F32) + b_ref[:, c0:c0 + lanes]
            win = win_ref[pl.ds(r, rows + CONV_HALO), c0:c0 + lanes]
            for sub in range(SUBLANES):
                shifted = win if sub == 0 else pltpu.roll(win, win.shape[0] - sub, 0)
                usable = win.shape[0] - (SUBLANES if sub else 0)
                for base in range(0, usable - rows + 1, SUBLANES):
                    k = base + sub - first
                    if 0 <= k < CONV_K:
                        acc = acc + shifted[base:base + rows] * w_ref[k:k + 1, c0:c0 + lanes]
            y_ref[pl.ds(r, rows), c0:c0 + lanes] = acc
            return carry
        lax.fori_loop(0, tm // rows, body, 0)

    def ln_body(rc, carry):
        r = pl.multiple_of(rc * NORM_ROWS, NORM_ROWS)
        v = y_ref[pl.ds(r, NORM_ROWS), :]
        mu = jnp.mean(v, axis=-1, keepdims=True)
        var = jnp.mean(jnp.square(v - mu), axis=-1, keepdims=True)
        z = (v - mu) * lax.rsqrt(var + EPS) * lg_ref[...] + lb_ref[...]
        o_ref[pl.ds(r, NORM_ROWS), :] = jax.nn.silu(z).astype(o_ref.dtype)
        return carry
    lax.fori_loop(0, tm // NORM_ROWS, ln_body, 0, unroll=NORM_UNROLL)


def _conv_branch(glu, w, b, lg, lb, tm=512):
    S, C = glu.shape
    per = tm // CONV_HALO
    return pl.pallas_call(
        functools.partial(_conv_kernel, rows=128, lanes=128),
        grid=(S // tm,),
        in_specs=[pl.BlockSpec((CONV_HALO, C), lambda i: (jnp.maximum(i * per - 1, 0), 0)),
                  pl.BlockSpec((tm, C), lambda i: (i, 0)),
                  pl.BlockSpec((CONV_K, C), lambda i: (0, 0)),
                  pl.BlockSpec((1, C), lambda i: (0, 0)),
                  pl.BlockSpec((1, C), lambda i: (0, 0)),
                  pl.BlockSpec((1, C), lambda i: (0, 0))],
        out_specs=pl.BlockSpec((tm, C), lambda i: (i, 0)),
        out_shape=jax.ShapeDtypeStruct((S, C), BF16),
        scratch_shapes=[pltpu.VMEM((tm + CONV_HALO, C), F32), pltpu.VMEM((tm, C), F32)],
        compiler_params=_params(("parallel",)),
        name="conv_branch",
    )(glu, glu, w, b, lg, lb)


def _bias_kernel(rb_ref, o_ref):
    delta = pl.program_id(0)
    shape = (BIAS_ROWS, MOBA_BLOCK)

    def body(c, carry):
        c0 = pl.multiple_of(c * BIAS_ROWS, BIAS_ROWS)
        key = c0 + lax.broadcasted_iota(jnp.int32, shape, 0)
        d = delta * MOBA_BLOCK + lax.broadcasted_iota(jnp.int32, shape, 1) - key
        n = jnp.maximum(d, 0)
        max_exact = N_BUCKETS // 2
        nf = jnp.maximum(n, 1).astype(F32)
        large = max_exact + (jnp.log(nf / max_exact) / math.log(REL_MAX_DIST / max_exact)
                             * (N_BUCKETS - max_exact)).astype(jnp.int32)
        large = jnp.minimum(large, N_BUCKETS - 1)
        bucket = jnp.where(n < max_exact, n, large)
        vals = [jnp.zeros(shape, F32) for _ in range(N_HEADS)]
        for b in range(N_BUCKETS):
            hit = bucket == b
            vals = [jnp.where(hit, rb_ref[b, h], vals[h]) for h in range(N_HEADS)]
        for h in range(N_HEADS):
            val = jnp.where(delta == N_BIAS_TILES - 1, 0.0, vals[h] * LOG2E)
            o_ref[h, 0, pl.ds(c0, BIAS_ROWS), :] = jnp.where(d >= 0, val, NEG)
        return carry
    lax.fori_loop(0, MOBA_BLOCK // BIAS_ROWS, body, 0)


def _bias_tiles(rel_bias):
    return pl.pallas_call(
        _bias_kernel,
        grid=(N_BIAS_TILES,),
        in_specs=[pl.BlockSpec(memory_space=pltpu.SMEM)],
        out_specs=pl.BlockSpec((N_HEADS, 1, MOBA_BLOCK, MOBA_BLOCK), lambda d: (0, d, 0, 0)),
        out_shape=jax.ShapeDtypeStruct((N_HEADS, N_BIAS_TILES, MOBA_BLOCK, MOBA_BLOCK), F32),
        compiler_params=_params(("parallel",)),
        name="t5_bias_tiles",
    )(rel_bias)


def _attn_kernel(rb_ref, qt_ref, k_ref, vt_ref, km_ref, ind_ref, bias_ref, o_ref,
                 acc_ref, sa_ref, sb_ref, pa_ref, pb_ref):
    i = pl.program_id(1)
    nb = km_ref.shape[0]
    tq = qt_ref.shape[2]
    scale = HEAD_DIM ** -0.5
    heads = range(ATTN_HEADS)
    lanes = [slice(hh * HEAD_DIM, (hh + 1) * HEAD_DIM) for hh in heads]

    def widened_query(hh):
        qf = qt_ref[0, lanes[hh], :].astype(F32)
        gate = jnp.dot(km_ref[:, lanes[hh]], qf, precision=lax.Precision.HIGHEST, preferred_element_type=F32)
        row = lax.broadcasted_iota(jnp.int32, (nb, tq), 0)
        valid = row < i
        g = jnp.where(valid, gate, NEG)
        sel = jnp.zeros((nb, tq), jnp.bool_)
        for _ in range(MOBA_TOPK):
            top = jnp.max(g, axis=0, keepdims=True)
            idx = jnp.min(jnp.where(g == top, row, nb), axis=0, keepdims=True)
            hit = row == idx
            sel = jnp.logical_or(sel, hit)
            g = jnp.where(hit, -jnp.inf, g)
        picked = jnp.logical_and(sel, valid)
        far = jnp.logical_and(picked, row <= i - (N_BIAS_TILES - 1))
        far_bias = rb_ref[N_BUCKETS - 1, pl.program_id(0) * ATTN_HEADS + hh] * LOG2E
        pen = jnp.where(row == i, 0.0, jnp.where(picked, jnp.where(far, far_bias, 0.0), NEG))
        pen_hi = pen.astype(BF16)
        pen_lo = jnp.where(far, pen - pen_hi.astype(F32), 0.0).astype(BF16)
        return jnp.concatenate(
            [(qf * (scale * LOG2E)).astype(BF16), pen_hi, pen_lo, jnp.zeros((HEAD_DIM - 2 * nb, tq), BF16)],
            axis=0)

    q_aug = [widened_query(hh) for hh in heads]
    n_groups = i // KV_GROUP + 1

    def scores(grp, dst_ref):
        r0 = pl.multiple_of(grp * GROUP_KEYS, GROUP_KEYS)
        ind = ind_ref[pl.ds(r0, GROUP_KEYS), :]
        for hh in heads:
            k_aug = jnp.concatenate([k_ref[pl.ds(r0, GROUP_KEYS), lanes[hh]], ind], axis=1)
            dst_ref[hh] = jnp.dot(k_aug, q_aug[hh], preferred_element_type=F32)

    def softmax(grp, src_ref, dst_ref, m_prev):
        m_out, alpha_out = [], []
        for hh in heads:
            parts = []
            for u in range(KV_GROUP):
                delta = jnp.clip(i - (grp * KV_GROUP + u), 0, N_BIAS_TILES - 1)
                parts.append(src_ref[hh, u * MOBA_BLOCK:(u + 1) * MOBA_BLOCK, :] + bias_ref[hh, delta])
            s = jnp.concatenate(parts, axis=0)
            m_new = jnp.maximum(m_prev[hh], jnp.max(s, axis=0, keepdims=True))
            dst_ref[hh] = jnp.exp2(s - m_new).astype(BF16)
            m_out.append(m_new)
            alpha_out.append(jnp.exp2(m_prev[hh] - m_new))
        return tuple(m_out), tuple(alpha_out)

    def weighted_values(grp, live, src_ref, alpha):
        for hh in heads:
            pv = jnp.dot(vt_ref[grp, hh * V_ROWS:(hh + 1) * V_ROWS, :], src_ref[hh], preferred_element_type=F32)
            acc_ref[hh] = alpha[hh] * acc_ref[hh] + jnp.where(live, pv, 0.0)

    acc_ref[...] = jnp.zeros_like(acc_ref)
    pb_ref[...] = jnp.zeros_like(pb_ref)
    scores(n_groups - 1, sa_ref)

    def body(u, carry):
        m_prev, alpha_b, grp_b_prev = carry
        grp_a = n_groups - 1 - 2 * u
        grp_b = grp_a - 1
        weighted_values(jnp.maximum(grp_b_prev, 0), grp_b_prev >= 0, pb_ref, alpha_b)
        m_a, alpha_a = softmax(grp_a, sa_ref, pa_ref, m_prev)
        scores(jnp.maximum(grp_b, 0), sb_ref)
        weighted_values(grp_a, True, pa_ref, alpha_a)
        m_b, alpha_b = softmax(jnp.maximum(grp_b, 0), sb_ref, pb_ref, m_a)
        scores(jnp.maximum(grp_b - 1, 0), sa_ref)
        return m_b, alpha_b, grp_b

    m0 = tuple(jnp.full((1, tq), NEG, F32) for _ in heads)
    a0 = tuple(jnp.ones((1, tq), F32) for _ in heads)
    _, alpha_b, grp_b = lax.fori_loop(0, (n_groups + 1) // 2, body, (m0, a0, jnp.int32(-1)))
    weighted_values(jnp.maximum(grp_b, 0), grp_b >= 0, pb_ref, alpha_b)
    for hh in heads:
        acc = acc_ref[hh]
        o_ref[:, lanes[hh]] = (acc[:HEAD_DIM] / acc[HEAD_DIM:HEAD_DIM + 1]).T.astype(o_ref.dtype)


def _attention(k, km, q_t, v_t, bias, rel_bias):
    S = k.shape[0]
    nb = S // MOBA_BLOCK
    tq = MOBA_BLOCK
    per = INPROJ_ROWS // tq
    width = ATTN_HEADS * HEAD_DIM
    key_blk = jnp.arange(S, dtype=jnp.int32)[:, None] // MOBA_BLOCK
    lane = jnp.arange(HEAD_DIM, dtype=jnp.int32)[None, :]
    ind = jnp.logical_and(lane < 2 * nb, lane % nb == key_blk).astype(BF16)
    return pl.pallas_call(
        _attn_kernel,
        grid=(N_HEADS // ATTN_HEADS, S // tq),
        in_specs=[pl.BlockSpec(memory_space=pltpu.SMEM),
                  pl.BlockSpec((1, width, tq), lambda h, i: (i // per, h, i % per)),
                  pl.BlockSpec((S, width), lambda h, i: (0, h)),
                  pl.BlockSpec((S // GROUP_KEYS, ATTN_HEADS * V_ROWS, GROUP_KEYS), lambda h, i: (0, h, 0)),
                  pl.BlockSpec((nb, width), lambda h, i: (0, h)),
                  pl.BlockSpec((S, HEAD_DIM), lambda h, i: (0, 0)),
                  pl.BlockSpec((ATTN_HEADS, N_BIAS_TILES, MOBA_BLOCK, MOBA_BLOCK), lambda h, i: (h, 0, 0, 0))],
        out_specs=pl.BlockSpec((tq, width), lambda h, i: (i, h)),
        out_shape=jax.ShapeDtypeStruct((S, D_ATTN), BF16),
        scratch_shapes=[pltpu.VMEM((ATTN_HEADS, HEAD_DIM + SUM_ROWS, tq), F32),
                        pltpu.VMEM((ATTN_HEADS, GROUP_KEYS, tq), F32),
                        pltpu.VMEM((ATTN_HEADS, GROUP_KEYS, tq), F32),
                        pltpu.VMEM((ATTN_HEADS, GROUP_KEYS, tq), BF16),
                        pltpu.VMEM((ATTN_HEADS, GROUP_KEYS, tq), BF16)],
        compiler_params=_params(("parallel", "parallel")),
        name="moba_attention",
    )(rel_bias, q_t, k, v_t, km, ind, bias)


def _mix_kernel(cs_ref, at_ref, gate_ref, x_ref, wc_ref, wa_ref, wo_ref, o_ref):
    D = x_ref.shape[1]
    yc = jnp.dot(cs_ref[...], wc_ref[0], preferred_element_type=F32)
    ya = jnp.dot(at_ref[...], wa_ref[0], preferred_element_type=F32)
    merged = gate_ref[:, :D].astype(F32) * yc + gate_ref[:, D:].astype(F32) * ya
    o_ref[...] = x_ref[...] + jnp.dot(merged.astype(BF16), wo_ref[0], preferred_element_type=F32)


def _mix(cs, attn, gates, x, wc, wa, wo, layer, tm=512):
    S, D = x.shape
    full = lambda shape: pl.BlockSpec(shape, lambda i: (layer, 0, 0), pipeline_mode=pl.Buffered(1))
    return pl.pallas_call(
        _mix_kernel,
        grid=(S // tm,),
        in_specs=[pl.BlockSpec((tm, D_CONV), lambda i: (i, 0)),
                  pl.BlockSpec((tm, D_ATTN), lambda i: (i, 0)),
                  pl.BlockSpec((tm, 2 * D), lambda i: (i, 0)),
                  pl.BlockSpec((tm, D), lambda i: (i, 0)),
                  full((1, D_CONV, D)), full((1, D_ATTN, D)), full((1, D, D))],
        out_specs=pl.BlockSpec((tm, D), lambda i: (i, 0)),
        out_shape=jax.ShapeDtypeStruct((S, D), F32),
        compiler_params=_params(("parallel",)),
        name="mix_out_proj",
    )(cs, attn, gates, x, wc, wa, wo)


def _ffn_kernel(xh_ref, x_ref, g_ref, wa_ref, wb_ref, dw_ref, b_ref, wd_ref,
                fg_ref, o_ref, h_ref, gate_ref, *, final_norm, n_tiles, up_chunk):
    i = pl.program_id(0)
    f = pl.program_id(1)
    tm = x_ref.shape[0]
    tf = wa_ref.shape[2]
    slot = f % 2

    def up_stage():
        h = h_ref[...]
        tile = jnp.minimum(f, n_tiles - 1)
        for c0 in range(0, tf, up_chunk):
            cols = slice(c0, c0 + up_chunk)

            def up_conv(w_ref, t):
                u = jnp.dot(h, w_ref[0, :, cols], preferred_element_type=F32)
                y = (u * dw_ref[t, 2:3, cols] + pltpu.roll(u, 1, 0) * dw_ref[t, 1:2, cols]
                     + pltpu.roll(u, 2, 0) * dw_ref[t, 0:1, cols] + b_ref[t, :, cols])
                return y[HALO:]

            ua = up_conv(wa_ref, tile)
            ub = up_conv(wb_ref, tile + n_tiles)
            gate_ref[slot, :, cols] = (jax.nn.silu(ub) * ua).astype(BF16)

    def down_product():
        return jnp.dot(gate_ref[1 - slot], wd_ref[0], preferred_element_type=F32)

    @pl.when(f == 0)
    def _():
        xh = xh_ref[...]
        ms = jnp.mean(xh * xh, axis=-1, keepdims=True)
        hh = xh * lax.rsqrt(ms + EPS) * g_ref[...]
        h_ref[0:HALO, :] = jnp.where(i > 0, hh, jnp.zeros_like(hh)).astype(BF16)
        _rmsnorm_rows(x_ref, g_ref, h_ref, tm, dst_off=HALO)
        o_ref[...] = jnp.zeros_like(o_ref)
        up_stage()

    @pl.when(jnp.logical_and(f > 0, f < n_tiles))
    def _():
        o_ref[...] += down_product()
        up_stage()

    @pl.when(f == n_tiles)
    def _():
        y = x_ref[...] + o_ref[...] + down_product()
        if final_norm:
            ms = jnp.mean(y * y, axis=-1, keepdims=True)
            y = y * lax.rsqrt(ms + EPS) * fg_ref[...]
        o_ref[...] = y


def _ffn(x, g, w_up, dw, dwb, w_down, layer, final_g, final_norm, tm=1024, tf=W_TILE):
    S, D = x.shape
    nf = D_FF // tf
    per = tm // HALO
    dw_tiles = dw.reshape(FFN_CONV_K, 2 * nf, tf).transpose(1, 0, 2)
    b_tiles = dwb.reshape(2 * nf, 1, tf)
    row = lambda shape: pl.BlockSpec(shape, lambda i, f: (0, 0))
    up = lambda f: jnp.minimum(f, nf - 1)
    down = lambda f: jnp.maximum(f - 1, 0)
    return pl.pallas_call(
        functools.partial(_ffn_kernel, final_norm=final_norm, n_tiles=nf, up_chunk=tf),
        grid=(S // tm, nf + 1),
        in_specs=[pl.BlockSpec((HALO, D), lambda i, f: (jnp.maximum(i * per - 1, 0), 0)),
                  pl.BlockSpec((tm, D), lambda i, f: (i, 0), pipeline_mode=pl.Buffered(1)),
                  row((1, D)),
                  pl.BlockSpec((1, D, tf), lambda i, f: (layer, 0, up(f))),
                  pl.BlockSpec((1, D, tf), lambda i, f: (layer, 0, up(f) + nf)),
                  pl.BlockSpec((2 * nf, FFN_CONV_K, tf), lambda i, f: (0, 0, 0)),
                  pl.BlockSpec((2 * nf, 1, tf), lambda i, f: (0, 0, 0)),
                  pl.BlockSpec((1, tf, D), lambda i, f: (layer, down(f), 0)),
                  row((1, D))],
        out_specs=pl.BlockSpec((tm, D), lambda i, f: (i, 0)),
        out_shape=jax.ShapeDtypeStruct((S, D), F32),
        scratch_shapes=[pltpu.VMEM((HALO + tm, D), BF16), pltpu.VMEM((2, tm, tf), BF16)],
        compiler_params=_params(("parallel", "arbitrary")),
        name="conv_ffn",
    )(x, x, g, w_up, w_up, dw_tiles, b_tiles, w_down, final_g)


def kernel(x, norm1_g, w_in, conv_dw, conv_dw_b, conv_ln_g, conv_ln_b, w_conv_out, rel_bias,
           w_attn_out, w_out, norm2_g, w_up, ffn_dw, ffn_dw_b, w_down, final_g):
    B, S, D = x.shape
    assert D == D_MODEL and S % INPROJ_ROWS == 0 and INPROJ_ROWS % GROUP_KEYS == 0
    depth = w_in.shape[0]
    bias = _bias_tiles(rel_bias)
    row = lambda v: v.reshape(1, -1)
    w_in_bf = w_in.astype(BF16)
    w_conv_bf, w_attn_bf, w_out_bf = (w.astype(BF16) for w in (w_conv_out, w_attn_out, w_out))
    w_up_bf, w_down_bf = w_up.astype(BF16), w_down.astype(BF16)
    outs = []
    for b in range(B):
        xb = x.reshape(S, D) if B == 1 else x[b]
        for l in range(depth):
            glu, k, km, q_t, v_t, gates = _inproj_fused(xb, row(norm1_g[l]), w_in_bf, l)
            cs = _conv_branch(glu, conv_dw[l], row(conv_dw_b[l]), row(conv_ln_g[l]), row(conv_ln_b[l]))
            attn = _attention(k, km, q_t, v_t, bias, rel_bias)
            xb = _mix(cs, attn, gates, xb, w_conv_bf, w_attn_bf, w_out_bf, l)
            xb = _ffn(xb, row(norm2_g[l]), w_up_bf, ffn_dw[l], row(ffn_dw_b[l]),
                      w_down_bf, l, row(final_g), final_norm=(l == depth - 1))
        outs.append(xb)
    return outs[0].reshape(1, S, D) if B == 1 else jnp.stack(outs, axis=0)
```

```python
import functools
import math

import jax
import jax.numpy as jnp
from jax import lax
from jax.experimental import pallas as pl
from jax.experimental.pallas import tpu as pltpu

F32 = jnp.float32
BF16 = jnp.bfloat16

D_MODEL = 2048
D_CONV = 1024
CONV_K = 31
N_HEADS = 8
HEAD_DIM = 128
D_ATTN = N_HEADS * HEAD_DIM
MOBA_BLOCK = 256
MOBA_TOPK = 3
N_BUCKETS = 32
REL_MAX_DIST = 2048
D_FF = 5632
FFN_CONV_K = 3
EPS = 1e-6
NEG = -1e30
LOG2E = math.log2(math.e)

N_BIAS_TILES = 8
BIAS_ROWS = 16
KV_GROUP = 2
GROUP_KEYS = KV_GROUP * MOBA_BLOCK
INPROJ_ROWS = 1024
SUM_ROWS = 16
V_ROWS = HEAD_DIM + SUM_ROWS
ATTN_HEADS = 2
ATTN_Q_BLOCKS = 2

V7X_VMEM_BYTES = 64 * 1024 * 1024
VMEM_LIMIT = V7X_VMEM_BYTES - 8 * 1024 * 1024

SUBLANES = 8
NORM_ROWS = 16
NORM_UNROLL = 8
HALO = 8
CONV_HALO = 32

W_TILE = 512

TN_DIMS = (((0,), (1,)), ((), ()))


def _params(sem):
    return pltpu.CompilerParams(dimension_semantics=sem, vmem_limit_bytes=VMEM_LIMIT)


def _rmsnorm_rows(x_ref, g_ref, dst_ref, n_rows, dst_off=0, out_dtype=BF16):
    def body(c, carry):
        r = pl.multiple_of(c * NORM_ROWS, NORM_ROWS)
        xv = x_ref[pl.ds(r, NORM_ROWS), :]
        ms = jnp.mean(xv * xv, axis=-1, keepdims=True)
        y = xv * lax.rsqrt(ms + EPS) * g_ref[...]
        dst_ref[pl.ds(dst_off + r, NORM_ROWS), :] = y.astype(out_dtype)
        return carry
    lax.fori_loop(0, n_rows // NORM_ROWS, body, 0, unroll=NORM_UNROLL)


def _inproj_kernel(x_ref, g_ref, w_ref, w2_ref, glu_ref, k_ref, km_ref, q_ref, v_ref, gate_ref,
                   h_ref, *, seg):
    j = pl.program_id(1)
    n_glu, n_k, n_q, n_v, _ = seg
    k0 = n_glu
    q0 = k0 + n_k
    v0 = q0 + n_q
    g0 = v0 + n_v

    @pl.when(j == 0)
    def _():
        _rmsnorm_rows(x_ref, g_ref, h_ref, x_ref.shape[0])

    @pl.when(j < k0)
    def _():
        h = h_ref[...]
        a = jnp.dot(h, w_ref[0], preferred_element_type=F32)
        b = jnp.dot(h, w2_ref[0], preferred_element_type=F32)
        glu_ref[...] = a * jax.nn.sigmoid(b)

    @pl.when(jnp.logical_and(j >= k0, j < q0))
    def _():
        acc = jnp.dot(h_ref[...], w_ref[0], preferred_element_type=F32)
        k_ref[...] = acc.astype(k_ref.dtype)
        for b in range(acc.shape[0] // MOBA_BLOCK):
            blk = acc[b * MOBA_BLOCK:(b + 1) * MOBA_BLOCK]
            km_ref[0, b:b + 1, :] = jnp.mean(blk, axis=0, keepdims=True)

    @pl.when(jnp.logical_and(j >= q0, j < v0))
    def _():
        acc = lax.dot_general(w_ref[0], h_ref[...], TN_DIMS, preferred_element_type=F32)
        q_ref[0] = acc.astype(q_ref.dtype)

    @pl.when(jnp.logical_and(j >= v0, j < g0))
    def _():
        acc = lax.dot_general(w_ref[0], h_ref[...], TN_DIMS, preferred_element_type=F32)
        for grp in range(v_ref.shape[0]):
            pos = slice(grp * GROUP_KEYS, (grp + 1) * GROUP_KEYS)
            for hh in range(acc.shape[0] // HEAD_DIM):
                r = hh * V_ROWS
                v_ref[grp, r:r + HEAD_DIM, :] = acc[hh * HEAD_DIM:(hh + 1) * HEAD_DIM, pos].astype(v_ref.dtype)
                v_ref[grp, r + HEAD_DIM:r + V_ROWS, :] = jnp.ones((SUM_ROWS, GROUP_KEYS), v_ref.dtype)

    @pl.when(j >= g0)
    def _():
        acc = jnp.dot(h_ref[...], w_ref[0], preferred_element_type=F32)
        gate_ref[...] = jax.nn.sigmoid(acc).astype(gate_ref.dtype)


def _inproj_fused(x, g, w_in, layer, tn=W_TILE):
    S, D = x.shape
    tm = INPROJ_ROWS
    nm = S // tm
    groups = tm // GROUP_KEYS
    seg = (D_CONV // tn, D_ATTN // tn, D_ATTN // tn, D_ATTN // tn, 2 * D_MODEL // tn)
    n_glu, n_k, n_q, n_v, n_g = seg
    k0, q0 = n_glu, n_glu + n_k
    v0 = q0 + n_q
    g0 = v0 + n_v
    col_q = 2 * D_CONV // tn
    col_k = col_q + n_q

    def w_col(j):
        return jnp.where(j < k0, j, jnp.where(j < q0, col_k + (j - k0),
                         jnp.where(j < v0, col_q + (j - q0), col_k + n_k + (j - v0))))

    seg_tile = lambda j, start, n: jnp.clip(j - start, 0, n - 1)
    v_tile_rows = tn // HEAD_DIM * V_ROWS
    glu, k, km, q_t, v_t, gates = pl.pallas_call(
        functools.partial(_inproj_kernel, seg=seg),
        grid=(nm, sum(seg)),
        in_specs=[pl.BlockSpec((tm, D), lambda i, j: (i, 0)),
                  pl.BlockSpec((1, D), lambda i, j: (0, 0)),
                  pl.BlockSpec((1, D, tn), lambda i, j: (layer, 0, w_col(j))),
                  pl.BlockSpec((1, D, tn), lambda i, j: (layer, 0, D_CONV // tn + seg_tile(j, 0, n_glu)))],
        out_specs=[pl.BlockSpec((tm, tn), lambda i, j: (i, seg_tile(j, 0, n_glu))),
                   pl.BlockSpec((tm, tn), lambda i, j: (i, seg_tile(j, k0, n_k))),
                   pl.BlockSpec((1, tm // MOBA_BLOCK, tn), lambda i, j: (i, 0, seg_tile(j, k0, n_k))),
                   pl.BlockSpec((1, tn, tm), lambda i, j: (i, seg_tile(j, q0, n_q), 0)),
                   pl.BlockSpec((groups, v_tile_rows, GROUP_KEYS), lambda i, j: (i, seg_tile(j, v0, n_v), 0)),
                   pl.BlockSpec((tm, tn), lambda i, j: (i, seg_tile(j, g0, n_g)))],
        out_shape=[jax.ShapeDtypeStruct((S, D_CONV), F32),
                   jax.ShapeDtypeStruct((S, D_ATTN), BF16),
                   jax.ShapeDtypeStruct((nm, tm // MOBA_BLOCK, D_ATTN), F32),
                   jax.ShapeDtypeStruct((nm, D_ATTN, tm), BF16),
                   jax.ShapeDtypeStruct((S // GROUP_KEYS, N_HEADS * V_ROWS, GROUP_KEYS), BF16),
                   jax.ShapeDtypeStruct((S, 2 * D_MODEL), BF16)],
        scratch_shapes=[pltpu.VMEM((tm, D), BF16)],
        compiler_params=_params(("parallel", "arbitrary")),
        name="inproj",
    )(x, g, w_in, w_in)
    return glu, k, km.reshape(S // MOBA_BLOCK, D_ATTN), q_t, v_t, gates


def _conv_kernel(halo_ref, x_ref, w_ref, b_ref, lg_ref, lb_ref, o_ref, win_ref, y_ref,
                 *, rows, lanes):
    tm, C = x_ref.shape
    i = pl.program_id(0)
    halo = halo_ref[...]
    win_ref[0:CONV_HALO, :] = jnp.where(i > 0, halo, jnp.zeros_like(halo))
    win_ref[CONV_HALO:, :] = x_ref[...]
    first = CONV_HALO - (CONV_K - 1)

    for c0 in range(0, C, lanes):
        def body(rc, carry):
            r = pl.multiple_of(rc * rows, rows)
            acc = jnp.zeros((rows, lanes), F32) + b_ref[:, c0:c0 + lanes]
            win = win_ref[pl.ds(r, rows + CONV_HALO), c0:c0 + lanes]
            for sub in range(SUBLANES):
                shifted = win if sub == 0 else pltpu.roll(win, win.shape[0] - sub, 0)
                usable = win.shape[0] - (SUBLANES if sub else 0)
                for base in range(0, usable - rows + 1, SUBLANES):
                    k = base + sub - first
                    if 0 <= k < CONV_K:
                        acc = acc + shifted[base:base + rows] * w_ref[k:k + 1, c0:c0 + lanes]
            y_ref[pl.ds(r, rows), c0:c0 + lanes] = acc
            return carry
        lax.fori_loop(0, tm // rows, body, 0)

    def ln_body(rc, carry):
        r = pl.multiple_of(rc * NORM_ROWS, NORM_ROWS)
        v = y_ref[pl.ds(r, NORM_ROWS), :]
        mu = jnp.mean(v, axis=-1, keepdims=True)
        var = jnp.mean(jnp.square(v - mu), axis=-1, keepdims=True)
        z = (v - mu) * lax.rsqrt(var + EPS) * lg_ref[...] + lb_ref[...]
        o_ref[pl.ds(r, NORM_ROWS), :] = jax.nn.silu(z).astype(o_ref.dtype)
        return carry
    lax.fori_loop(0, tm // NORM_ROWS, ln_body, 0, unroll=NORM_UNROLL)


def _conv_branch(glu, w, b, lg, lb, tm=512):
    S, C = glu.shape
    per = tm // CONV_HALO
    return pl.pallas_call(
        functools.partial(_conv_kernel, rows=128, lanes=128),
        grid=(S // tm,),
        in_specs=[pl.BlockSpec((CONV_HALO, C), lambda i: (jnp.maximum(i * per - 1, 0), 0)),
                  pl.BlockSpec((tm, C), lambda i: (i, 0)),
                  pl.BlockSpec((CONV_K, C), lambda i: (0, 0)),
                  pl.BlockSpec((1, C), lambda i: (0, 0)),
                  pl.BlockSpec((1, C), lambda i: (0, 0)),
                  pl.BlockSpec((1, C), lambda i: (0, 0))],
        out_specs=pl.BlockSpec((tm, C), lambda i: (i, 0)),
        out_shape=jax.ShapeDtypeStruct((S, C), BF16),
        scratch_shapes=[pltpu.VMEM((tm + CONV_HALO, C), F32), pltpu.VMEM((tm, C), F32)],
        compiler_params=_params(("parallel",)),
        name="conv_branch",
    )(glu, glu, w, b, lg, lb)


def _bias_kernel(rb_ref, o_ref):
    delta = pl.program_id(0)
    shape = (BIAS_ROWS, MOBA_BLOCK)

    def body(c, carry):
        c0 = pl.multiple_of(c * BIAS_ROWS, BIAS_ROWS)
        key = c0 + lax.broadcasted_iota(jnp.int32, shape, 0)
        d = delta * MOBA_BLOCK + lax.broadcasted_iota(jnp.int32, shape, 1) - key
        n = jnp.maximum(d, 0)
        max_exact = N_BUCKETS // 2
        nf = jnp.maximum(n, 1).astype(F32)
        large = max_exact + (jnp.log(nf / max_exact) / math.log(REL_MAX_DIST / max_exact)
                             * (N_BUCKETS - max_exact)).astype(jnp.int32)
        large = jnp.minimum(large, N_BUCKETS - 1)
        bucket = jnp.where(n < max_exact, n, large)
        vals = [jnp.zeros(shape, F32) for _ in range(N_HEADS)]
        for b in range(N_BUCKETS):
            hit = bucket == b
            vals = [jnp.where(hit, rb_ref[b, h], vals[h]) for h in range(N_HEADS)]
        for h in range(N_HEADS):
            val = jnp.where(delta == N_BIAS_TILES - 1, 0.0, vals[h] * LOG2E)
            o_ref[h, 0, pl.ds(c0, BIAS_ROWS), :] = jnp.where(d >= 0, val, NEG)
        return carry
    lax.fori_loop(0, MOBA_BLOCK // BIAS_ROWS, body, 0)


def _bias_tiles(rel_bias):
    return pl.pallas_call(
        _bias_kernel,
        grid=(N_BIAS_TILES,),
        in_specs=[pl.BlockSpec(memory_space=pltpu.SMEM)],
        out_specs=pl.BlockSpec((N_HEADS, 1, MOBA_BLOCK, MOBA_BLOCK), lambda d: (0, d, 0, 0)),
        out_shape=jax.ShapeDtypeStruct((N_HEADS, N_BIAS_TILES, MOBA_BLOCK, MOBA_BLOCK), F32),
        compiler_params=_params(("parallel",)),
        name="t5_bias_tiles",
    )(rel_bias)


def _attn_kernel(rb_ref, qt_ref, k_ref, vt_ref, km_ref, ind_ref, bias_ref, o_ref,
                 acc_ref, sa_ref, sb_ref, pa_ref, pb_ref):
    nb = km_ref.shape[0]
    tq = qt_ref.shape[2]
    q_blocks = tq // MOBA_BLOCK
    first_q_block = pl.program_id(1) * q_blocks
    scale = HEAD_DIM ** -0.5
    heads = range(ATTN_HEADS)
    lanes = [slice(hh * HEAD_DIM, (hh + 1) * HEAD_DIM) for hh in heads]

    def widened_query(hh):
        qf = qt_ref[0, lanes[hh], :].astype(F32)
        gate = jnp.dot(km_ref[:, lanes[hh]], qf, precision=lax.Precision.HIGHEST, preferred_element_type=F32)
        row = lax.broadcasted_iota(jnp.int32, (nb, tq), 0)
        own = first_q_block + lax.broadcasted_iota(jnp.int32, (nb, tq), 1) // MOBA_BLOCK
        valid = row < own
        g = jnp.where(valid, gate, NEG)
        sel = jnp.zeros((nb, tq), jnp.bool_)
        for _ in range(MOBA_TOPK):
            top = jnp.max(g, axis=0, keepdims=True)
            idx = jnp.min(jnp.where(g == top, row, nb), axis=0, keepdims=True)
            hit = row == idx
            sel = jnp.logical_or(sel, hit)
            g = jnp.where(hit, -jnp.inf, g)
        picked = jnp.logical_and(sel, valid)
        far = jnp.logical_and(picked, row <= own - (N_BIAS_TILES - 1))
        far_bias = rb_ref[N_BUCKETS - 1, pl.program_id(0) * ATTN_HEADS + hh] * LOG2E
        pen = jnp.where(row == own, 0.0, jnp.where(picked, jnp.where(far, far_bias, 0.0), NEG))
        pen_hi = pen.astype(BF16)
        pen_lo = jnp.where(far, pen - pen_hi.astype(F32), 0.0).astype(BF16)
        return jnp.concatenate(
            [(qf * (scale * LOG2E)).astype(BF16), pen_hi, pen_lo, jnp.zeros((HEAD_DIM - 2 * nb, tq), BF16)],
            axis=0)

    q_aug = [widened_query(hh) for hh in heads]
    n_groups = (first_q_block + q_blocks - 1) // KV_GROUP + 1

    def scores(grp, dst_ref):
        r0 = pl.multiple_of(grp * GROUP_KEYS, GROUP_KEYS)
        ind = ind_ref[pl.ds(r0, GROUP_KEYS), :]
        for hh in heads:
            k_aug = jnp.concatenate([k_ref[pl.ds(r0, GROUP_KEYS), lanes[hh]], ind], axis=1)
            dst_ref[hh] = jnp.dot(k_aug, q_aug[hh], preferred_element_type=F32)

    def softmax(grp, src_ref, dst_ref, m_prev):
        m_out, alpha_out = [], []
        for hh in heads:
            parts = []
            for u in range(KV_GROUP):
                keys = slice(u * MOBA_BLOCK, (u + 1) * MOBA_BLOCK)
                across = []
                for qb in range(q_blocks):
                    delta = jnp.clip(first_q_block + qb - (grp * KV_GROUP + u), 0, N_BIAS_TILES - 1)
                    across.append(src_ref[hh, keys, qb * MOBA_BLOCK:(qb + 1) * MOBA_BLOCK] + bias_ref[hh, delta])
                parts.append(jnp.concatenate(across, axis=1))
            s = jnp.concatenate(parts, axis=0)
            m_new = jnp.maximum(m_prev[hh], jnp.max(s, axis=0, keepdims=True))
            dst_ref[hh] = jnp.exp2(s - m_new).astype(BF16)
            m_out.append(m_new)
            alpha_out.append(jnp.exp2(m_prev[hh] - m_new))
        return tuple(m_out), tuple(alpha_out)

    def weighted_values(grp, live, src_ref, alpha):
        for hh in heads:
            pv = jnp.dot(vt_ref[grp, hh * V_ROWS:(hh + 1) * V_ROWS, :], src_ref[hh], preferred_element_type=F32)
            acc_ref[hh] = alpha[hh] * acc_ref[hh] + jnp.where(live, pv, 0.0)

    acc_ref[...] = jnp.zeros_like(acc_ref)
    pb_ref[...] = jnp.zeros_like(pb_ref)
    scores(n_groups - 1, sa_ref)

    def body(u, carry):
        m_prev, alpha_b, grp_b_prev = carry
        grp_a = n_groups - 1 - 2 * u
        grp_b = grp_a - 1
        weighted_values(jnp.maximum(grp_b_prev, 0), grp_b_prev >= 0, pb_ref, alpha_b)
        m_a, alpha_a = softmax(grp_a, sa_ref, pa_ref, m_prev)
        scores(jnp.maximum(grp_b, 0), sb_ref)
        weighted_values(grp_a, True, pa_ref, alpha_a)
        m_b, alpha_b = softmax(jnp.maximum(grp_b, 0), sb_ref, pb_ref, m_a)
        scores(jnp.maximum(grp_b - 1, 0), sa_ref)
        return m_b, alpha_b, grp_b

    m0 = tuple(jnp.full((1, tq), NEG, F32) for _ in heads)
    a0 = tuple(jnp.ones((1, tq), F32) for _ in heads)
    _, alpha_b, grp_b = lax.fori_loop(0, (n_groups + 1) // 2, body, (m0, a0, jnp.int32(-1)))
    weighted_values(jnp.maximum(grp_b, 0), grp_b >= 0, pb_ref, alpha_b)
    for hh in heads:
        acc = acc_ref[hh]
        o_ref[:, lanes[hh]] = (acc[:HEAD_DIM] / acc[HEAD_DIM:HEAD_DIM + 1]).T.astype(o_ref.dtype)


def _attention(k, km, q_t, v_t, bias, rel_bias):
    S = k.shape[0]
    nb = S // MOBA_BLOCK
    tq = ATTN_Q_BLOCKS * MOBA_BLOCK
    per = INPROJ_ROWS // tq
    width = ATTN_HEADS * HEAD_DIM
    key_blk = jnp.arange(S, dtype=jnp.int32)[:, None] // MOBA_BLOCK
    lane = jnp.arange(HEAD_DIM, dtype=jnp.int32)[None, :]
    ind = jnp.logical_and(lane < 2 * nb, lane % nb == key_blk).astype(BF16)
    return pl.pallas_call(
        _attn_kernel,
        grid=(N_HEADS // ATTN_HEADS, S // tq),
        in_specs=[pl.BlockSpec(memory_space=pltpu.SMEM),
                  pl.BlockSpec((1, width, tq), lambda h, i: (i // per, h, i % per)),
                  pl.BlockSpec((S, width), lambda h, i: (0, h)),
                  pl.BlockSpec((S // GROUP_KEYS, ATTN_HEADS * V_ROWS, GROUP_KEYS), lambda h, i: (0, h, 0)),
                  pl.BlockSpec((nb, width), lambda h, i: (0, h)),
                  pl.BlockSpec((S, HEAD_DIM), lambda h, i: (0, 0)),
                  pl.BlockSpec((ATTN_HEADS, N_BIAS_TILES, MOBA_BLOCK, MOBA_BLOCK), lambda h, i: (h, 0, 0, 0))],
        out_specs=pl.BlockSpec((tq, width), lambda h, i: (i, h)),
        out_shape=jax.ShapeDtypeStruct((S, D_ATTN), BF16),
        scratch_shapes=[pltpu.VMEM((ATTN_HEADS, HEAD_DIM + SUM_ROWS, tq), F32),
                        pltpu.VMEM((ATTN_HEADS, GROUP_KEYS, tq), F32),
                        pltpu.VMEM((ATTN_HEADS, GROUP_KEYS, tq), F32),
                        pltpu.VMEM((ATTN_HEADS, GROUP_KEYS, tq), BF16),
                        pltpu.VMEM((ATTN_HEADS, GROUP_KEYS, tq), BF16)],
        compiler_params=_params(("parallel", "parallel")),
        name="moba_attention",
    )(rel_bias, q_t, k, v_t, km, ind, bias)


def _mix_kernel(cs_ref, at_ref, gate_ref, x_ref, wc_ref, wa_ref, wo_ref, o_ref):
    D = x_ref.shape[1]
    yc = jnp.dot(cs_ref[...], wc_ref[0], preferred_element_type=F32)
    ya = jnp.dot(at_ref[...], wa_ref[0], preferred_element_type=F32)
    merged = gate_ref[:, :D].astype(F32) * yc + gate_ref[:, D:].astype(F32) * ya
    o_ref[...] = x_ref[...] + jnp.dot(merged.astype(BF16), wo_ref[0], preferred_element_type=F32)


def _mix(cs, attn, gates, x, wc, wa, wo, layer, tm=512):
    S, D = x.shape
    full = lambda shape: pl.BlockSpec(shape, lambda i: (layer, 0, 0), pipeline_mode=pl.Buffered(1))
    return pl.pallas_call(
        _mix_kernel,
        grid=(S // tm,),
        in_specs=[pl.BlockSpec((tm, D_CONV), lambda i: (i, 0)),
                  pl.BlockSpec((tm, D_ATTN), lambda i: (i, 0)),
                  pl.BlockSpec((tm, 2 * D), lambda i: (i, 0)),
                  pl.BlockSpec((tm, D), lambda i: (i, 0)),
                  full((1, D_CONV, D)), full((1, D_ATTN, D)), full((1, D, D))],
        out_specs=pl.BlockSpec((tm, D), lambda i: (i, 0)),
        out_shape=jax.ShapeDtypeStruct((S, D), F32),
        compiler_params=_params(("parallel",)),
        name="mix_out_proj",
    )(cs, attn, gates, x, wc, wa, wo)


def _ffn_kernel(xh_ref, x_ref, g_ref, wa_ref, wb_ref, dw_ref, b_ref, wd_ref,
                fg_ref, o_ref, h_ref, gate_ref, *, final_norm, n_tiles, up_chunk):
    i = pl.program_id(0)
    f = pl.program_id(1)
    tm = x_ref.shape[0]
    tf = wa_ref.shape[2]
    slot = f % 2

    def up_stage():
        h = h_ref[...]
        tile = jnp.minimum(f, n_tiles - 1)
        for c0 in range(0, tf, up_chunk):
            cols = slice(c0, c0 + up_chunk)

            def up_conv(w_ref, t):
                u = jnp.dot(h, w_ref[0, :, cols], preferred_element_type=F32)
                y = (u * dw_ref[t, 2:3, cols] + pltpu.roll(u, 1, 0) * dw_ref[t, 1:2, cols]
                     + pltpu.roll(u, 2, 0) * dw_ref[t, 0:1, cols] + b_ref[t, :, cols])
                return y[HALO:]

            ua = up_conv(wa_ref, tile)
            ub = up_conv(wb_ref, tile + n_tiles)
            gate_ref[slot, :, cols] = (jax.nn.silu(ub) * ua).astype(BF16)

    def down_product():
        return jnp.dot(gate_ref[1 - slot], wd_ref[0], preferred_element_type=F32)

    @pl.when(f == 0)
    def _():
        xh = xh_ref[...]
        ms = jnp.mean(xh * xh, axis=-1, keepdims=True)
        hh = xh * lax.rsqrt(ms + EPS) * g_ref[...]
        h_ref[0:HALO, :] = jnp.where(i > 0, hh, jnp.zeros_like(hh)).astype(BF16)
        _rmsnorm_rows(x_ref, g_ref, h_ref, tm, dst_off=HALO)
        o_ref[...] = jnp.zeros_like(o_ref)
        up_stage()

    @pl.when(jnp.logical_and(f > 0, f < n_tiles))
    def _():
        o_ref[...] += down_product()
        up_stage()

    @pl.when(f == n_tiles)
    def _():
        y = x_ref[...] + o_ref[...] + down_product()
        if final_norm:
            ms = jnp.mean(y * y, axis=-1, keepdims=True)
            y = y * lax.rsqrt(ms + EPS) * fg_ref[...]
        o_ref[...] = y


def _ffn(x, g, w_up, dw, dwb, w_down, layer, final_g, final_norm, tm=1024, tf=W_TILE):
    S, D = x.shape
    nf = D_FF // tf
    per = tm // HALO
    dw_tiles = dw.reshape(FFN_CONV_K, 2 * nf, tf).transpose(1, 0, 2)
    b_tiles = dwb.reshape(2 * nf, 1, tf)
    row = lambda shape: pl.BlockSpec(shape, lambda i, f: (0, 0))
    up = lambda f: jnp.minimum(f, nf - 1)
    down = lambda f: jnp.maximum(f - 1, 0)
    return pl.pallas_call(
        functools.partial(_ffn_kernel, final_norm=final_norm, n_tiles=nf, up_chunk=tf),
        grid=(S // tm, nf + 1),
        in_specs=[pl.BlockSpec((HALO, D), lambda i, f: (jnp.maximum(i * per - 1, 0), 0)),
                  pl.BlockSpec((tm, D), lambda i, f: (i, 0), pipeline_mode=pl.Buffered(1)),
                  row((1, D)),
                  pl.BlockSpec((1, D, tf), lambda i, f: (layer, 0, up(f))),
                  pl.BlockSpec((1, D, tf), lambda i, f: (layer, 0, up(f) + nf)),
                  pl.BlockSpec((2 * nf, FFN_CONV_K, tf), lambda i, f: (0, 0, 0)),
                  pl.BlockSpec((2 * nf, 1, tf), lambda i, f: (0, 0, 0)),
                  pl.BlockSpec((1, tf, D), lambda i, f: (layer, down(f), 0)),
                  row((1, D))],
        out_specs=pl.BlockSpec((tm, D), lambda i, f: (i, 0)),
        out_shape=jax.ShapeDtypeStruct((S, D), F32),
        scratch_shapes=[pltpu.VMEM((HALO + tm, D), BF16), pltpu.VMEM((2, tm, tf), BF16)],
        compiler_params=_params(("parallel", "arbitrary")),
        name="conv_ffn",
    )(x, x, g, w_up, w_up, dw_tiles, b_tiles, w_down, final_g)


def kernel(x, norm1_g, w_in, conv_dw, conv_dw_b, conv_ln_g, conv_ln_b, w_conv_out, rel_bias,
           w_attn_out, w_out, norm2_g, w_up, ffn_dw, ffn_dw_b, w_down, final_g):
    B, S, D = x.shape
    assert D == D_MODEL and S % INPROJ_ROWS == 0 and INPROJ_ROWS % GROUP_KEYS == 0
    depth = w_in.shape[0]
    bias = _bias_tiles(rel_bias)
    row = lambda v: v.reshape(1, -1)
    w_in_bf = w_in.astype(BF16)
    w_conv_bf, w_attn_bf, w_out_bf = (w.astype(BF16) for w in (w_conv_out, w_attn_out, w_out))
    w_up_bf, w_down_bf = w_up.astype(BF16), w_down.astype(BF16)
    outs = []
    for b in range(B):
        xb = x.reshape(S, D) if B == 1 else x[b]
        for l in range(depth):
            glu, k, km, q_t, v_t, gates = _inproj_fused(xb, row(norm1_g[l]), w_in_bf, l)
            cs = _conv_branch(glu, conv_dw[l], row(conv_dw_b[l]), row(conv_ln_g[l]), row(conv_ln_b[l]))
            attn = _attention(k, km, q_t, v_t, bias, rel_bias)
            xb = _mix(cs, attn, gates, xb, w_conv_bf, w_attn_bf, w_out_bf, l)
            xb = _ffn(xb, row(norm2_g[l]), w_up_bf, ffn_dw[l], row(ffn_dw_b[l]),
                      w_down_bf, l, row(final_g), final_norm=(l == depth - 1))
        outs.append(xb)
    return outs[0].reshape(1, S, D) if B == 1 else jnp.stack(outs, axis=0)
```

```python
import functools
import math

import jax
import jax.numpy as jnp
from jax import lax
from jax.experimental import pallas as pl
from jax.experimental.pallas import tpu as pltpu

F32 = jnp.float32
BF16 = jnp.bfloat16

D_MODEL = 2048
D_CONV = 1024
CONV_K = 31
N_HEADS = 8
HEAD_DIM = 128
D_ATTN = N_HEADS * HEAD_DIM
MOBA_BLOCK = 256
MOBA_TOPK = 3
N_BUCKETS = 32
REL_MAX_DIST = 2048
D_FF = 5632
FFN_CONV_K = 3
EPS = 1e-6
NEG = -1e30
LOG2E = math.log2(math.e)

N_BIAS_TILES = 8
BIAS_ROWS = 16
KV_GROUP = 2
GROUP_KEYS = KV_GROUP * MOBA_BLOCK
INPROJ_ROWS = 1024
SUM_ROWS = 16
V_ROWS = HEAD_DIM + SUM_ROWS
ATTN_HEADS = 2
ATTN_Q_BLOCKS = 4

V7X_VMEM_BYTES = 64 * 1024 * 1024
VMEM_LIMIT = V7X_VMEM_BYTES - 8 * 1024 * 1024

SUBLANES = 8
NORM_ROWS = 16
NORM_UNROLL = 8
HALO = 8
CONV_HALO = 32

W_TILE = 512

TN_DIMS = (((0,), (1,)), ((), ()))


def _params(sem):
    return pltpu.CompilerParams(dimension_semantics=sem, vmem_limit_bytes=VMEM_LIMIT)


def _rmsnorm_rows(x_ref, g_ref, dst_ref, n_rows, dst_off=0, out_dtype=BF16):
    def body(c, carry):
        r = pl.multiple_of(c * NORM_ROWS, NORM_ROWS)
        xv = x_ref[pl.ds(r, NORM_ROWS), :]
        ms = jnp.mean(xv * xv, axis=-1, keepdims=True)
        y = xv * lax.rsqrt(ms + EPS) * g_ref[...]
        dst_ref[pl.ds(dst_off + r, NORM_ROWS), :] = y.astype(out_dtype)
        return carry
    lax.fori_loop(0, n_rows // NORM_ROWS, body, 0, unroll=NORM_UNROLL)


def _inproj_kernel(x_ref, g_ref, w_ref, w2_ref, glu_ref, k_ref, km_ref, q_ref, v_ref, gate_ref,
                   h_ref, *, seg):
    j = pl.program_id(1)
    n_glu, n_k, n_q, n_v, _ = seg
    k0 = n_glu
    q0 = k0 + n_k
    v0 = q0 + n_q
    g0 = v0 + n_v

    @pl.when(j == 0)
    def _():
        _rmsnorm_rows(x_ref, g_ref, h_ref, x_ref.shape[0])

    @pl.when(j < k0)
    def _():
        h = h_ref[...]
        a = jnp.dot(h, w_ref[0], preferred_element_type=F32)
        b = jnp.dot(h, w2_ref[0], preferred_element_type=F32)
        glu_ref[...] = a * jax.nn.sigmoid(b)

    @pl.when(jnp.logical_and(j >= k0, j < q0))
    def _():
        acc = jnp.dot(h_ref[...], w_ref[0], preferred_element_type=F32)
        k_ref[...] = acc.astype(k_ref.dtype)
        for b in range(acc.shape[0] // MOBA_BLOCK):
            blk = acc[b * MOBA_BLOCK:(b + 1) * MOBA_BLOCK]
            km_ref[0, b:b + 1, :] = jnp.mean(blk, axis=0, keepdims=True)

    @pl.when(jnp.logical_and(j >= q0, j < v0))
    def _():
        acc = lax.dot_general(w_ref[0], h_ref[...], TN_DIMS, preferred_element_type=F32)
        q_ref[0] = acc.astype(q_ref.dtype)

    @pl.when(jnp.logical_and(j >= v0, j < g0))
    def _():
        acc = lax.dot_general(w_ref[0], h_ref[...], TN_DIMS, preferred_element_type=F32)
        for grp in range(v_ref.shape[0]):
            pos = slice(grp * GROUP_KEYS, (grp + 1) * GROUP_KEYS)
            for hh in range(acc.shape[0] // HEAD_DIM):
                r = hh * V_ROWS
                v_ref[grp, r:r + HEAD_DIM, :] = acc[hh * HEAD_DIM:(hh + 1) * HEAD_DIM, pos].astype(v_ref.dtype)
                v_ref[grp, r + HEAD_DIM:r + V_ROWS, :] = jnp.ones((SUM_ROWS, GROUP_KEYS), v_ref.dtype)

    @pl.when(j >= g0)
    def _():
        acc = jnp.dot(h_ref[...], w_ref[0], preferred_element_type=F32)
        gate_ref[...] = jax.nn.sigmoid(acc).astype(gate_ref.dtype)


def _inproj_fused(x, g, w_in, layer, tn=W_TILE):
    S, D = x.shape
    tm = INPROJ_ROWS
    nm = S // tm
    groups = tm // GROUP_KEYS
    seg = (D_CONV // tn, D_ATTN // tn, D_ATTN // tn, D_ATTN // tn, 2 * D_MODEL // tn)
    n_glu, n_k, n_q, n_v, n_g = seg
    k0, q0 = n_glu, n_glu + n_k
    v0 = q0 + n_q
    g0 = v0 + n_v
    col_q = 2 * D_CONV // tn
    col_k = col_q + n_q

    def w_col(j):
        return jnp.where(j < k0, j, jnp.where(j < q0, col_k + (j - k0),
                         jnp.where(j < v0, col_q + (j - q0), col_k + n_k + (j - v0))))

    seg_tile = lambda j, start, n: jnp.clip(j - start, 0, n - 1)
    v_tile_rows = tn // HEAD_DIM * V_ROWS
    glu, k, km, q_t, v_t, gates = pl.pallas_call(
        functools.partial(_inproj_kernel, seg=seg),
        grid=(nm, sum(seg)),
        in_specs=[pl.BlockSpec((tm, D), lambda i, j: (i, 0)),
                  pl.BlockSpec((1, D), lambda i, j: (0, 0)),
                  pl.BlockSpec((1, D, tn), lambda i, j: (layer, 0, w_col(j))),
                  pl.BlockSpec((1, D, tn), lambda i, j: (layer, 0, D_CONV // tn + seg_tile(j, 0, n_glu)))],
        out_specs=[pl.BlockSpec((tm, tn), lambda i, j: (i, seg_tile(j, 0, n_glu))),
                   pl.BlockSpec((tm, tn), lambda i, j: (i, seg_tile(j, k0, n_k))),
                   pl.BlockSpec((1, tm // MOBA_BLOCK, tn), lambda i, j: (i, 0, seg_tile(j, k0, n_k))),
                   pl.BlockSpec((1, tn, tm), lambda i, j: (i, seg_tile(j, q0, n_q), 0)),
                   pl.BlockSpec((groups, v_tile_rows, GROUP_KEYS), lambda i, j: (i, seg_tile(j, v0, n_v), 0)),
                   pl.BlockSpec((tm, tn), lambda i, j: (i, seg_tile(j, g0, n_g)))],
        out_shape=[jax.ShapeDtypeStruct((S, D_CONV), F32),
                   jax.ShapeDtypeStruct((S, D_ATTN), BF16),
                   jax.ShapeDtypeStruct((nm, tm // MOBA_BLOCK, D_ATTN), F32),
                   jax.ShapeDtypeStruct((nm, D_ATTN, tm), BF16),
                   jax.ShapeDtypeStruct((S // GROUP_KEYS, N_HEADS * V_ROWS, GROUP_KEYS), BF16),
                   jax.ShapeDtypeStruct((S, 2 * D_MODEL), BF16)],
        scratch_shapes=[pltpu.VMEM((tm, D), BF16)],
        compiler_params=_params(("parallel", "arbitrary")),
        name="inproj",
    )(x, g, w_in, w_in)
    return glu, k, km.reshape(S // MOBA_BLOCK, D_ATTN), q_t, v_t, gates


def _conv_kernel(halo_ref, x_ref, w_ref, b_ref, lg_ref, lb_ref, o_ref, win_ref, y_ref,
                 *, rows, lanes):
    tm, C = x_ref.shape
    i = pl.program_id(0)
    halo = halo_ref[...]
    win_ref[0:CONV_HALO, :] = jnp.where(i > 0, halo, jnp.zeros_like(halo))
    win_ref[CONV_HALO:, :] = x_ref[...]
    first = CONV_HALO - (CONV_K - 1)

    for c0 in range(0, C, lanes):
        def body(rc, carry):
            r = pl.multiple_of(rc * rows, rows)
            acc = jnp.zeros((rows, lanes), F32) + b_ref[:, c0:c0 + lanes]
            win = win_ref[pl.ds(r, rows + CONV_HALO), c0:c0 + lanes]
            for sub in range(SUBLANES):
                shifted = win if sub == 0 else pltpu.roll(win, win.shape[0] - sub, 0)
                usable = win.shape[0] - (SUBLANES if sub else 0)
                for base in range(0, usable - rows + 1, SUBLANES):
                    k = base + sub - first
                    if 0 <= k < CONV_K:
                        acc = acc + shifted[base:base + rows] * w_ref[k:k + 1, c0:c0 + lanes]
            y_ref[pl.ds(r, rows), c0:c0 + lanes] = acc
            return carry
        lax.fori_loop(0, tm // rows, body, 0)

    def ln_body(rc, carry):
        r = pl.multiple_of(rc * NORM_ROWS, NORM_ROWS)
        v = y_ref[pl.ds(r, NORM_ROWS), :]
        mu = jnp.mean(v, axis=-1, keepdims=True)
        var = jnp.mean(jnp.square(v - mu), axis=-1, keepdims=True)
        z = (v - mu) * lax.rsqrt(var + EPS) * lg_ref[...] + lb_ref[...]
        o_ref[pl.ds(r, NORM_ROWS), :] = jax.nn.silu(z).astype(o_ref.dtype)
        return carry
    lax.fori_loop(0, tm // NORM_ROWS, ln_body, 0, unroll=NORM_UNROLL)


def _conv_branch(glu, w, b, lg, lb, tm=512):
    S, C = glu.shape
    per = tm // CONV_HALO
    return pl.pallas_call(
        functools.partial(_conv_kernel, rows=128, lanes=128),
        grid=(S // tm,),
        in_specs=[pl.BlockSpec((CONV_HALO, C), lambda i: (jnp.maximum(i * per - 1, 0), 0)),
                  pl.BlockSpec((tm, C), lambda i: (i, 0)),
                  pl.BlockSpec((CONV_K, C), lambda i: (0, 0)),
                  pl.BlockSpec((1, C), lambda i: (0, 0)),
                  pl.BlockSpec((1, C), lambda i: (0, 0)),
                  pl.BlockSpec((1, C), lambda i: (0, 0))],
        out_specs=pl.BlockSpec((tm, C), lambda i: (i, 0)),
        out_shape=jax.ShapeDtypeStruct((S, C), BF16),
        scratch_shapes=[pltpu.VMEM((tm + CONV_HALO, C), F32), pltpu.VMEM((tm, C), F32)],
        compiler_params=_params(("parallel",)),
        name="conv_branch",
    )(glu, glu, w, b, lg, lb)


def _bias_kernel(rb_ref, o_ref):
    delta = pl.program_id(0)
    shape = (BIAS_ROWS, MOBA_BLOCK)

    def body(c, carry):
        c0 = pl.multiple_of(c * BIAS_ROWS, BIAS_ROWS)
        key = c0 + lax.broadcasted_iota(jnp.int32, shape, 0)
        d = delta * MOBA_BLOCK + lax.broadcasted_iota(jnp.int32, shape, 1) - key
        n = jnp.maximum(d, 0)
        max_exact = N_BUCKETS // 2
        nf = jnp.maximum(n, 1).astype(F32)
        large = max_exact + (jnp.log(nf / max_exact) / math.log(REL_MAX_DIST / max_exact)
                             * (N_BUCKETS - max_exact)).astype(jnp.int32)
        large = jnp.minimum(large, N_BUCKETS - 1)
        bucket = jnp.where(n < max_exact, n, large)
        vals = [jnp.zeros(shape, F32) for _ in range(N_HEADS)]
        for b in range(N_BUCKETS):
            hit = bucket == b
            vals = [jnp.where(hit, rb_ref[b, h], vals[h]) for h in range(N_HEADS)]
        for h in range(N_HEADS):
            val = jnp.where(delta == N_BIAS_TILES - 1, 0.0, vals[h] * LOG2E)
            o_ref[h, 0, pl.ds(c0, BIAS_ROWS), :] = jnp.where(d >= 0, val, NEG)
        return carry
    lax.fori_loop(0, MOBA_BLOCK // BIAS_ROWS, body, 0)


def _bias_tiles(rel_bias):
    return pl.pallas_call(
        _bias_kernel,
        grid=(N_BIAS_TILES,),
        in_specs=[pl.BlockSpec(memory_space=pltpu.SMEM)],
        out_specs=pl.BlockSpec((N_HEADS, 1, MOBA_BLOCK, MOBA_BLOCK), lambda d: (0, d, 0, 0)),
        out_shape=jax.ShapeDtypeStruct((N_HEADS, N_BIAS_TILES, MOBA_BLOCK, MOBA_BLOCK), F32),
        compiler_params=_params(("parallel",)),
        name="t5_bias_tiles",
    )(rel_bias)


def _attn_kernel(rb_ref, qt_ref, k_ref, vt_ref, km_ref, ind_ref, bias_ref, o_ref,
                 acc_ref, sa_ref, sb_ref, pa_ref, pb_ref):
    nb = km_ref.shape[0]
    tq = qt_ref.shape[2]
    q_blocks = tq // MOBA_BLOCK
    first_q_block = pl.program_id(1) * q_blocks
    scale = HEAD_DIM ** -0.5
    heads = range(ATTN_HEADS)
    lanes = [slice(hh * HEAD_DIM, (hh + 1) * HEAD_DIM) for hh in heads]

    def widened_query(hh):
        qf = qt_ref[0, lanes[hh], :].astype(F32)
        gate = jnp.dot(km_ref[:, lanes[hh]], qf, precision=lax.Precision.HIGHEST, preferred_element_type=F32)
        row = lax.broadcasted_iota(jnp.int32, (nb, tq), 0)
        own = first_q_block + lax.broadcasted_iota(jnp.int32, (nb, tq), 1) // MOBA_BLOCK
        valid = row < own
        g = jnp.where(valid, gate, NEG)
        sel = jnp.zeros((nb, tq), jnp.bool_)
        for _ in range(MOBA_TOPK):
            top = jnp.max(g, axis=0, keepdims=True)
            idx = jnp.min(jnp.where(g == top, row, nb), axis=0, keepdims=True)
            hit = row == idx
            sel = jnp.logical_or(sel, hit)
            g = jnp.where(hit, -jnp.inf, g)
        picked = jnp.logical_and(sel, valid)
        far = jnp.logical_and(picked, row <= own - (N_BIAS_TILES - 1))
        far_bias = rb_ref[N_BUCKETS - 1, pl.program_id(0) * ATTN_HEADS + hh] * LOG2E
        pen = jnp.where(row == own, 0.0, jnp.where(picked, jnp.where(far, far_bias, 0.0), NEG))
        pen_hi = pen.astype(BF16)
        pen_lo = jnp.where(far, pen - pen_hi.astype(F32), 0.0).astype(BF16)
        return jnp.concatenate(
            [(qf * (scale * LOG2E)).astype(BF16), pen_hi, pen_lo, jnp.zeros((HEAD_DIM - 2 * nb, tq), BF16)],
            axis=0)

    q_aug = [widened_query(hh) for hh in heads]
    n_groups = (first_q_block + q_blocks - 1) // KV_GROUP + 1

    def scores(grp, dst_ref):
        r0 = pl.multiple_of(grp * GROUP_KEYS, GROUP_KEYS)
        ind = ind_ref[pl.ds(r0, GROUP_KEYS), :]
        for hh in heads:
            k_aug = jnp.concatenate([k_ref[pl.ds(r0, GROUP_KEYS), lanes[hh]], ind], axis=1)
            dst_ref[hh] = jnp.dot(k_aug, q_aug[hh], preferred_element_type=F32)

    def softmax(grp, src_ref, dst_ref, m_prev):
        m_out, alpha_out = [], []
        for hh in heads:
            parts = []
            for u in range(KV_GROUP):
                keys = slice(u * MOBA_BLOCK, (u + 1) * MOBA_BLOCK)
                across = []
                for qb in range(q_blocks):
                    delta = jnp.clip(first_q_block + qb - (grp * KV_GROUP + u), 0, N_BIAS_TILES - 1)
                    across.append(src_ref[hh, keys, qb * MOBA_BLOCK:(qb + 1) * MOBA_BLOCK] + bias_ref[hh, delta])
                parts.append(jnp.concatenate(across, axis=1))
            s = jnp.concatenate(parts, axis=0)
            m_new = jnp.maximum(m_prev[hh], jnp.max(s, axis=0, keepdims=True))
            dst_ref[hh] = jnp.exp2(s - m_new).astype(BF16)
            m_out.append(m_new)
            alpha_out.append(jnp.exp2(m_prev[hh] - m_new))
        return tuple(m_out), tuple(alpha_out)

    def weighted_values(grp, live, src_ref, alpha):
        for hh in heads:
            pv = jnp.dot(vt_ref[grp, hh * V_ROWS:(hh + 1) * V_ROWS, :], src_ref[hh], preferred_element_type=F32)
            acc_ref[hh] = alpha[hh] * acc_ref[hh] + jnp.where(live, pv, 0.0)

    acc_ref[...] = jnp.zeros_like(acc_ref)
    pb_ref[...] = jnp.zeros_like(pb_ref)
    scores(n_groups - 1, sa_ref)

    def body(u, carry):
        m_prev, alpha_b, grp_b_prev = carry
        grp_a = n_groups - 1 - 2 * u
        grp_b = grp_a - 1
        weighted_values(jnp.maximum(grp_b_prev, 0), grp_b_prev >= 0, pb_ref, alpha_b)
        m_a, alpha_a = softmax(grp_a, sa_ref, pa_ref, m_prev)
        scores(jnp.maximum(grp_b, 0), sb_ref)
        weighted_values(grp_a, True, pa_ref, alpha_a)
        m_b, alpha_b = softmax(jnp.maximum(grp_b, 0), sb_ref, pb_ref, m_a)
        scores(jnp.maximum(grp_b - 1, 0), sa_ref)
        return m_b, alpha_b, grp_b

    m0 = tuple(jnp.full((1, tq), NEG, F32) for _ in heads)
    a0 = tuple(jnp.ones((1, tq), F32) for _ in heads)
    _, alpha_b, grp_b = lax.fori_loop(0, (n_groups + 1) // 2, body, (m0, a0, jnp.int32(-1)))
    weighted_values(jnp.maximum(grp_b, 0), grp_b >= 0, pb_ref, alpha_b)
    for hh in heads:
        acc = acc_ref[hh]
        o_ref[:, lanes[hh]] = (acc[:HEAD_DIM] / acc[HEAD_DIM:HEAD_DIM + 1]).T.astype(o_ref.dtype)


def _attention(k, km, q_t, v_t, bias, rel_bias):
    S = k.shape[0]
    nb = S // MOBA_BLOCK
    tq = ATTN_Q_BLOCKS * MOBA_BLOCK
    per = INPROJ_ROWS // tq
    width = ATTN_HEADS * HEAD_DIM
    key_blk = jnp.arange(S, dtype=jnp.int32)[:, None] // MOBA_BLOCK
    lane = jnp.arange(HEAD_DIM, dtype=jnp.int32)[None, :]
    ind = jnp.logical_and(lane < 2 * nb, lane % nb == key_blk).astype(BF16)
    return pl.pallas_call(
        _attn_kernel,
        grid=(N_HEADS // ATTN_HEADS, S // tq),
        in_specs=[pl.BlockSpec(memory_space=pltpu.SMEM),
                  pl.BlockSpec((1, width, tq), lambda h, i: (i // per, h, i % per)),
                  pl.BlockSpec((S, width), lambda h, i: (0, h)),
                  pl.BlockSpec((S // GROUP_KEYS, ATTN_HEADS * V_ROWS, GROUP_KEYS), lambda h, i: (0, h, 0)),
                  pl.BlockSpec((nb, width), lambda h, i: (0, h)),
                  pl.BlockSpec((S, HEAD_DIM), lambda h, i: (0, 0)),
                  pl.BlockSpec((ATTN_HEADS, N_BIAS_TILES, MOBA_BLOCK, MOBA_BLOCK), lambda h, i: (h, 0, 0, 0))],
        out_specs=pl.BlockSpec((tq, width), lambda h, i: (i, h)),
        out_shape=jax.ShapeDtypeStruct((S, D_ATTN), BF16),
        scratch_shapes=[pltpu.VMEM((ATTN_HEADS, HEAD_DIM + SUM_ROWS, tq), F32),
                        pltpu.VMEM((ATTN_HEADS, GROUP_KEYS, tq), F32),
                        pltpu.VMEM((ATTN_HEADS, GROUP_KEYS, tq), F32),
                        pltpu.VMEM((ATTN_HEADS, GROUP_KEYS, tq), BF16),
                        pltpu.VMEM((ATTN_HEADS, GROUP_KEYS, tq), BF16)],
        compiler_params=_params(("parallel", "parallel")),
        name="moba_attention",
    )(rel_bias, q_t, k, v_t, km, ind, bias)


def _mix_kernel(cs_ref, at_ref, gate_ref, x_ref, wc_ref, wa_ref, wo_ref, o_ref):
    D = x_ref.shape[1]
    yc = jnp.dot(cs_ref[...], wc_ref[0], preferred_element_type=F32)
    ya = jnp.dot(at_ref[...], wa_ref[0], preferred_element_type=F32)
    merged = gate_ref[:, :D].astype(F32) * yc + gate_ref[:, D:].astype(F32) * ya
    o_ref[...] = x_ref[...] + jnp.dot(merged.astype(BF16), wo_ref[0], preferred_element_type=F32)


def _mix(cs, attn, gates, x, wc, wa, wo, layer, tm=512):
    S, D = x.shape
    full = lambda shape: pl.BlockSpec(shape, lambda i: (layer, 0, 0), pipeline_mode=pl.Buffered(1))
    return pl.pallas_call(
        _mix_kernel,
        grid=(S // tm,),
        in_specs=[pl.BlockSpec((tm, D_CONV), lambda i: (i, 0)),
                  pl.BlockSpec((tm, D_ATTN), lambda i: (i, 0)),
                  pl.BlockSpec((tm, 2 * D), lambda i: (i, 0)),
                  pl.BlockSpec((tm, D), lambda i: (i, 0)),
                  full((1, D_CONV, D)), full((1, D_ATTN, D)), full((1, D, D))],
        out_specs=pl.BlockSpec((tm, D), lambda i: (i, 0)),
        out_shape=jax.ShapeDtypeStruct((S, D), F32),
        compiler_params=_params(("parallel",)),
        name="mix_out_proj",
    )(cs, attn, gates, x, wc, wa, wo)


def _ffn_kernel(xh_ref, x_ref, g_ref, wa_ref, wb_ref, dw_ref, b_ref, wd_ref,
                fg_ref, o_ref, h_ref, gate_ref, *, final_norm, n_tiles, up_chunk):
    i = pl.program_id(0)
    f = pl.program_id(1)
    tm = x_ref.shape[0]
    tf = wa_ref.shape[2]
    slot = f % 2

    def up_stage():
        h = h_ref[...]
        tile = jnp.minimum(f, n_tiles - 1)
        for c0 in range(0, tf, up_chunk):
            cols = slice(c0, c0 + up_chunk)

            def up_conv(w_ref, t):
                u = jnp.dot(h, w_ref[0, :, cols], preferred_element_type=F32)
                y = (u * dw_ref[t, 2:3, cols] + pltpu.roll(u, 1, 0) * dw_ref[t, 1:2, cols]
                     + pltpu.roll(u, 2, 0) * dw_ref[t, 0:1, cols] + b_ref[t, :, cols])
                return y[HALO:]

            ua = up_conv(wa_ref, tile)
            ub = up_conv(wb_ref, tile + n_tiles)
            gate_ref[slot, :, cols] = (jax.nn.silu(ub) * ua).astype(BF16)

    def down_product():
        return jnp.dot(gate_ref[1 - slot], wd_ref[0], preferred_element_type=F32)

    @pl.when(f == 0)
    def _():
        xh = xh_ref[...]
        ms = jnp.mean(xh * xh, axis=-1, keepdims=True)
        hh = xh * lax.rsqrt(ms + EPS) * g_ref[...]
        h_ref[0:HALO, :] = jnp.where(i > 0, hh, jnp.zeros_like(hh)).astype(BF16)
        _rmsnorm_rows(x_ref, g_ref, h_ref, tm, dst_off=HALO)
        o_ref[...] = jnp.zeros_like(o_ref)
        up_stage()

    @pl.when(jnp.logical_and(f > 0, f < n_tiles))
    def _():
        o_ref[...] += down_product()
        up_stage()

    @pl.when(f == n_tiles)
    def _():
        y = x_ref[...] + o_ref[...] + down_product()
        if final_norm:
            ms = jnp.mean(y * y, axis=-1, keepdims=True)
            y = y * lax.rsqrt(ms + EPS) * fg_ref[...]
        o_ref[...] = y


def _ffn(x, g, w_up, dw, dwb, w_down, layer, final_g, final_norm, tm=1024, tf=W_TILE):
    S, D = x.shape
    nf = D_FF // tf
    per = tm // HALO
    dw_tiles = dw.reshape(FFN_CONV_K, 2 * nf, tf).transpose(1, 0, 2)
    b_tiles = dwb.reshape(2 * nf, 1, tf)
    row = lambda shape: pl.BlockSpec(shape, lambda i, f: (0, 0))
    up = lambda f: jnp.minimum(f, nf - 1)
    down = lambda f: jnp.maximum(f - 1, 0)
    return pl.pallas_call(
        functools.partial(_ffn_kernel, final_norm=final_norm, n_tiles=nf, up_chunk=tf),
        grid=(S // tm, nf + 1),
        in_specs=[pl.BlockSpec((HALO, D), lambda i, f: (jnp.maximum(i * per - 1, 0), 0)),
                  pl.BlockSpec((tm, D), lambda i, f: (i, 0), pipeline_mode=pl.Buffered(1)),
                  row((1, D)),
                  pl.BlockSpec((1, D, tf), lambda i, f: (layer, 0, up(f))),
                  pl.BlockSpec((1, D, tf), lambda i, f: (layer, 0, up(f) + nf)),
                  pl.BlockSpec((2 * nf, FFN_CONV_K, tf), lambda i, f: (0, 0, 0)),
                  pl.BlockSpec((2 * nf, 1, tf), lambda i, f: (0, 0, 0)),
                  pl.BlockSpec((1, tf, D), lambda i, f: (layer, down(f), 0)),
                  row((1, D))],
        out_specs=pl.BlockSpec((tm, D), lambda i, f: (i, 0)),
        out_shape=jax.ShapeDtypeStruct((S, D), F32),
        scratch_shapes=[pltpu.VMEM((HALO + tm, D), BF16), pltpu.VMEM((2, tm, tf), BF16)],
        compiler_params=_params(("parallel", "arbitrary")),
        name="conv_ffn",
    )(x, x, g, w_up, w_up, dw_tiles, b_tiles, w_down, final_g)


def kernel(x, norm1_g, w_in, conv_dw, conv_dw_b, conv_ln_g, conv_ln_b, w_conv_out, rel_bias,
           w_attn_out, w_out, norm2_g, w_up, ffn_dw, ffn_dw_b, w_down, final_g):
    B, S, D = x.shape
    assert D == D_MODEL and S % INPROJ_ROWS == 0 and INPROJ_ROWS % GROUP_KEYS == 0
    depth = w_in.shape[0]
    bias = _bias_tiles(rel_bias)
    row = lambda v: v.reshape(1, -1)
    w_in_bf = w_in.astype(BF16)
    w_conv_bf, w_attn_bf, w_out_bf = (w.astype(BF16) for w in (w_conv_out, w_attn_out, w_out))
    w_up_bf, w_down_bf = w_up.astype(BF16), w_down.astype(BF16)
    outs = []
    for b in range(B):
        xb = x.reshape(S, D) if B == 1 else x[b]
        for l in range(depth):
            glu, k, km, q_t, v_t, gates = _inproj_fused(xb, row(norm1_g[l]), w_in_bf, l)
            cs = _conv_branch(glu, conv_dw[l], row(conv_dw_b[l]), row(conv_ln_g[l]), row(conv_ln_b[l]))
            attn = _attention(k, km, q_t, v_t, bias, rel_bias)
            xb = _mix(cs, attn, gates, xb, w_conv_bf, w_attn_bf, w_out_bf, l)
            xb = _ffn(xb, row(norm2_g[l]), w_up_bf, ffn_dw[l], row(ffn_dw_b[l]),
                      w_down_bf, l, row(final_g), final_norm=(l == depth - 1))
        outs.append(xb)
    return outs[0].reshape(1, S, D) if B == 1 else jnp.stack(outs, axis=0)
```
